```python
import math
import jax
import jax.numpy as jnp
from jax import lax
import numpy as np

D_MODEL = 1024
BATCH = 2
SEQ = 8192
DEPTH = 2
DEC_BATCH = 32
DEC_SEQ = 4
PAST_LEN = 16384
PAGE_SIZE = 128

N_EVEN = (DEPTH + 1) // 2
N_ODD = DEPTH // 2
CONV_CH = D_MODEL // 2
CONV_W = 3
DA_HEADS = 4
DA_DIM = D_MODEL // (4 * DA_HEADS)
DA_WIDTH = 2 * DA_HEADS * DA_DIM
IN0 = 3 * CONV_CH + 3 * DA_WIDTH
ML_HEADS = 4
ML_DIM = D_MODEL // ML_HEADS
ML_CHUNK = 64
IN1 = 4 * D_MODEL + 2 * ML_HEADS
D_FF = -(-8 * D_MODEL // (3 * 256)) * 256
Q_BLOCK = 128
EPS = 1e-6

kernel_name = "hybrid_conv_diffattn_mlstm_decode_step"


def rmsnorm(x, g):
    xf = x.astype(jnp.float32)
    y = xf * lax.rsqrt(jnp.mean(xf * xf, axis=-1, keepdims=True) + EPS)
    return (y * g.astype(jnp.float32)).astype(x.dtype)


def swiglu(x, w_gu, w_down):
    gate, up = jnp.split(x @ w_gu, 2, axis=-1)
    return (jax.nn.silu(gate) * up) @ w_down


def lambda_init(layer):
    return 0.8 - 0.6 * math.exp(-0.3 * layer)


def short_conv(u, conv_w, buf):
    s = u.shape[1]
    ext = jnp.concatenate([buf.astype(u.dtype), u], axis=1)
    y = ext[:, 0:s] * conv_w[0]
    for j in range(1, CONV_W):
        y = y + ext[:, j:j + s] * conv_w[j]
    return y, ext[:, s:]


def diff_attend(q, k, v, mask, lam):
    s = jnp.einsum('bqhcd,bkhcd->bhcqk', q, k).astype(jnp.float32) * (DA_DIM ** -0.5)
    s = jnp.where(mask, s, -jnp.inf)
    p = jax.nn.softmax(s, axis=-1)
    w = p[:, :, 0] - lam * p[:, :, 1]
    return jnp.einsum('bhqk,bkhe->bqhe', w.astype(v.dtype), v)


def mixer_conv_diffattn(u, conv_buf, past_k, past_v, w_in, conv_w, lam_vec, subln_g, w_out, lam_init):
    b, s, _ = u.shape
    proj = u @ w_in
    gate_b, gate_c, xin, q, k, v = jnp.split(
        proj, [CONV_CH, 2 * CONV_CH, 3 * CONV_CH, 3 * CONV_CH + DA_WIDTH, 3 * CONV_CH + 2 * DA_WIDTH], axis=-1)
    conv_y, conv_new = short_conv(gate_c * xin, conv_w, conv_buf)
    y_conv = gate_b * conv_y
    lf = lam_vec.astype(jnp.float32)
    lam = jnp.exp(jnp.sum(lf[0] * lf[1])) - jnp.exp(jnp.sum(lf[2] * lf[3])) + lam_init
    q = q.reshape(b, s, DA_HEADS, 2, DA_DIM)
    k_rows = k.reshape(b, s, DA_HEADS, 2 * DA_DIM)
    v_rows = v.reshape(b, s, DA_HEADS, 2 * DA_DIM)
    if past_k is None:
        keys = k_rows.reshape(b, s, DA_HEADS, 2, DA_DIM)
        n_blk = s // Q_BLOCK
        q_blocks = jnp.moveaxis(q.reshape(b, n_blk, Q_BLOCK, DA_HEADS, 2, DA_DIM), 1, 0)
        k_pos = jnp.arange(s)

        def one_block(args):
            q_blk, i = args
            q_pos = i * Q_BLOCK + jnp.arange(Q_BLOCK)
            return diff_attend(q_blk, keys, v_rows, k_pos[None, :] <= q_pos[:, None], lam)

        attn = lax.map(one_block, (q_blocks, jnp.arange(n_blk)))
        attn = jnp.moveaxis(attn, 0, 1).reshape(b, s, DA_HEADS, 2 * DA_DIM)
    else:
        p = past_k.shape[1]
        keys = jnp.concatenate([past_k.astype(k_rows.dtype), k_rows], axis=1).reshape(b, p + s, DA_HEADS, 2, DA_DIM)
        vals = jnp.concatenate([past_v.astype(v_rows.dtype), v_rows], axis=1)
        mask = jnp.arange(p + s)[None, :] <= (p + jnp.arange(s))[:, None]
        attn = diff_attend(q, keys, vals, mask, lam)
    attn = rmsnorm(attn, subln_g) * (1.0 - lam_init)
    mixed = jnp.concatenate([y_conv, attn.reshape(b, s, DA_WIDTH)], axis=-1) @ w_out
    return mixed, k_rows, v_rows, conv_new


def mlstm_chunkwise(q, k, v, log_i, log_f, c0, n0, m0):
    b, s, h, dh = q.shape
    L = ML_CHUNK if s % ML_CHUNK == 0 else s
    nc = s // L

    def chunks(a):
        a = a.reshape((b, nc, L, h) + a.shape[3:])
        return jnp.moveaxis(jnp.moveaxis(a, 1, 0), 3, 2)

    tri = jnp.tril(jnp.ones((L, L), dtype=bool))

    def step(carry, xs):
        c, n, m = carry
        qi, ki, vi, li, lf = xs
        bc = jnp.cumsum(lf, axis=-1)
        mt = bc + jnp.maximum(m[..., None], lax.cummax(li - bc, axis=2))
        log_d = bc[..., :, None] - bc[..., None, :] + li[..., None, :] - mt[..., :, None]
        dmat = jnp.exp(jnp.where(tri, log_d, -jnp.inf))
        inter = jnp.exp(bc + m[..., None] - mt)
        sqk = jnp.einsum('bhtd,bhsd->bhts', qi, ki) * dmat
        num = inter[..., None] * jnp.einsum('bhtd,bhde->bhte', qi, c) + jnp.einsum('bhts,bhse->bhte', sqk, vi)
        den = inter * jnp.einsum('bhtd,bhd->bht', qi, n) + jnp.sum(sqk, axis=-1)
        h_til = num / jnp.maximum(jnp.abs(den), jnp.exp(-mt))[..., None]
        m_end = mt[..., -1]
        w_end = jnp.exp(bc[..., -1:] - bc + li - m_end[..., None])
        decay = jnp.exp(bc[..., -1] + m - m_end)
        wk = w_end[..., None] * ki
        c_new = decay[..., None, None] * c + jnp.einsum('bhsd,bhse->bhde', wk, vi)
        n_new = decay[..., None] * n + jnp.sum(wk, axis=2)
        return (c_new, n_new, m_end), h_til

    (c, n, m), hs = lax.scan(step, (c0, n0, m0), (chunks(q), chunks(k), chunks(v), chunks(log_i), chunks(log_f)))
    hs = jnp.transpose(hs, (1, 0, 3, 2, 4)).reshape(b, s, h, dh)
    return hs, c, n, m


def mixer_mlstm(u, c0, n0, m0, w_in, b_if, mh_g, w_out):
    b, s, _ = u.shape
    proj = u @ w_in
    q, k, v, o, gates = jnp.split(proj, [D_MODEL, 2 * D_MODEL, 3 * D_MODEL, 4 * D_MODEL], axis=-1)
    gates = gates.astype(jnp.float32) + b_if.astype(jnp.float32)
    log_i = gates[..., :ML_HEADS]
    log_f = jax.nn.log_sigmoid(gates[..., ML_HEADS:])

    def heads(a):
        return a.astype(jnp.float32).reshape(b, s, ML_HEADS, ML_DIM)

    h_til, c, n, m = mlstm_chunkwise(heads(q), heads(k) * (ML_DIM ** -0.5), heads(v), log_i, log_f,
                                     c0.astype(jnp.float32), n0.astype(jnp.float32), m0.astype(jnp.float32))
    hcell = jax.nn.sigmoid(heads(o)) * h_til
    hc = hcell - jnp.mean(hcell, axis=-1, keepdims=True)
    hn = hc * lax.rsqrt(jnp.mean(hc * hc, axis=-1, keepdims=True) + EPS)
    hn = hn.reshape(b, s, D_MODEL) * mh_g.astype(jnp.float32)
    return hn.astype(u.dtype) @ w_out, c, n, m


def trunk(x, state_conv, cache_k, cache_v, page_table, state_C, state_n, state_m, norms,
          w_in0, conv_w0, lam0, subln_g0, w_out0, w_in1, b_if1, mh_g1, w_out1, w_gu, w_down):
    h = x
    ks, vs, convs, cs, ns, ms = [], [], [], [], [], []
    for layer in range(DEPTH):
        j = layer // 2
        g = norms[layer]
        u = rmsnorm(h, g[0])
        if layer % 2 == 0:
            if cache_k is None:
                past_k = None
                past_v = None
            else:
                nb = page_table.shape[0]
                past_k = cache_k[j, page_table].reshape(nb, -1, DA_HEADS, 2 * DA_DIM)
                past_v = cache_v[j, page_table].reshape(nb, -1, DA_HEADS, 2 * DA_DIM)
            mix, k_rows, v_rows, conv_new = mixer_conv_diffattn(
                u, state_conv[j], past_k, past_v, w_in0[j], conv_w0[j], lam0[j], subln_g0[j], w_out0[j],
                lambda_init(layer))
            ks.append(k_rows)
            vs.append(v_rows)
            convs.append(conv_new)
        else:
            mix, c, n, m = mixer_mlstm(u, state_C[j], state_n[j], state_m[j], w_in1[j], b_if1[j], mh_g1[j], w_out1[j])
            cs.append(c)
            ns.append(n)
            ms.append(m)
        h = h + rmsnorm(mix, g[1])
        h = h + rmsnorm(swiglu(rmsnorm(h, g[2]), w_gu[layer], w_down[layer]), g[3])
    return h, jnp.stack(ks), jnp.stack(vs), jnp.stack(convs), jnp.stack(cs), jnp.stack(ns), jnp.stack(ms)


def setup_inputs(seed: int = 0) -> dict:
    key = jax.random.key(seed)
    ks = jax.random.split(key, 24)
    n_pages = PAST_LEN // PAGE_SIZE
    n_used = DEC_BATCH * n_pages
    n_phys = n_used + n_used // 4

    def nrm(k, shape, scale):
        return jax.random.normal(k, shape, jnp.float32) * scale

    return {
        'x_prompt': nrm(ks[0], (BATCH, SEQ, D_MODEL), 1.0),
        'x_sample': nrm(ks[1], (DEC_BATCH, DEC_SEQ, D_MODEL), 1.0),
        'cache_k': nrm(ks[2], (N_EVEN, n_phys, PAGE_SIZE, DA_HEADS, 2 * DA_DIM), 1.0),
        'cache_v': nrm(ks[3], (N_EVEN, n_phys, PAGE_SIZE, DA_HEADS, 2 * DA_DIM), 1.0),
        'state_conv': nrm(ks[4], (N_EVEN, DEC_BATCH, CONV_W - 1, CONV_CH), 1.0),
        'state_C': nrm(ks[5], (N_ODD, DEC_BATCH, ML_HEADS, ML_DIM, ML_DIM), ML_DIM ** -0.5),
        'state_n': nrm(ks[6], (N_ODD, DEC_BATCH, ML_HEADS, ML_DIM), 1.0),
        'state_m': nrm(ks[7], (N_ODD, DEC_BATCH, ML_HEADS), 1.0),
        'page_table': jax.random.permutation(ks[8], n_phys)[:n_used].reshape(DEC_BATCH, n_pages).astype(jnp.int32),
        'norms': 1.0 + nrm(ks[9], (DEPTH, 4, D_MODEL), 0.05),
        'w_in0': nrm(ks[10], (N_EVEN, D_MODEL, IN0), D_MODEL ** -0.5),
        'conv_w0': nrm(ks[11], (N_EVEN, CONV_W, CONV_CH), CONV_W ** -0.5),
        'lam0': nrm(ks[12], (N_EVEN, 4, DA_DIM), 0.1),
        'subln_g0': 1.0 + nrm(ks[13], (N_EVEN, 2 * DA_DIM), 0.05),
        'w_out0': nrm(ks[14], (N_EVEN, D_MODEL, D_MODEL), D_MODEL ** -0.5),
        'w_in1': nrm(ks[15], (N_ODD, D_MODEL, IN1), D_MODEL ** -0.5),
        'b_if1': jnp.concatenate([nrm(ks[16], (N_ODD, ML_HEADS), 0.1),
                                  3.0 + nrm(ks[17], (N_ODD, ML_HEADS), 0.1)], axis=-1),
        'mh_g1': 1.0 + nrm(ks[18], (N_ODD, D_MODEL), 0.05),
        'w_out1': nrm(ks[19], (N_ODD, D_MODEL, D_MODEL), D_MODEL ** -0.5),
        'w_gu': nrm(ks[20], (DEPTH, D_MODEL, 2 * D_FF), D_MODEL ** -0.5),
        'w_down': nrm(ks[21], (DEPTH, D_FF, D_MODEL), D_FF ** -0.5),
    }


def reference(x_prompt, x_sample, cache_k, cache_v, state_conv, state_C, state_n, state_m, page_table,
              norms, w_in0, conv_w0, lam0, subln_g0, w_out0, w_in1, b_if1, mh_g1, w_out1, w_gu, w_down):
    bp = x_prompt.shape[0]
    conv0 = jnp.zeros((N_EVEN, bp, CONV_W - 1, CONV_CH), x_prompt.dtype)
    c0 = jnp.zeros((N_ODD, bp, ML_HEADS, ML_DIM, ML_DIM), jnp.float32)
    n0 = jnp.zeros((N_ODD, bp, ML_HEADS, ML_DIM), jnp.float32)
    m0 = jnp.zeros((N_ODD, bp, ML_HEADS), jnp.float32)
    y_prompt, k_p, v_p, conv_p, c_p, n_p, m_p = trunk(
        x_prompt, conv0, None, None, None, c0, n0, m0, norms,
        w_in0, conv_w0, lam0, subln_g0, w_out0, w_in1, b_if1, mh_g1, w_out1, w_gu, w_down)
    y_sample, k_s, v_s, conv_s, c_s, n_s, m_s = trunk(
        x_sample, state_conv, cache_k, cache_v, page_table, state_C, state_n, state_m, norms,
        w_in0, conv_w0, lam0, subln_g0, w_out0, w_in1, b_if1, mh_g1, w_out1, w_gu, w_down)
    return (y_prompt, y_sample, k_p, v_p, conv_p, c_p, n_p, m_p, k_s, v_s, conv_s, c_s, n_s, m_s)
```

```python
import functools
import math

import jax
import jax.numpy as jnp
from jax import lax
from jax.experimental import pallas as pl
from jax.experimental.pallas import tpu as pltpu

EPS = 1e-6
F32 = jnp.float32
BF16 = jnp.bfloat16

V7X_VMEM_BYTES = 64 * 1024 * 1024
VMEM_LIMIT_BYTES = V7X_VMEM_BYTES - 8 * 1024 * 1024
LANES = 128
SUBLANES = 8

ROW_TILE = 512
FFN_ROW_TILE = 1024
FFN_COL_TILE = 256
ATTN_Q_TILE = 256
ATTN_K_TILE = 512
DECODE_PAGES = 8
MLSTM_CHUNK = 256
MLSTM_ROWS = 512


def _params(*semantics):
    return pltpu.CompilerParams(dimension_semantics=semantics, vmem_limit_bytes=VMEM_LIMIT_BYTES)


def _dot(a, b):
    return jnp.dot(a, b, preferred_element_type=F32)


def _dot_nt(a, b):
    return lax.dot_general(a, b, (((1,), (1,)), ((), ())), preferred_element_type=F32)


def _dot_tn(a, b):
    return lax.dot_general(a, b, (((0,), (0,)), ((), ())), preferred_element_type=F32)


def _rms(x, g):
    return x * lax.rsqrt(jnp.mean(x * x, axis=-1, keepdims=True) + EPS) * g


def _lambda_init(layer):
    return 0.8 - 0.6 * math.exp(-0.3 * layer)


def _lam_value(lam_ref, lam_init):
    lv = lam_ref[...]
    a = jnp.sum(lv[0:1] * lv[1:2], axis=-1, keepdims=True)
    b = jnp.sum(lv[2:3] * lv[3:4], axis=-1, keepdims=True)
    return jnp.exp(a) - jnp.exp(b) + lam_init


def _row_tile(m, want):
    t = min(want, m)
    assert m % t == 0, (m, t)
    return t


def _inproj0_kernel(x_ref, g_ref, w_ref, gb_ref, cx_ref, q_ref, k_ref, v_ref, kb_ref, vb_ref,
                    *, cw, aw, qscale):
    xn = _rms(x_ref[...], g_ref[...]).astype(BF16)

    def proj(lo, width):
        return _dot(xn, w_ref[:, lo:lo + width])

    gb_ref[...] = proj(0, cw)
    cx_ref[...] = proj(cw, cw) * proj(2 * cw, cw)
    q_ref[...] = (proj(3 * cw, aw) * qscale).astype(BF16)
    k = proj(3 * cw + aw, aw)
    k_ref[...] = k
    kb_ref[...] = k.astype(BF16)
    v = proj(3 * cw + 2 * aw, aw)
    v_ref[...] = v
    vb_ref[...] = v.astype(BF16)


def _inproj0(x, g, w, cw, aw, qscale):
    m, d = x.shape
    tm = _row_tile(m, ROW_TILE)
    n = w.shape[1]
    row = lambda width: pl.BlockSpec((tm, width), lambda i: (i, 0))
    shapes = [(cw, F32), (cw, F32), (aw, BF16), (aw, F32), (aw, F32), (aw, BF16), (aw, BF16)]
    return pl.pallas_call(
        functools.partial(_inproj0_kernel, cw=cw, aw=aw, qscale=qscale),
        grid=(m // tm,),
        in_specs=[row(d), pl.BlockSpec((1, d), lambda i: (0, 0)), pl.BlockSpec((d, n), lambda i: (0, 0))],
        out_specs=[row(wd) for wd, _ in shapes],
        out_shape=[jax.ShapeDtypeStruct((m, wd), dt) for wd, dt in shapes],
        compiler_params=_params("parallel"),
        name="inproj0",
    )(x, g, w)


def _attn_prompt_kernel(lam_ref, sg_ref, q_ref, k_ref, v_ref, o_ref, *, tq, tk, dh, lam_init):
    qi = pl.program_id(2)
    q = q_ref[...]
    lane = lax.broadcasted_iota(jnp.int32, q.shape, 1)
    zero = jnp.zeros_like(q)
    qq = jnp.concatenate([jnp.where(lane < dh, q, zero), jnp.where(lane >= dh, q, zero)], axis=0)
    rows = 2 * tq

    def step(j, carry, masked):
        m, l, acc = carry
        off = pl.multiple_of(j * tk, tk)
        k = k_ref[pl.ds(off, tk), :]
        v = v_ref[pl.ds(off, tk), :]
        s = _dot_nt(qq, k)
        if masked:
            r = lax.broadcasted_iota(jnp.int32, (rows, tk), 0)
            c = lax.broadcasted_iota(jnp.int32, (rows, tk), 1)
            qpos = qi * tq + jnp.where(r >= tq, r - tq, r)
            s = jnp.where(off + c <= qpos, s, -jnp.inf)
        m_new = jnp.maximum(m, jnp.max(s, axis=-1, keepdims=True))
        alpha = jnp.exp(m - m_new)
        p = jnp.exp(s - m_new)
        l = alpha * l + jnp.sum(p, axis=-1, keepdims=True)
        acc = alpha * acc + _dot(p.astype(BF16), v)
        return m_new, l, acc

    n_full = (qi * tq) // tk
    init = (jnp.full((rows, 1), -jnp.inf, F32), jnp.zeros((rows, 1), F32), jnp.zeros((rows, 2 * dh), F32))
    carry = lax.fori_loop(0, n_full, lambda j, c: step(j, c, False), init)
    _, l, acc = step(n_full, carry, True)

    o = acc / l
    a = o[:tq] - _lam_value(lam_ref, lam_init) * o[tq:]
    o_ref[...] = (_rms(a, sg_ref[...]) * (1.0 - lam_init)).astype(o_ref.dtype)


def _attn_prompt(qb, kb, vb, lam, subln_g, nh, dh, lam_init):
    b, s, _ = qb.shape
    tq = _row_tile(s, ATTN_Q_TILE)
    tk = _row_tile(s, ATTN_K_TILE)
    assert tk % tq == 0
    hw = 2 * dh
    assert hw == LANES
    kv_spec = pl.BlockSpec((None, s, hw), lambda bi, h, qi: (bi, 0, h))
    q_spec = pl.BlockSpec((None, tq, hw), lambda bi, h, qi: (bi, qi, h))
    return pl.pallas_call(
        functools.partial(_attn_prompt_kernel, tq=tq, tk=tk, dh=dh, lam_init=lam_init),
        grid=(b, nh, s // tq),
        in_specs=[pl.BlockSpec(lam.shape, lambda bi, h, qi: (0, 0)),
                  pl.BlockSpec((1, hw), lambda bi, h, qi: (0, 0)),
                  q_spec, kv_spec, kv_spec],
        out_specs=q_spec,
        out_shape=jax.ShapeDtypeStruct((b, s, nh * hw), BF16),
        compiler_params=_params("parallel", "parallel", "arbitrary"),
        name="attn_prompt",
    )(lam, subln_g, qb, kb, vb)


def _attn_decode_kernel(pt_ref, lam_ref, sg_ref, q_ref, kn_ref, vn_ref, *refs,
                        pages, nh, dh, nq, lam_init):
    del pt_ref
    k_refs = refs[:pages]
    v_refs = refs[pages:2 * pages]
    o_ref = refs[2 * pages]
    m_ref, l_ref, acc_ref = refs[2 * pages + 1:]
    step = pl.program_id(1)
    hw = 2 * dh
    page = k_refs[0].shape[0]

    @pl.when(step == 0)
    def _():
        m_ref[...] = jnp.full(m_ref.shape, -jnp.inf, F32)
        l_ref[...] = jnp.zeros(l_ref.shape, F32)
        acc_ref[...] = jnp.zeros(acc_ref.shape, F32)

    for h in range(nh):
        qh = q_ref[h]
        s = jnp.concatenate(
            [_dot_nt(qh, k_refs[r][:, h, :].astype(BF16)) for r in range(pages)], axis=1)
        m_old = m_ref[h]
        m_new = jnp.maximum(m_old, jnp.max(s, axis=-1, keepdims=True))
        alpha = jnp.exp(m_old - m_new)
        p = jnp.exp(s - m_new)
        l_ref[h] = alpha * l_ref[h] + jnp.sum(p, axis=-1, keepdims=True)
        acc = alpha * acc_ref[h]
        for r in range(pages):
            acc = acc + _dot(p[:, r * page:(r + 1) * page].astype(BF16),
                             v_refs[r][:, h, :].astype(BF16))
        acc_ref[h] = acc
        m_ref[h] = m_new

    @pl.when(step == pl.num_programs(1) - 1)
    def _():
        lam = _lam_value(lam_ref, lam_init)
        rowi = lax.broadcasted_iota(jnp.int32, (2 * nq, 1), 0)
        qidx = jnp.where(rowi >= nq, rowi - nq, rowi)
        for h in range(nh):
            qh = q_ref[h].astype(F32)
            lanes = slice(h * hw, (h + 1) * hw)
            sj = [jnp.sum(qh * kn_ref[j:j + 1, lanes].astype(BF16).astype(F32), axis=-1, keepdims=True)
                  for j in range(nq)]
            m_old = m_ref[h]
            m_new = m_old
            for j in range(nq):
                m_new = jnp.where(qidx >= j, jnp.maximum(m_new, sj[j]), m_new)
            alpha = jnp.exp(m_old - m_new)
            l = alpha * l_ref[h]
            acc = alpha * acc_ref[h]
            for j in range(nq):
                pj = jnp.where(qidx >= j, jnp.exp(sj[j] - m_new), 0.0)
                l = l + pj
                acc = acc + pj.astype(BF16).astype(F32) * vn_ref[j:j + 1, lanes].astype(BF16).astype(F32)
            o = acc / l
            a = o - lam * pltpu.roll(o, nq, 0)
            y = _rms(a, sg_ref[...]) * (1.0 - lam_init)
            o_ref[:, lanes] = y[:nq].astype(o_ref.dtype)


def _attn_decode(qb, k_new, v_new, cache_k, cache_v, layer_idx, page_table, lam, subln_g,
                 nb, nq, nh, dh, lam_init):
    hw = 2 * dh
    rows = 2 * nq
    assert rows == SUBLANES and hw == LANES
    n_pages = page_table.shape[1]
    page = cache_k.shape[2]
    pages = min(DECODE_PAGES, n_pages)
    assert n_pages % pages == 0
    qh = jnp.tile(qb.reshape(nb, nq, nh, hw).transpose(0, 2, 1, 3), (1, 1, 2, 1))
    r = lax.broadcasted_iota(jnp.int32, (rows, hw), 0)
    c = lax.broadcasted_iota(jnp.int32, (rows, hw), 1)
    qbd = jnp.where((r // nq) == (c // dh), qh, jnp.zeros_like(qh))

    def cache_spec(slot):
        return pl.BlockSpec((None, None, page, nh, hw),
                            lambda b, p, pt: (layer_idx, pt[b, p * pages + slot], 0, 0, 0))

    const = lambda shape: pl.BlockSpec(shape, lambda b, p, pt: (0,) * len(shape))
    per_b = lambda shape: pl.BlockSpec((None,) + shape, lambda b, p, pt: (b,) + (0,) * len(shape))
    grid_spec = pltpu.PrefetchScalarGridSpec(
        num_scalar_prefetch=1,
        grid=(nb, n_pages // pages),
        in_specs=[const(lam.shape), const((1, hw)), per_b((nh, rows, hw)),
                  per_b((nq, nh * hw)), per_b((nq, nh * hw))]
                 + [cache_spec(i) for i in range(pages)] * 2,
        out_specs=per_b((nq, nh * hw)),
        scratch_shapes=[pltpu.VMEM((nh, rows, 1), F32), pltpu.VMEM((nh, rows, 1), F32),
                        pltpu.VMEM((nh, rows, hw), F32)],
    )
    out = pl.pallas_call(
        functools.partial(_attn_decode_kernel, pages=pages, nh=nh, dh=dh, nq=nq, lam_init=lam_init),
        grid_spec=grid_spec,
        out_shape=jax.ShapeDtypeStruct((nb, nq, nh * hw), F32),
        compiler_params=_params("parallel", "arbitrary"),
        name="attn_decode",
    )(page_table, lam, subln_g, qbd, k_new.reshape(nb, nq, nh * hw), v_new.reshape(nb, nq, nh * hw),
      *([cache_k] * pages), *([cache_v] * pages))
    return out.reshape(nb * nq, nh * hw)


def _conv_taps(cx, prev1, prev2, cw_ref):
    return cw_ref[0:1, :] * prev2 + cw_ref[1:2, :] * prev1 + cw_ref[2:3, :] * cx


def _outproj_conv_kernel(gb_ref, cx_ref, at_ref, r_ref, w_ref, g_ref, cw_ref, *rest, seq, carry_mode):
    cx = cx_ref[...]
    tm, cwid = cx.shape
    row = lax.broadcasted_iota(jnp.int32, (tm, 1), 0)
    roll1 = pltpu.roll(cx, 1, 0)
    roll2 = pltpu.roll(cx, 2, 0)
    if carry_mode:
        st_ref, o_ref, carry = rest

        @pl.when((pl.program_id(0) * tm) % seq == 0)
        def _():
            carry[0:2, :] = st_ref[...]

        c0 = carry[0:1, :]
        c1 = carry[1:2, :]
        prev1 = jnp.where(row >= 1, roll1, c1)
        prev2 = jnp.where(row >= 2, roll2, jnp.where(row == 1, c1, c0))
        carry[0:2, :] = cx[tm - 2:tm, :]
    else:
        e1_ref, e2_ref, o_ref = rest
        t = row % seq
        prev1 = jnp.where(t >= 1, roll1, e1_ref[...])
        prev2 = jnp.where(t >= 2, roll2, e2_ref[...])
    yconv = (gb_ref[...] * _conv_taps(cx, prev1, prev2, cw_ref)).astype(BF16)
    mix = _dot(yconv, w_ref[0:cwid, :]) + _dot(at_ref[...].astype(BF16), w_ref[cwid:, :])
    o_ref[...] = r_ref[...] + _rms(mix, g_ref[...])


def _outproj_conv(gb, cx, attn, resid, w, g, conv_w, state, seq):
    m, d = resid.shape
    cwid = gb.shape[1]
    tm = _row_tile(m, ROW_TILE)
    carry_mode = seq % tm == 0
    row = lambda width: pl.BlockSpec((tm, width), lambda i: (i, 0))
    const = lambda shape: pl.BlockSpec(shape, lambda i: (0,) * len(shape))
    in_specs = [row(cwid), row(cwid), row(attn.shape[1]), row(d), const(w.shape), const((1, d)),
                const(conv_w.shape)]
    if carry_mode:
        extra = [state]
        in_specs += [pl.BlockSpec((None, 2, cwid), lambda i: ((i * tm) // seq, 0, 0))]
        scratch = [pltpu.VMEM((SUBLANES, cwid), F32)]
    else:
        assert tm % seq == 0 and seq >= 2
        nb = m // seq
        zeros = jnp.zeros((nb, seq - 1, cwid), F32)
        e1 = jnp.concatenate([state[:, 1:2], zeros], axis=1).reshape(m, cwid)
        e2 = jnp.concatenate([state[:, 0:1], state[:, 1:2], zeros[:, 1:]], axis=1).reshape(m, cwid)
        extra = [e1, e2]
        in_specs += [row(cwid), row(cwid)]
        scratch = []
    return pl.pallas_call(
        functools.partial(_outproj_conv_kernel, seq=seq, carry_mode=carry_mode),
        grid=(m // tm,),
        in_specs=in_specs,
        out_specs=row(d),
        out_shape=jax.ShapeDtypeStruct((m, d), F32),
        scratch_shapes=scratch,
        compiler_params=_params("arbitrary"),
        name="outproj_conv",
    )(gb, cx, attn, resid, w, g, conv_w, *extra)


def _outproj_kernel(x_ref, r_ref, w_ref, g_ref, o_ref):
    o_ref[...] = r_ref[...] + _rms(_dot(x_ref[...], w_ref[...]), g_ref[...])


def _outproj(x, resid, w, g):
    m, d = resid.shape
    tm = _row_tile(m, ROW_TILE)
    row = lambda width: pl.BlockSpec((tm, width), lambda i: (i, 0))
    return pl.pallas_call(
        _outproj_kernel,
        grid=(m // tm,),
        in_specs=[row(x.shape[1]), row(d), pl.BlockSpec(w.shape, lambda i: (0, 0)),
                  pl.BlockSpec((1, d), lambda i: (0, 0))],
        out_specs=row(d),
        out_shape=jax.ShapeDtypeStruct((m, d), F32),
        compiler_params=_params("parallel"),
        name="outproj",
    )(x, resid, w, g)


def _ffn_kernel(x_ref, g_in_ref, wg_ref, wu_ref, wd_ref, g_out_ref, o_ref, xn_ref, acc_ref):
    f = pl.program_id(1)

    @pl.when(f == 0)
    def _():
        xn_ref[...] = _rms(x_ref[...], g_in_ref[...]).astype(BF16)
        acc_ref[...] = jnp.zeros(acc_ref.shape, F32)

    xn = xn_ref[...]
    gate = _dot(xn, wg_ref[...])
    up = _dot(xn, wu_ref[...])
    act = (gate * jax.nn.sigmoid(gate) * up).astype(BF16)
    acc_ref[...] += _dot(act, wd_ref[...])

    @pl.when(f == pl.num_programs(1) - 1)
    def _():
        o_ref[...] = x_ref[...] + _rms(acc_ref[...], g_out_ref[...])


def _ffn(x, g_in, w_gu, w_down, g_out):
    m, d = x.shape
    dff = w_down.shape[0]
    tm = _row_tile(m, FFN_ROW_TILE)
    tf = _row_tile(dff, FFN_COL_TILE)
    nf = dff // tf
    row = pl.BlockSpec((tm, d), lambda i, f: (i, 0))
    vec = pl.BlockSpec((1, d), lambda i, f: (0, 0))
    return pl.pallas_call(
        _ffn_kernel,
        grid=(m // tm, nf),
        in_specs=[row, vec,
                  pl.BlockSpec((d, tf), lambda i, f: (0, f)),
                  pl.BlockSpec((d, tf), lambda i, f: (0, nf + f)),
                  pl.BlockSpec((tf, d), lambda i, f: (f, 0)),
                  vec],
        out_specs=row,
        out_shape=jax.ShapeDtypeStruct((m, d), F32),
        scratch_shapes=[pltpu.VMEM((tm, d), BF16), pltpu.VMEM((tm, d), F32)],
        compiler_params=_params("parallel", "arbitrary"),
        name="ffn",
    )(x, g_in, w_gu, w_gu, w_down, g_out)


def _inproj1_kernel(x_ref, g_ref, w_ref, wg_ref, b_ref, q_ref, k_ref, v_ref, o_ref, gt_ref,
                    *, d, nh, kscale):
    xn = _rms(x_ref[...], g_ref[...]).astype(BF16)
    q_ref[...] = _dot(xn, w_ref[:, 0:d]).astype(BF16)
    k_ref[...] = (_dot(xn, w_ref[:, d:2 * d]) * kscale).astype(BF16)
    v_ref[...] = _dot(xn, w_ref[:, 2 * d:3 * d]).astype(BF16)
    o_ref[...] = _dot(xn, w_ref[:, 3 * d:4 * d])
    gates = _dot(xn, wg_ref[...]) + b_ref[...]
    lane = lax.broadcasted_iota(jnp.int32, gates.shape, 1)
    gt_ref[...] = jnp.where(lane < nh, gates, jax.nn.log_sigmoid(gates))


def _inproj1(x, g, w, wg, b, nh, kscale):
    m, d = x.shape
    tm = _row_tile(m, ROW_TILE)
    row = lambda width: pl.BlockSpec((tm, width), lambda i: (i, 0))
    const = lambda shape: pl.BlockSpec(shape, lambda i: (0,) * len(shape))
    shapes = [(d, BF16), (d, BF16), (d, BF16), (d, F32), (LANES, F32)]
    return pl.pallas_call(
        functools.partial(_inproj1_kernel, d=d, nh=nh, kscale=kscale),
        grid=(m // tm,),
        in_specs=[row(d), const((1, d)), const(w.shape), const(wg.shape), const((1, LANES))],
        out_specs=[row(wd) for wd, _ in shapes],
        out_shape=[jax.ShapeDtypeStruct((m, wd), dt) for wd, dt in shapes],
        compiler_params=_params("parallel"),
        name="inproj1",
    )(x, g, w, wg, b)


def _mlstm_kernel(q_ref, k_ref, v_ref, o_ref, gt_ref, c0_ref, n0_ref, m0_ref, mhg_ref,
                  hn_ref, c_out, n_out, m_out, c_s, n_s, m_s, *, chunk, nchunks, nh):
    h = pl.program_id(1)
    si = pl.program_id(2)

    @pl.when(si == 0)
    def _():
        c_s[...] = c0_ref[...]
        n_s[...] = n0_ref[...]
        m_s[...] = m0_ref[...]

    L = chunk
    row = lax.broadcasted_iota(jnp.int32, (L, L), 0)
    col = lax.broadcasted_iota(jnp.int32, (L, L), 1)
    tri = col <= row
    eye = col == row
    lane = lax.broadcasted_iota(jnp.int32, (L, LANES), 1)

    for c in range(nchunks):
        sl = pl.ds(c * L, L)
        gt = gt_ref[sl, :]
        li_c = jnp.sum(jnp.where(lane == h, gt, 0.0), axis=1, keepdims=True)
        lf_c = jnp.sum(jnp.where(lane == nh + h, gt, 0.0), axis=1, keepdims=True)
        lf_r = jnp.sum(jnp.where(eye, lf_c, 0.0), axis=0, keepdims=True)
        li_r = jnp.sum(jnp.where(eye, li_c, 0.0), axis=0, keepdims=True)
        bc_c = jnp.sum(jnp.where(tri, lf_r, 0.0), axis=1, keepdims=True)
        bc_r = jnp.sum(jnp.where(row <= col, lf_c, 0.0), axis=0, keepdims=True)
        x_r = li_r - bc_r
        cm_c = jnp.max(jnp.where(tri, x_r, -jnp.inf), axis=1, keepdims=True)
        m_prev = m_s[0:1, 0:1]
        mt_c = bc_c + jnp.maximum(m_prev, cm_c)
        dmat = jnp.exp(jnp.where(tri, (bc_c - mt_c) + x_r, -jnp.inf))
        inter = jnp.exp(bc_c + m_prev - mt_c)

        q = q_ref[sl, :]
        k = k_ref[sl, :]
        v = v_ref[sl, :]
        sqk = _dot_nt(q, k) * dmat
        c_old = c_s[...]
        n_old = n_s[...]
        num = inter * _dot(q, c_old.astype(BF16)) + _dot(sqk.astype(BF16), v)
        den = (inter * jnp.sum(q.astype(F32) * n_old, axis=1, keepdims=True)
               + jnp.sum(sqk, axis=1, keepdims=True))
        h_til = num / jnp.maximum(jnp.abs(den), jnp.exp(-mt_c))

        m_end = mt_c[L - 1:L, :]
        bc_end = bc_c[L - 1:L, :]
        w_end = jnp.exp(bc_end - bc_c + li_c - m_end)
        decay = jnp.exp(bc_end + m_prev - m_end)
        wk = w_end * k.astype(F32)
        c_s[...] = decay * c_old + _dot_tn(wk.astype(BF16), v)
        n_s[...] = decay * n_old + jnp.sum(wk, axis=0, keepdims=True)
        m_s[...] = jnp.broadcast_to(m_end, m_s.shape)

        hcell = jax.nn.sigmoid(o_ref[sl, :]) * h_til
        hc = hcell - jnp.mean(hcell, axis=-1, keepdims=True)
        hn = hc * lax.rsqrt(jnp.mean(hc * hc, axis=-1, keepdims=True) + EPS) * mhg_ref[...]
        hn_ref[sl, :] = hn.astype(hn_ref.dtype)

    @pl.when(si == pl.num_programs(2) - 1)
    def _():
        c_out[...] = c_s[...]
        n_out[...] = n_s[...]
        m_out[...] = m_s[...]


def _mlstm(q, k, v, o, gt, c0, n0, m0, mh_g, nh, chunk, rows_per_step):
    b, s, d = q.shape
    dh = d // nh
    assert rows_per_step % chunk == 0 and s % rows_per_step == 0
    blk = lambda width: pl.BlockSpec((None, rows_per_step, width), lambda bi, h, si: (bi, si, h))
    st = lambda r, w: pl.BlockSpec((None, None, r, w), lambda bi, h, si: (bi, h, 0, 0))
    return pl.pallas_call(
        functools.partial(_mlstm_kernel, chunk=chunk, nchunks=rows_per_step // chunk, nh=nh),
        grid=(b, nh, s // rows_per_step),
        in_specs=[blk(dh), blk(dh), blk(dh), blk(dh),
                  pl.BlockSpec((None, rows_per_step, LANES), lambda bi, h, si: (bi, si, 0)),
                  st(dh, dh), st(1, dh), st(SUBLANES, LANES),
                  pl.BlockSpec((1, dh), lambda bi, h, si: (0, h))],
        out_specs=[blk(dh), st(dh, dh), st(1, dh), st(SUBLANES, LANES)],
        out_shape=[jax.ShapeDtypeStruct((b, s, d), BF16),
                   jax.ShapeDtypeStruct((b, nh, dh, dh), F32),
                   jax.ShapeDtypeStruct((b, nh, 1, dh), F32),
                   jax.ShapeDtypeStruct((b, nh, SUBLANES, LANES), F32)],
        scratch_shapes=[pltpu.VMEM((dh, dh), F32), pltpu.VMEM((1, dh), F32),
                        pltpu.VMEM((SUBLANES, LANES), F32)],
        compiler_params=_params("parallel", "parallel", "arbitrary"),
        name="mlstm",
    )(q, k, v, o, gt, c0, n0, m0, mh_g)


def _trunk(x, state_conv, cache_k, cache_v, page_table, state_c, state_n, state_m, wts):
    b, s, d = x.shape
    m = b * s
    depth = wts["norms"].shape[0]
    cw = wts["conv_w0"].shape[2]
    dh = wts["lam0"].shape[2]
    aw = (wts["w_in0"].shape[2] - 3 * cw) // 3
    nh_a = aw // (2 * dh)
    nh_m = state_c.shape[2]
    dh_m = d // nh_m
    h = x.reshape(m, d)
    ks, vs, convs, cs, ns, ms = [], [], [], [], [], []
    for layer in range(depth):
        j = layer // 2
        g = wts["norms"][layer]
        if layer % 2 == 0:
            lam_init = _lambda_init(layer)
            gb, cx, qb, k, v, kb, vb = _inproj0(h, g[0:1], wts["w_in0"][j], cw, aw, dh ** -0.5)
            sg = wts["subln_g0"][j].reshape(1, 2 * dh)
            if cache_k is None:
                attn = _attn_prompt(qb.reshape(b, s, aw), kb.reshape(b, s, aw), vb.reshape(b, s, aw),
                                    wts["lam0"][j], sg, nh_a, dh, lam_init).reshape(m, aw)
            else:
                attn = _attn_decode(qb, k, v, cache_k, cache_v, j, page_table, wts["lam0"][j], sg,
                                    b, s, nh_a, dh, lam_init)
            h = _outproj_conv(gb, cx, attn, h, wts["w_out0"][j], g[1:2], wts["conv_w0"][j],
                              state_conv[j], s)
            ks.append(k.reshape(b, s, nh_a, 2 * dh))
            vs.append(v.reshape(b, s, nh_a, 2 * dh))
            convs.append(cx.reshape(b, s, cw)[:, s - 2:, :])
        else:
            q, k, v, o, gt = _inproj1(h, g[0:1], wts["w_in1"][j], wts["w_in1_gates"][j], wts["b_if1"][j],
                                      nh_m, dh_m ** -0.5)
            chunk = MLSTM_CHUNK if s % MLSTM_CHUNK == 0 else s
            sp = s
            r3 = lambda a: a.reshape(b, s, a.shape[1])
            q, k, v, o, gt = r3(q), r3(k), r3(v), r3(o), r3(gt)
            if chunk % SUBLANES != 0:
                sp = -(-s // SUBLANES) * SUBLANES
                chunk = sp
                pad = lambda a: jnp.pad(a, ((0, 0), (0, sp - s), (0, 0)))
                lane = lax.broadcasted_iota(jnp.int32, (b, sp - s, LANES), 2)
                gt_pad = jnp.where(lane < nh_m, -jnp.inf, 0.0).astype(F32)
                q, k, v, o = pad(q), pad(k), pad(v), pad(o)
                gt = jnp.concatenate([gt, gt_pad], axis=1)
            rows = min(MLSTM_ROWS, sp)
            m0 = jnp.broadcast_to(state_m[j][:, :, None, None], (b, nh_m, SUBLANES, LANES))
            hn, c_new, n_new, m_new = _mlstm(q, k, v, o, gt, state_c[j], state_n[j][:, :, None, :], m0,
                                             wts["mh_g1"][j].reshape(1, d), nh_m, chunk, rows)
            hn = hn[:, :s].reshape(m, d)
            h = _outproj(hn, h, wts["w_out1"][j], g[1:2])
            cs.append(c_new)
            ns.append(n_new[:, :, 0, :])
            ms.append(m_new[:, :, 0, 0])
        h = _ffn(h, g[2:3], wts["w_gu"][layer], wts["w_down"][layer], g[3:4])
    return (h.reshape(b, s, d), jnp.stack(ks), jnp.stack(vs), jnp.stack(convs),
            jnp.stack(cs), jnp.stack(ns), jnp.stack(ms))


def kernel(x_prompt, x_sample, cache_k, cache_v, state_conv, state_C, state_n, state_m, page_table,
           norms, w_in0, conv_w0, lam0, subln_g0, w_out0, w_in1, b_if1, mh_g1, w_out1, w_gu, w_down):
    d = x_prompt.shape[-1]
    nh_m = state_C.shape[2]
    n_odd = w_in1.shape[0]
    w_gates = jnp.pad(w_in1[:, :, 4 * d:], ((0, 0), (0, 0), (0, LANES - 2 * nh_m))).astype(BF16)
    b_pad = jnp.pad(b_if1.astype(F32), ((0, 0), (0, LANES - 2 * nh_m))).reshape(n_odd, 1, LANES)
    wts = dict(
        norms=norms.astype(F32), w_in0=w_in0.astype(BF16), conv_w0=conv_w0, lam0=lam0.astype(F32),
        subln_g0=subln_g0.astype(F32), w_out0=w_out0.astype(BF16), w_in1=w_in1[:, :, :4 * d].astype(BF16),
        w_in1_gates=w_gates, b_if1=b_pad, mh_g1=mh_g1.astype(F32), w_out1=w_out1.astype(BF16),
        w_gu=w_gu.astype(BF16), w_down=w_down.astype(BF16))
    bp = x_prompt.shape[0]
    n_even = state_conv.shape[0]
    conv0 = jnp.zeros((n_even, bp) + state_conv.shape[2:], x_prompt.dtype)
    c0 = jnp.zeros((n_odd, bp) + state_C.shape[2:], F32)
    n0 = jnp.zeros((n_odd, bp) + state_n.shape[2:], F32)
    m0 = jnp.zeros((n_odd, bp) + state_m.shape[2:], F32)
    y_p, k_p, v_p, conv_p, c_p, n_p, m_p = _trunk(x_prompt, conv0, None, None, None, c0, n0, m0, wts)
    y_s, k_s, v_s, conv_s, c_s, n_s, m_s = _trunk(x_sample, state_conv, cache_k, cache_v, page_table,
                                                  state_C.astype(F32), state_n.astype(F32),
                                                  state_m.astype(F32), wts)
    return (y_p, y_s, k_p, v_p, conv_p, c_p, n_p, m_p, k_s, v_s, conv_s, c_s, n_s, m_s)
```

```python
import functools
import math

import jax
import jax.numpy as jnp
from jax import lax
from jax.experimental import pallas as pl
from jax.experimental.pallas import tpu as pltpu

EPS = 1e-6
F32 = jnp.float32
BF16 = jnp.bfloat16

V7X_VMEM_BYTES = 64 * 1024 * 1024
VMEM_LIMIT_BYTES = V7X_VMEM_BYTES - 8 * 1024 * 1024
LANES = 128
SUBLANES = 8

ROW_TILE = 512
FFN_ROW_TILE = 1024
FFN_COL_TILE = 256
ATTN_Q_TILE = 256
ATTN_K_TILE = 1024
DECODE_PAGES = 16
DECODE_GROUP = 16
MLSTM_CHUNK = 256
MLSTM_ROWS = 512


def _params(*semantics):
    return pltpu.CompilerParams(dimension_semantics=semantics, vmem_limit_bytes=VMEM_LIMIT_BYTES)


def _dot(a, b):
    return jnp.dot(a, b, preferred_element_type=F32)


def _dot_nt(a, b):
    return lax.dot_general(a, b, (((1,), (1,)), ((), ())), preferred_element_type=F32)


def _dot_tn(a, b):
    return lax.dot_general(a, b, (((0,), (0,)), ((), ())), preferred_element_type=F32)


def _rms(x, g):
    return x * lax.rsqrt(jnp.mean(x * x, axis=-1, keepdims=True) + EPS) * g


def _lambda_init(layer):
    return 0.8 - 0.6 * math.exp(-0.3 * layer)


def _lam_value(lam_ref, lam_init):
    lv = lam_ref[...]
    a = jnp.sum(lv[0:1] * lv[1:2], axis=-1, keepdims=True)
    b = jnp.sum(lv[2:3] * lv[3:4], axis=-1, keepdims=True)
    return jnp.exp(a) - jnp.exp(b) + lam_init


def _row_tile(m, want):
    t = min(want, m)
    assert m % t == 0, (m, t)
    return t


def _inproj0_kernel(x_ref, g_ref, w_ref, *rest, cw, aw, qscale, transposed):
    xn = _rms(x_ref[...], g_ref[...]).astype(BF16)

    def proj(lo, width):
        return _dot(xn, w_ref[:, lo:lo + width])

    if transposed:
        wt_ref, gb_ref, cx_ref, k_ref, v_ref, kb_ref, qt_ref, vt_ref = rest
        qt_ref[...] = (_dot_nt(wt_ref[0:aw, :], xn) * qscale).astype(BF16)
        vt_ref[...] = _dot_nt(wt_ref[aw:2 * aw, :], xn).astype(BF16)
    else:
        gb_ref, cx_ref, k_ref, v_ref, q_ref = rest
        q_ref[...] = (proj(3 * cw, aw) * qscale).astype(BF16)
    gb_ref[...] = proj(0, cw)
    cx_ref[...] = proj(cw, cw) * proj(2 * cw, cw)
    k = proj(3 * cw + aw, aw)
    k_ref[...] = k
    v_ref[...] = proj(3 * cw + 2 * aw, aw)
    if transposed:
        kb_ref[...] = k.astype(BF16)


def _inproj0(x, g, w, wt, cw, aw, qscale, transposed):
    m, d = x.shape
    tm = _row_tile(m, ROW_TILE)
    n = w.shape[1]
    row = lambda width: pl.BlockSpec((tm, width), lambda i: (i, 0))
    const = lambda shape: pl.BlockSpec(shape, lambda i: (0,) * len(shape))
    shapes = [(cw, F32), (cw, F32), (aw, F32), (aw, F32), (aw, BF16)]
    out_specs = [row(wd) for wd, _ in shapes]
    out_shape = [jax.ShapeDtypeStruct((m, wd), dt) for wd, dt in shapes]
    in_specs = [row(d), const((1, d)), const((d, n))]
    args = [x, g, w]
    if transposed:
        in_specs.append(const(wt.shape))
        args.append(wt)
        out_specs += [pl.BlockSpec((aw, tm), lambda i: (0, i))] * 2
        out_shape += [jax.ShapeDtypeStruct((aw, m), BF16)] * 2
    return pl.pallas_call(
        functools.partial(_inproj0_kernel, cw=cw, aw=aw, qscale=qscale, transposed=transposed),
        grid=(m // tm,),
        in_specs=in_specs,
        out_specs=out_specs,
        out_shape=out_shape,
        compiler_params=_params("parallel"),
        name="inproj0",
    )(*args)


def _attn_prompt_kernel(lam_ref, sg_ref, qt_ref, k_ref, vt_ref, o_ref, *, tq, tk, dh, lam_init):
    nch = tk // tq
    g = pl.program_id(2)
    qt = qt_ref[...]
    sub = lax.broadcasted_iota(jnp.int32, (2 * dh, tq), 0)
    zero = jnp.zeros((2 * dh, tq), qt.dtype)
    qqs = []
    for c in range(nch):
        qc = qt[:, c * tq:(c + 1) * tq]
        qqs.append(jnp.concatenate([jnp.where(sub < dh, qc, zero), jnp.where(sub >= dh, qc, zero)], axis=1))
    cols = 2 * tq

    def step(j, carry, masked):
        off = pl.multiple_of(j * tk, tk)
        k = k_ref[pl.ds(off, tk), :]
        vt = vt_ref[:, pl.ds(off, tk)]
        nks = [(c + 1) * tq if masked else tk for c in range(nch)]
        scores = [_dot(k[:nks[c]], qqs[c]) for c in range(nch)]
        probs, stats = [], []
        for c in range(nch):
            m, l, _ = carry[c]
            s = scores[c]
            if masked:
                r = lax.broadcasted_iota(jnp.int32, (nks[c], cols), 0)
                cc = lax.broadcasted_iota(jnp.int32, (nks[c], cols), 1)
                s = jnp.where(r <= c * tq + jnp.where(cc >= tq, cc - tq, cc), s, -jnp.inf)
            m_new = jnp.maximum(m, jnp.max(s, axis=0, keepdims=True))
            alpha = jnp.exp(m - m_new)
            p = jnp.exp(s - m_new)
            stats.append((m_new, alpha, alpha * l + jnp.sum(p, axis=0, keepdims=True)))
            probs.append(p.astype(BF16))
        out = []
        for c in range(nch):
            m_new, alpha, l = stats[c]
            acc = alpha * carry[c][2] + _dot(vt[:, :nks[c]], probs[c])
            out.append((m_new, l, acc))
        return tuple(out)

    init = tuple((jnp.full((1, cols), -jnp.inf, F32), jnp.zeros((1, cols), F32),
                  jnp.zeros((2 * dh, cols), F32)) for _ in range(nch))
    carry = lax.fori_loop(0, g, lambda j, c: step(j, c, False), init)
    carry = step(g, carry, True)

    lam = _lam_value(lam_ref, lam_init)
    for c in range(nch):
        _, l, acc = carry[c]
        o = acc / l
        a = o[:, :tq] - lam * o[:, tq:]
        y = a * lax.rsqrt(jnp.mean(a * a, axis=0, keepdims=True) + EPS) * sg_ref[...] * (1.0 - lam_init)
        o_ref[c * tq:(c + 1) * tq, :] = jnp.transpose(y).astype(o_ref.dtype)


def _attn_prompt(qt, kb, vt, lam, subln_g, b, s, nh, dh, lam_init):
    tk = _row_tile(s, ATTN_K_TILE)
    tq = _row_tile(tk, ATTN_Q_TILE)
    hw = 2 * dh
    assert hw == LANES
    nq = s // tk
    return pl.pallas_call(
        functools.partial(_attn_prompt_kernel, tq=tq, tk=tk, dh=dh, lam_init=lam_init),
        grid=(b, nh, nq),
        in_specs=[pl.BlockSpec(lam.shape, lambda bi, h, qi: (0, 0)),
                  pl.BlockSpec((hw, 1), lambda bi, h, qi: (0, 0)),
                  pl.BlockSpec((hw, tk), lambda bi, h, qi: (h, bi * nq + qi)),
                  pl.BlockSpec((None, s, hw), lambda bi, h, qi: (bi, 0, h)),
                  pl.BlockSpec((hw, s), lambda bi, h, qi: (h, bi))],
        out_specs=pl.BlockSpec((None, tk, hw), lambda bi, h, qi: (bi, qi, h)),
        out_shape=jax.ShapeDtypeStruct((b, s, nh * hw), BF16),
        compiler_params=_params("parallel", "parallel", "arbitrary"),
        name="attn_prompt",
    )(lam, subln_g.reshape(hw, 1), qt, kb, vt)


def _attn_decode_kernel(pt_ref, lam_ref, sg_ref, q_ref, kn_ref, vn_ref, *refs,
                        pages, group, nh, nq, lam_init):
    del pt_ref
    k_refs = refs[:pages]
    v_refs = refs[pages:2 * pages]
    o_ref = refs[2 * pages]
    m_ref, l_ref, acc_ref = refs[2 * pages + 1:]
    step = pl.program_id(1)
    q = q_ref[...]
    rows = q.shape[0]
    hw = q.shape[1]
    prow = k_refs[0].shape[0]

    def masks(width):
        r = lax.broadcasted_iota(jnp.int32, (rows, width), 0)
        c = lax.broadcasted_iota(jnp.int32, (rows, width), 1)
        return r, c, (c % nh) == (r // (2 * nq))

    def update(scores, values):
        m_old = m_ref[...]
        m_new = m_old
        for s in scores:
            m_new = jnp.maximum(m_new, jnp.max(s, axis=-1, keepdims=True))
        alpha = jnp.exp(m_old - m_new)
        l = alpha * l_ref[...]
        acc = alpha * acc_ref[...]
        for s, v in zip(scores, values):
            p = jnp.exp(s - m_new)
            l = l + jnp.sum(p, axis=-1, keepdims=True)
            acc = acc + _dot(p.astype(BF16), v)
        m_ref[...] = m_new
        l_ref[...] = l
        acc_ref[...] = acc

    @pl.when(step == 0)
    def _():
        m_ref[...] = jnp.full(m_ref.shape, -jnp.inf, F32)
        l_ref[...] = jnp.zeros(l_ref.shape, F32)
        acc_ref[...] = jnp.zeros(acc_ref.shape, F32)

    _, _, head_ok = masks(prow)
    for g0 in range(0, pages, group):
        scores = [jnp.where(head_ok, _dot_nt(q, k_refs[r][...].astype(BF16)), -jnp.inf)
                  for r in range(g0, g0 + group)]
        update(scores, [v_refs[r][...].astype(BF16) for r in range(g0, g0 + group)])

    @pl.when(step == pl.num_programs(1) - 1)
    def _():
        width = kn_ref.shape[0]
        r, c, ok = masks(width)
        ok = ok & ((c // nh) <= (r % nq)) & (c < nq * nh)
        s = jnp.where(ok, _dot_nt(q, kn_ref[...].astype(BF16)), -jnp.inf)
        update([s], [vn_ref[...].astype(BF16)])
        o = acc_ref[...] / l_ref[...]
        a = o - _lam_value(lam_ref, lam_init) * pltpu.roll(o, rows - nq, 0)
        y = _rms(a, sg_ref[...]) * (1.0 - lam_init)
        for h in range(nh):
            o_ref[:, h * hw:(h + 1) * hw] = y[h * 2 * nq:h * 2 * nq + nq].astype(o_ref.dtype)


def _attn_decode(qb, k_new, v_new, cache_k, cache_v, layer_idx, page_table, lam, subln_g,
                 nb, nq, nh, dh, lam_init):
    hw = 2 * dh
    assert 2 * nq == SUBLANES and hw == LANES and nq * nh <= LANES
    n_pages = page_table.shape[1]
    page = cache_k.shape[2]
    pages = min(DECODE_PAGES, n_pages)
    group = min(DECODE_GROUP, pages)
    assert n_pages % pages == 0 and pages % group == 0
    rows = nh * 2 * nq
    qh = jnp.tile(qb.reshape(nb, nq, nh, hw).transpose(0, 2, 1, 3), (1, 1, 2, 1)).reshape(nb, rows, hw)
    r = lax.broadcasted_iota(jnp.int32, (rows, hw), 0)
    c = lax.broadcasted_iota(jnp.int32, (rows, hw), 1)
    qbd = jnp.where(((r // nq) % 2) == (c // dh), qh, jnp.zeros_like(qh))
    pad_new = lambda a: jnp.pad(a.reshape(nb, nq * nh, hw), ((0, 0), (0, LANES - nq * nh), (0, 0)))
    view = lambda cache: cache.reshape(cache.shape[0], cache.shape[1], page * nh, hw)

    def cache_spec(slot):
        return pl.BlockSpec((None, None, page * nh, hw),
                            lambda b, p, pt: (layer_idx, pt[b, p * pages + slot], 0, 0))

    const = lambda shape: pl.BlockSpec(shape, lambda b, p, pt: (0,) * len(shape))
    per_b = lambda shape: pl.BlockSpec((None,) + shape, lambda b, p, pt: (b,) + (0,) * len(shape))
    grid_spec = pltpu.PrefetchScalarGridSpec(
        num_scalar_prefetch=1,
        grid=(nb, n_pages // pages),
        in_specs=[const(lam.shape), const((1, hw)), per_b((rows, hw)),
                  per_b((LANES, hw)), per_b((LANES, hw))]
                 + [cache_spec(i) for i in range(pages)] * 2,
        out_specs=per_b((nq, nh * hw)),
        scratch_shapes=[pltpu.VMEM((rows, 1), F32), pltpu.VMEM((rows, 1), F32),
                        pltpu.VMEM((rows, hw), F32)],
    )
    out = pl.pallas_call(
        functools.partial(_attn_decode_kernel, pages=pages, group=group, nh=nh, nq=nq, lam_init=lam_init),
        grid_spec=grid_spec,
        out_shape=jax.ShapeDtypeStruct((nb, nq, nh * hw), F32),
        compiler_params=_params("parallel", "arbitrary"),
        name="attn_decode",
    )(page_table, lam, subln_g, qbd, pad_new(k_new), pad_new(v_new),
      *([view(cache_k)] * pages), *([view(cache_v)] * pages))
    return out.reshape(nb * nq, nh * hw)


def _conv_taps(cx, prev1, prev2, cw_ref):
    return cw_ref[0:1, :] * prev2 + cw_ref[1:2, :] * prev1 + cw_ref[2:3, :] * cx


def _outproj_conv_kernel(gb_ref, cx_ref, at_ref, r_ref, w_ref, g_ref, cw_ref, *rest, seq, carry_mode):
    cx = cx_ref[...]
    tm, cwid = cx.shape
    row = lax.broadcasted_iota(jnp.int32, (tm, 1), 0)
    roll1 = pltpu.roll(cx, 1, 0)
    roll2 = pltpu.roll(cx, 2, 0)
    if carry_mode:
        st_ref, o_ref, carry = rest

        @pl.when((pl.program_id(0) * tm) % seq == 0)
        def _():
            carry[0:2, :] = st_ref[...]

        c0 = carry[0:1, :]
        c1 = carry[1:2, :]
        prev1 = jnp.where(row >= 1, roll1, c1)
        prev2 = jnp.where(row >= 2, roll2, jnp.where(row == 1, c1, c0))
        carry[0:2, :] = cx[tm - 2:tm, :]
    else:
        e1_ref, e2_ref, o_ref = rest
        t = row % seq
        prev1 = jnp.where(t >= 1, roll1, e1_ref[...])
        prev2 = jnp.where(t >= 2, roll2, e2_ref[...])
    yconv = (gb_ref[...] * _conv_taps(cx, prev1, prev2, cw_ref)).astype(BF16)
    mix = _dot(yconv, w_ref[0:cwid, :]) + _dot(at_ref[...].astype(BF16), w_ref[cwid:, :])
    o_ref[...] = r_ref[...] + _rms(mix, g_ref[...])


def _outproj_conv(gb, cx, attn, resid, w, g, conv_w, state, seq):
    m, d = resid.shape
    cwid = gb.shape[1]
    tm = _row_tile(m, ROW_TILE)
    carry_mode = seq % tm == 0
    row = lambda width: pl.BlockSpec((tm, width), lambda i: (i, 0))
    const = lambda shape: pl.BlockSpec(shape, lambda i: (0,) * len(shape))
    in_specs = [row(cwid), row(cwid), row(attn.shape[1]), row(d), const(w.shape), const((1, d)),
                const(conv_w.shape)]
    if carry_mode:
        extra = [state]
        in_specs += [pl.BlockSpec((None, 2, cwid), lambda i: ((i * tm) // seq, 0, 0))]
        scratch = [pltpu.VMEM((SUBLANES, cwid), F32)]
    else:
        assert tm % seq == 0 and seq >= 2
        nb = m // seq
        zeros = jnp.zeros((nb, seq - 1, cwid), F32)
        e1 = jnp.concatenate([state[:, 1:2], zeros], axis=1).reshape(m, cwid)
        e2 = jnp.concatenate([state[:, 0:1], state[:, 1:2], zeros[:, 1:]], axis=1).reshape(m, cwid)
        extra = [e1, e2]
        in_specs += [row(cwid), row(cwid)]
        scratch = []
    return pl.pallas_call(
        functools.partial(_outproj_conv_kernel, seq=seq, carry_mode=carry_mode),
        grid=(m // tm,),
        in_specs=in_specs,
        out_specs=row(d),
        out_shape=jax.ShapeDtypeStruct((m, d), F32),
        scratch_shapes=scratch,
        compiler_params=_params("arbitrary"),
        name="outproj_conv",
    )(gb, cx, attn, resid, w, g, conv_w, *extra)


def _outproj_kernel(x_ref, r_ref, w_ref, g_ref, o_ref):
    o_ref[...] = r_ref[...] + _rms(_dot(x_ref[...], w_ref[...]), g_ref[...])


def _outproj(x, resid, w, g):
    m, d = resid.shape
    tm = _row_tile(m, ROW_TILE)
    row = lambda width: pl.BlockSpec((tm, width), lambda i: (i, 0))
    return pl.pallas_call(
        _outproj_kernel,
        grid=(m // tm,),
        in_specs=[row(x.shape[1]), row(d), pl.BlockSpec(w.shape, lambda i: (0, 0)),
                  pl.BlockSpec((1, d), lambda i: (0, 0))],
        out_specs=row(d),
        out_shape=jax.ShapeDtypeStruct((m, d), F32),
        compiler_params=_params("parallel"),
        name="outproj",
    )(x, resid, w, g)


def _ffn_kernel(x_ref, g_in_ref, wg_ref, wu_ref, wd_ref, g_out_ref, o_ref, xn_ref, acc_ref):
    f = pl.program_id(1)

    @pl.when(f == 0)
    def _():
        xn_ref[...] = _rms(x_ref[...], g_in_ref[...]).astype(BF16)
        acc_ref[...] = jnp.zeros(acc_ref.shape, F32)

    xn = xn_ref[...]
    gate = _dot(xn, wg_ref[...])
    up = _dot(xn, wu_ref[...])
    act = (gate * jax.nn.sigmoid(gate) * up).astype(BF16)
    acc_ref[...] += _dot(act, wd_ref[...])

    @pl.when(f == pl.num_programs(1) - 1)
    def _():
        o_ref[...] = x_ref[...] + _rms(acc_ref[...], g_out_ref[...])


def _ffn(x, g_in, w_gu, w_down, g_out):
    m, d = x.shape
    dff = w_down.shape[0]
    tm = _row_tile(m, FFN_ROW_TILE)
    tf = _row_tile(dff, FFN_COL_TILE)
    nf = dff // tf
    row = pl.BlockSpec((tm, d), lambda i, f: (i, 0))
    vec = pl.BlockSpec((1, d), lambda i, f: (0, 0))
    return pl.pallas_call(
        _ffn_kernel,
        grid=(m // tm, nf),
        in_specs=[row, vec,
                  pl.BlockSpec((d, tf), lambda i, f: (0, f)),
                  pl.BlockSpec((d, tf), lambda i, f: (0, nf + f)),
                  pl.BlockSpec((tf, d), lambda i, f: (f, 0)),
                  vec],
        out_specs=row,
        out_shape=jax.ShapeDtypeStruct((m, d), F32),
        scratch_shapes=[pltpu.VMEM((tm, d), BF16), pltpu.VMEM((tm, d), F32)],
        compiler_params=_params("parallel", "arbitrary"),
        name="ffn",
    )(x, g_in, w_gu, w_gu, w_down, g_out)


def _inproj1_kernel(x_ref, g_ref, w_ref, wg_ref, b_ref, q_ref, k_ref, v_ref, o_ref, gt_ref,
                    *, d, nh, kscale):
    xn = _rms(x_ref[...], g_ref[...]).astype(BF16)
    q_ref[...] = _dot(xn, w_ref[:, 0:d]).astype(BF16)
    k_ref[...] = (_dot(xn, w_ref[:, d:2 * d]) * kscale).astype(BF16)
    v_ref[...] = _dot(xn, w_ref[:, 2 * d:3 * d]).astype(BF16)
    o_ref[...] = _dot(xn, w_ref[:, 3 * d:4 * d])
    gates = _dot(xn, wg_ref[...]) + b_ref[...]
    lane = lax.broadcasted_iota(jnp.int32, gates.shape, 1)
    gt_ref[...] = jnp.where(lane < nh, gates, jax.nn.log_sigmoid(gates))


def _inproj1(x, g, w, wg, b, nh, kscale):
    m, d = x.shape
    tm = _row_tile(m, ROW_TILE)
    row = lambda width: pl.BlockSpec((tm, width), lambda i: (i, 0))
    const = lambda shape: pl.BlockSpec(shape, lambda i: (0,) * len(shape))
    shapes = [(d, BF16), (d, BF16), (d, BF16), (d, F32), (LANES, F32)]
    return pl.pallas_call(
        functools.partial(_inproj1_kernel, d=d, nh=nh, kscale=kscale),
        grid=(m // tm,),
        in_specs=[row(d), const((1, d)), const(w.shape), const(wg.shape), const((1, LANES))],
        out_specs=[row(wd) for wd, _ in shapes],
        out_shape=[jax.ShapeDtypeStruct((m, wd), dt) for wd, dt in shapes],
        compiler_params=_params("parallel"),
        name="inproj1",
    )(x, g, w, wg, b)


def _mlstm_kernel(q_ref, k_ref, v_ref, o_ref, gt_ref, c0_ref, n0_ref, m0_ref, mhg_ref,
                  hn_ref, c_out, n_out, m_out, c_s, n_s, m_s, *, chunk, nchunks, nh):
    h = pl.program_id(1)
    si = pl.program_id(2)

    @pl.when(si == 0)
    def _():
        c_s[...] = c0_ref[...]
        n_s[...] = n0_ref[...]
        m_s[...] = m0_ref[...]

    L = chunk
    row = lax.broadcasted_iota(jnp.int32, (L, L), 0)
    col = lax.broadcasted_iota(jnp.int32, (L, L), 1)
    tri = col <= row
    eye = col == row
    lane = lax.broadcasted_iota(jnp.int32, (L, LANES), 1)

    for c in range(nchunks):
        sl = pl.ds(c * L, L)
        gt = gt_ref[sl, :]
        li_c = jnp.sum(jnp.where(lane == h, gt, 0.0), axis=1, keepdims=True)
        lf_c = jnp.sum(jnp.where(lane == nh + h, gt, 0.0), axis=1, keepdims=True)
        lf_r = jnp.sum(jnp.where(eye, lf_c, 0.0), axis=0, keepdims=True)
        li_r = jnp.sum(jnp.where(eye, li_c, 0.0), axis=0, keepdims=True)
        bc_c = jnp.sum(jnp.where(tri, lf_r, 0.0), axis=1, keepdims=True)
        bc_r = jnp.sum(jnp.where(row <= col, lf_c, 0.0), axis=0, keepdims=True)
        x_r = li_r - bc_r
        cm_c = jnp.max(jnp.where(tri, x_r, -jnp.inf), axis=1, keepdims=True)
        m_prev = m_s[0:1, 0:1]
        mt_c = bc_c + jnp.maximum(m_prev, cm_c)
        dmat = jnp.exp(jnp.where(tri, (bc_c - mt_c) + x_r, -jnp.inf))
        inter = jnp.exp(bc_c + m_prev - mt_c)

        q = q_ref[sl, :]
        k = k_ref[sl, :]
        v = v_ref[sl, :]
        sqk = _dot_nt(q, k) * dmat
        c_old = c_s[...]
        n_old = n_s[...]
        num = inter * _dot(q, c_old.astype(BF16)) + _dot(sqk.astype(BF16), v)
        den = (inter * jnp.sum(q.astype(F32) * n_old, axis=1, keepdims=True)
               + jnp.sum(sqk, axis=1, keepdims=True))
        h_til = num / jnp.maximum(jnp.abs(den), jnp.exp(-mt_c))

        m_end = mt_c[L - 1:L, :]
        bc_end = bc_c[L - 1:L, :]
        w_end = jnp.exp(bc_end - bc_c + li_c - m_end)
        decay = jnp.exp(bc_end + m_prev - m_end)
        wk = w_end * k.astype(F32)
        c_s[...] = decay * c_old + _dot_tn(wk.astype(BF16), v)
        n_s[...] = decay * n_old + jnp.sum(wk, axis=0, keepdims=True)
        m_s[...] = jnp.broadcast_to(m_end, m_s.shape)

        hcell = jax.nn.sigmoid(o_ref[sl, :]) * h_til
        hc = hcell - jnp.mean(hcell, axis=-1, keepdims=True)
        hn = hc * lax.rsqrt(jnp.mean(hc * hc, axis=-1, keepdims=True) + EPS) * mhg_ref[...]
        hn_ref[sl, :] = hn.astype(hn_ref.dtype)

    @pl.when(si == pl.num_programs(2) - 1)
    def _():
        c_out[...] = c_s[...]
        n_out[...] = n_s[...]
        m_out[...] = m_s[...]


def _mlstm(q, k, v, o, gt, c0, n0, m0, mh_g, nh, chunk, rows_per_step):
    b, s, d = q.shape
    dh = d // nh
    assert rows_per_step % chunk == 0 and s % rows_per_step == 0
    blk = lambda width: pl.BlockSpec((None, rows_per_step, width), lambda bi, h, si: (bi, si, h))
    st = lambda r, w: pl.BlockSpec((None, None, r, w), lambda bi, h, si: (bi, h, 0, 0))
    return pl.pallas_call(
        functools.partial(_mlstm_kernel, chunk=chunk, nchunks=rows_per_step // chunk, nh=nh),
        grid=(b, nh, s // rows_per_step),
        in_specs=[blk(dh), blk(dh), blk(dh), blk(dh),
                  pl.BlockSpec((None, rows_per_step, LANES), lambda bi, h, si: (bi, si, 0)),
                  st(dh, dh), st(1, dh), st(SUBLANES, LANES),
                  pl.BlockSpec((1, dh), lambda bi, h, si: (0, h))],
        out_specs=[blk(dh), st(dh, dh), st(1, dh), st(SUBLANES, LANES)],
        out_shape=[jax.ShapeDtypeStruct((b, s, d), BF16),
                   jax.ShapeDtypeStruct((b, nh, dh, dh), F32),
                   jax.ShapeDtypeStruct((b, nh, 1, dh), F32),
                   jax.ShapeDtypeStruct((b, nh, SUBLANES, LANES), F32)],
        scratch_shapes=[pltpu.VMEM((dh, dh), F32), pltpu.VMEM((1, dh), F32),
                        pltpu.VMEM((SUBLANES, LANES), F32)],
        compiler_params=_params("parallel", "parallel", "arbitrary"),
        name="mlstm",
    )(q, k, v, o, gt, c0, n0, m0, mh_g)


def _trunk(x, state_conv, cache_k, cache_v, page_table, state_c, state_n, state_m, wts):
    b, s, d = x.shape
    m = b * s
    depth = wts["norms"].shape[0]
    cw = wts["conv_w0"].shape[2]
    dh = wts["lam0"].shape[2]
    aw = (wts["w_in0"].shape[2] - 3 * cw) // 3
    nh_a = aw // (2 * dh)
    nh_m = state_c.shape[2]
    dh_m = d // nh_m
    h = x.reshape(m, d)
    ks, vs, convs, cs, ns, ms = [], [], [], [], [], []
    for layer in range(depth):
        j = layer // 2
        g = wts["norms"][layer]
        if layer % 2 == 0:
            lam_init = _lambda_init(layer)
            prompt = cache_k is None
            proj = _inproj0(h, g[0:1], wts["w_in0"][j], wts["w_in0_qv_t"][j], cw, aw, dh ** -0.5, prompt)
            gb, cx, k, v = proj[:4]
            sg = wts["subln_g0"][j].reshape(1, 2 * dh)
            if prompt:
                kb, qt, vt = proj[4:]
                attn = _attn_prompt(qt, kb.reshape(b, s, aw), vt, wts["lam0"][j], sg, b, s, nh_a, dh,
                                    lam_init).reshape(m, aw)
            else:
                attn = _attn_decode(proj[4], k, v, cache_k, cache_v, j, page_table, wts["lam0"][j], sg,
                                    b, s, nh_a, dh, lam_init)
            h = _outproj_conv(gb, cx, attn, h, wts["w_out0"][j], g[1:2], wts["conv_w0"][j],
                              state_conv[j], s)
            ks.append(k.reshape(b, s, nh_a, 2 * dh))
            vs.append(v.reshape(b, s, nh_a, 2 * dh))
            convs.append(cx.reshape(b, s, cw)[:, s - 2:, :])
        else:
            q, k, v, o, gt = _inproj1(h, g[0:1], wts["w_in1"][j], wts["w_in1_gates"][j], wts["b_if1"][j],
                                      nh_m, dh_m ** -0.5)
            chunk = MLSTM_CHUNK if s % MLSTM_CHUNK == 0 else s
            sp = s
            r3 = lambda a: a.reshape(b, s, a.shape[1])
            q, k, v, o, gt = r3(q), r3(k), r3(v), r3(o), r3(gt)
            if chunk % SUBLANES != 0:
                sp = -(-s // SUBLANES) * SUBLANES
                chunk = sp
                pad = lambda a: jnp.pad(a, ((0, 0), (0, sp - s), (0, 0)))
                lane = lax.broadcasted_iota(jnp.int32, (b, sp - s, LANES), 2)
                gt_pad = jnp.where(lane < nh_m, -jnp.inf, 0.0).astype(F32)
                q, k, v, o = pad(q), pad(k), pad(v), pad(o)
                gt = jnp.concatenate([gt, gt_pad], axis=1)
            rows = min(MLSTM_ROWS, sp)
            m0 = jnp.broadcast_to(state_m[j][:, :, None, None], (b, nh_m, SUBLANES, LANES))
            hn, c_new, n_new, m_new = _mlstm(q, k, v, o, gt, state_c[j], state_n[j][:, :, None, :], m0,
                                             wts["mh_g1"][j].reshape(1, d), nh_m, chunk, rows)
            hn = hn[:, :s].reshape(m, d)
            h = _outproj(hn, h, wts["w_out1"][j], g[1:2])
            cs.append(c_new)
            ns.append(n_new[:, :, 0, :])
            ms.append(m_new[:, :, 0, 0])
        h = _ffn(h, g[2:3], wts["w_gu"][layer], wts["w_down"][layer], g[3:4])
    return (h.reshape(b, s, d), jnp.stack(ks), jnp.stack(vs), jnp.stack(convs),
            jnp.stack(cs), jnp.stack(ns), jnp.stack(ms))


def _qv_transposed(w_in0, cw):
    aw = (w_in0.shape[2] - 3 * cw) // 3
    wq = w_in0[:, :, 3 * cw:3 * cw + aw]
    wv = w_in0[:, :, 3 * cw + 2 * aw:]
    return jnp.swapaxes(jnp.concatenate([wq, wv], axis=2), 1, 2).astype(BF16)


def kernel(x_prompt, x_sample, cache_k, cache_v, state_conv, state_C, state_n, state_m, page_table,
           norms, w_in0, conv_w0, lam0, subln_g0, w_out0, w_in1, b_if1, mh_g1, w_out1, w_gu, w_down):
    d = x_prompt.shape[-1]
    nh_m = state_C.shape[2]
    n_odd = w_in1.shape[0]
    w_gates = jnp.pad(w_in1[:, :, 4 * d:], ((0, 0), (0, 0), (0, LANES - 2 * nh_m))).astype(BF16)
    b_pad = jnp.pad(b_if1.astype(F32), ((0, 0), (0, LANES - 2 * nh_m))).reshape(n_odd, 1, LANES)
    wts = dict(
        norms=norms.astype(F32), w_in0=w_in0.astype(BF16), conv_w0=conv_w0, lam0=lam0.astype(F32),
        w_in0_qv_t=_qv_transposed(w_in0, conv_w0.shape[2]),
        subln_g0=subln_g0.astype(F32), w_out0=w_out0.astype(BF16), w_in1=w_in1[:, :, :4 * d].astype(BF16),
        w_in1_gates=w_gates, b_if1=b_pad, mh_g1=mh_g1.astype(F32), w_out1=w_out1.astype(BF16),
        w_gu=w_gu.astype(BF16), w_down=w_down.astype(BF16))
    bp = x_prompt.shape[0]
    n_even = state_conv.shape[0]
    conv0 = jnp.zeros((n_even, bp) + state_conv.shape[2:], x_prompt.dtype)
    c0 = jnp.zeros((n_odd, bp) + state_C.shape[2:], F32)
    n0 = jnp.zeros((n_odd, bp) + state_n.shape[2:], F32)
    m0 = jnp.zeros((n_odd, bp) + state_m.shape[2:], F32)
    y_p, k_p, v_p, conv_p, c_p, n_p, m_p = _trunk(x_prompt, conv0, None, None, None, c0, n0, m0, wts)
    y_s, k_s, v_s, conv_s, c_s, n_s, m_s = _trunk(x_sample, state_conv, cache_k, cache_v, page_table,
                                                  state_C.astype(F32), state_n.astype(F32),
                                                  state_m.astype(F32), wts)
    return (y_p, y_s, k_p, v_p, conv_p, c_p, n_p, m_p, k_s, v_s, conv_s, c_s, n_s, m_s)
```

```python
import functools
import math

import jax
import jax.numpy as jnp
from jax import lax
from jax.experimental import pallas as pl
from jax.experimental.pallas import tpu as pltpu

EPS = 1e-6
LOG2_E = math.log2(math.e)
F32 = jnp.float32
BF16 = jnp.bfloat16

V7X_VMEM_BYTES = 64 * 1024 * 1024
VMEM_LIMIT_BYTES = V7X_VMEM_BYTES - 8 * 1024 * 1024
LANES = 128
SUBLANES = 8

ROW_TILE = 512
FFN_ROW_TILE = 1024
FFN_COL_TILE = 256
ATTN_Q_TILE = 256
ATTN_K_TILE = 1024
DECODE_PAGES = 16
DECODE_GROUP = 16
MLSTM_CHUNK = 256
MLSTM_ROWS = 256
MLSTM_SEQS = 2


def _params(*semantics):
    return pltpu.CompilerParams(dimension_semantics=semantics, vmem_limit_bytes=VMEM_LIMIT_BYTES)


def _dot(a, b):
    return jnp.dot(a, b, preferred_element_type=F32)


def _dot_nt(a, b):
    return lax.dot_general(a, b, (((1,), (1,)), ((), ())), preferred_element_type=F32)


def _dot_tn(a, b):
    return lax.dot_general(a, b, (((0,), (0,)), ((), ())), preferred_element_type=F32)


def _rms(x, g):
    return x * lax.rsqrt(jnp.mean(x * x, axis=-1, keepdims=True) + EPS) * g


def _lambda_init(layer):
    return 0.8 - 0.6 * math.exp(-0.3 * layer)


def _lam_value(lam_ref, lam_init):
    lv = lam_ref[...]
    a = jnp.sum(lv[0:1] * lv[1:2], axis=-1, keepdims=True)
    b = jnp.sum(lv[2:3] * lv[3:4], axis=-1, keepdims=True)
    return jnp.exp(a) - jnp.exp(b) + lam_init


def _row_tile(m, want):
    t = min(want, m)
    assert m % t == 0, (m, t)
    return t


def _inproj0_kernel(x_ref, g_ref, w_ref, *rest, cw, aw, qscale, transposed):
    xn = _rms(x_ref[...], g_ref[...]).astype(BF16)

    def proj(lo, width):
        return _dot(xn, w_ref[:, lo:lo + width])

    if transposed:
        wt_ref, gb_ref, cx_ref, k_ref, v_ref, kb_ref, qt_ref, vt_ref = rest
        qt_ref[...] = (_dot_nt(wt_ref[0:aw, :], xn) * qscale).astype(BF16)
        vt_ref[...] = _dot_nt(wt_ref[aw:2 * aw, :], xn).astype(BF16)
    else:
        gb_ref, cx_ref, k_ref, v_ref, q_ref = rest
        q_ref[...] = (proj(3 * cw, aw) * qscale).astype(BF16)
    gb_ref[...] = proj(0, cw)
    cx_ref[...] = proj(cw, cw) * proj(2 * cw, cw)
    k = proj(3 * cw + aw, aw)
    v = proj(3 * cw + 2 * aw, aw)
    tm = k.shape[0]
    nh = aw // LANES
    for h in range(nh):
        k_ref[pl.ds(h, tm, stride=nh), :] = k[:, h * LANES:(h + 1) * LANES]
        v_ref[pl.ds(h, tm, stride=nh), :] = v[:, h * LANES:(h + 1) * LANES]
    if transposed:
        kb_ref[...] = k.astype(BF16)


def _inproj0(x, g, w, wt, cw, aw, qscale, transposed):
    m, d = x.shape
    tm = _row_tile(m, ROW_TILE)
    n = w.shape[1]
    row = lambda width: pl.BlockSpec((tm, width), lambda i: (i, 0))
    const = lambda shape: pl.BlockSpec(shape, lambda i: (0,) * len(shape))
    nh = aw // LANES
    kv_spec = pl.BlockSpec((tm * nh, LANES), lambda i: (i, 0))
    kv_shape = jax.ShapeDtypeStruct((m * nh, LANES), F32)
    out_specs = [row(cw), row(cw), kv_spec, kv_spec, row(aw)]
    out_shape = [jax.ShapeDtypeStruct((m, cw), F32), jax.ShapeDtypeStruct((m, cw), F32), kv_shape, kv_shape,
                 jax.ShapeDtypeStruct((m, aw), BF16)]
    in_specs = [row(d), const((1, d)), const((d, n))]
    args = [x, g, w]
    if transposed:
        in_specs.append(const(wt.shape))
        args.append(wt)
        out_specs += [pl.BlockSpec((aw, tm), lambda i: (0, i))] * 2
        out_shape += [jax.ShapeDtypeStruct((aw, m), BF16)] * 2
    return pl.pallas_call(
        functools.partial(_inproj0_kernel, cw=cw, aw=aw, qscale=qscale, transposed=transposed),
        grid=(m // tm,),
        in_specs=in_specs,
        out_specs=out_specs,
        out_shape=out_shape,
        compiler_params=_params("parallel"),
        name="inproj0",
    )(*args)


def _attn_prompt_kernel(lam_ref, sg_ref, qt_ref, k_ref, vt_ref, o_ref, *, tq, tk, dh, lam_init):
    nch = tk // tq
    g = pl.program_id(2)
    qt = qt_ref[...]
    sub = lax.broadcasted_iota(jnp.int32, (2 * dh, tq), 0)
    zero = jnp.zeros((2 * dh, tq), qt.dtype)
    qqs = []
    for c in range(nch):
        qc = qt[:, c * tq:(c + 1) * tq]
        qqs.append(jnp.concatenate([jnp.where(sub < dh, qc, zero), jnp.where(sub >= dh, qc, zero)], axis=1))
    cols = 2 * tq

    def step(j, carry, masked):
        off = pl.multiple_of(j * tk, tk)
        k = k_ref[pl.ds(off, tk), :]
        vt = vt_ref[:, pl.ds(off, tk)]
        nks = [(c + 1) * tq if masked else tk for c in range(nch)]
        scores = [_dot(k[:nks[c]], qqs[c]) for c in range(nch)]
        probs, stats = [], []
        for c in range(nch):
            m, l, _ = carry[c]
            s = scores[c]
            if masked:
                r = lax.broadcasted_iota(jnp.int32, (nks[c], cols), 0)
                cc = lax.broadcasted_iota(jnp.int32, (nks[c], cols), 1)
                s = jnp.where(r <= c * tq + jnp.where(cc >= tq, cc - tq, cc), s, -jnp.inf)
            m_new = jnp.maximum(m, jnp.max(s, axis=0, keepdims=True))
            alpha = jnp.exp2(m - m_new)
            p = jnp.exp2(s - m_new)
            stats.append((m_new, alpha, alpha * l + jnp.sum(p, axis=0, keepdims=True)))
            probs.append(p.astype(BF16))
        out = []
        for c in range(nch):
            m_new, alpha, l = stats[c]
            acc = alpha * carry[c][2] + _dot(vt[:, :nks[c]], probs[c])
            out.append((m_new, l, acc))
        return tuple(out)

    init = tuple((jnp.full((1, cols), -jnp.inf, F32), jnp.zeros((1, cols), F32),
                  jnp.zeros((2 * dh, cols), F32)) for _ in range(nch))
    carry = lax.fori_loop(0, g, lambda j, c: step(j, c, False), init)
    carry = step(g, carry, True)

    lam = _lam_value(lam_ref, lam_init)
    for c in range(nch):
        _, l, acc = carry[c]
        o = acc / l
        a = o[:, :tq] - lam * o[:, tq:]
        y = a * lax.rsqrt(jnp.mean(a * a, axis=0, keepdims=True) + EPS) * sg_ref[...] * (1.0 - lam_init)
        o_ref[c * tq:(c + 1) * tq, :] = jnp.transpose(y).astype(o_ref.dtype)


def _attn_prompt(qt, kb, vt, lam, subln_g, b, s, nh, dh, lam_init):
    tk = _row_tile(s, ATTN_K_TILE)
    tq = _row_tile(tk, ATTN_Q_TILE)
    hw = 2 * dh
    assert hw == LANES
    nq = s // tk
    return pl.pallas_call(
        functools.partial(_attn_prompt_kernel, tq=tq, tk=tk, dh=dh, lam_init=lam_init),
        grid=(b, nh, nq),
        in_specs=[pl.BlockSpec(lam.shape, lambda bi, h, qi: (0, 0)),
                  pl.BlockSpec((hw, 1), lambda bi, h, qi: (0, 0)),
                  pl.BlockSpec((hw, tk), lambda bi, h, qi: (h, bi * nq + qi)),
                  pl.BlockSpec((None, s, hw), lambda bi, h, qi: (bi, 0, h)),
                  pl.BlockSpec((hw, s), lambda bi, h, qi: (h, bi))],
        out_specs=pl.BlockSpec((None, tk, hw), lambda bi, h, qi: (bi, qi, h)),
        out_shape=jax.ShapeDtypeStruct((b, s, nh * hw), BF16),
        compiler_params=_params("parallel", "parallel", "arbitrary"),
        name="attn_prompt",
    )(lam, subln_g.reshape(hw, 1), qt, kb, vt)


def _attn_decode_kernel(pt_ref, lam_ref, sg_ref, q_ref, kn_ref, vn_ref, *refs,
                        pages, group, nh, nq, lam_init):
    del pt_ref
    k_refs = refs[:pages]
    v_refs = refs[pages:2 * pages]
    o_ref = refs[2 * pages]
    m_ref, l_ref, acc_ref = refs[2 * pages + 1:]
    step = pl.program_id(1)
    q = q_ref[...]
    rows = q.shape[0]
    hw = q.shape[1]
    prow = k_refs[0].shape[0]

    def masks(width):
        r = lax.broadcasted_iota(jnp.int32, (rows, width), 0)
        c = lax.broadcasted_iota(jnp.int32, (rows, width), 1)
        return r, c, (c % nh) == (r // (2 * nq))

    def update(scores, values):
        m_old = m_ref[...]
        m_new = m_old
        for s in scores:
            m_new = jnp.maximum(m_new, jnp.max(s, axis=-1, keepdims=True))
        alpha = jnp.exp2(m_old - m_new)
        l = alpha * l_ref[...]
        acc = alpha * acc_ref[...]
        for s, v in zip(scores, values):
            p = jnp.exp2(s - m_new)
            l = l + jnp.sum(p, axis=-1, keepdims=True)
            acc = acc + _dot(p.astype(BF16), v)
        m_ref[...] = m_new
        l_ref[...] = l
        acc_ref[...] = acc

    @pl.when(step == 0)
    def _():
        m_ref[...] = jnp.full(m_ref.shape, -jnp.inf, F32)
        l_ref[...] = jnp.zeros(l_ref.shape, F32)
        acc_ref[...] = jnp.zeros(acc_ref.shape, F32)

    _, _, head_ok = masks(prow)
    for g0 in range(0, pages, group):
        scores = [jnp.where(head_ok, _dot_nt(q, k_refs[r][...].astype(BF16)), -jnp.inf)
                  for r in range(g0, g0 + group)]
        update(scores, [v_refs[r][...].astype(BF16) for r in range(g0, g0 + group)])

    @pl.when(step == pl.num_programs(1) - 1)
    def _():
        width = kn_ref.shape[0]
        r, c, ok = masks(width)
        ok = ok & ((c // nh) <= (r % nq)) & (c < nq * nh)
        s = jnp.where(ok, _dot_nt(q, kn_ref[...].astype(BF16)), -jnp.inf)
        update([s], [vn_ref[...].astype(BF16)])
        o = acc_ref[...] / l_ref[...]
        a = o - _lam_value(lam_ref, lam_init) * pltpu.roll(o, rows - nq, 0)
        y = _rms(a, sg_ref[...]) * (1.0 - lam_init)
        for h in range(nh):
            o_ref[:, h * hw:(h + 1) * hw] = y[h * 2 * nq:h * 2 * nq + nq].astype(o_ref.dtype)


def _attn_decode(qb, k_new, v_new, cache_k, cache_v, layer_idx, page_table, lam, subln_g,
                 nb, nq, nh, dh, lam_init):
    hw = 2 * dh
    assert 2 * nq == SUBLANES and hw == LANES and nq * nh <= LANES
    n_pages = page_table.shape[1]
    page = cache_k.shape[2]
    pages = min(DECODE_PAGES, n_pages)
    group = min(DECODE_GROUP, pages)
    assert n_pages % pages == 0 and pages % group == 0
    rows = nh * 2 * nq
    qh = jnp.tile(qb.reshape(nb, nq, nh, hw).transpose(0, 2, 1, 3), (1, 1, 2, 1)).reshape(nb, rows, hw)
    r = lax.broadcasted_iota(jnp.int32, (rows, hw), 0)
    c = lax.broadcasted_iota(jnp.int32, (rows, hw), 1)
    qbd = jnp.where(((r // nq) % 2) == (c // dh), qh, jnp.zeros_like(qh))
    pad_new = lambda a: jnp.pad(a.reshape(nb, nq * nh, hw), ((0, 0), (0, LANES - nq * nh), (0, 0)))
    view = lambda cache: cache.reshape(cache.shape[0], cache.shape[1], page * nh, hw)

    def cache_spec(slot):
        return pl.BlockSpec((None, None, page * nh, hw),
                            lambda b, p, pt: (layer_idx, pt[b, p * pages + slot], 0, 0))

    const = lambda shape: pl.BlockSpec(shape, lambda b, p, pt: (0,) * len(shape))
    per_b = lambda shape: pl.BlockSpec((None,) + shape, lambda b, p, pt: (b,) + (0,) * len(shape))
    grid_spec = pltpu.PrefetchScalarGridSpec(
        num_scalar_prefetch=1,
        grid=(nb, n_pages // pages),
        in_specs=[const(lam.shape), const((1, hw)), per_b((rows, hw)),
                  per_b((LANES, hw)), per_b((LANES, hw))]
                 + [cache_spec(i) for i in range(pages)] * 2,
        out_specs=per_b((nq, nh * hw)),
        scratch_shapes=[pltpu.VMEM((rows, 1), F32), pltpu.VMEM((rows, 1), F32),
                        pltpu.VMEM((rows, hw), F32)],
    )
    out = pl.pallas_call(
        functools.partial(_attn_decode_kernel, pages=pages, group=group, nh=nh, nq=nq, lam_init=lam_init),
        grid_spec=grid_spec,
        out_shape=jax.ShapeDtypeStruct((nb, nq, nh * hw), F32),
        compiler_params=_params("parallel", "arbitrary"),
        name="attn_decode",
    )(page_table, lam, subln_g, qbd, pad_new(k_new), pad_new(v_new),
      *([view(cache_k)] * pages), *([view(cache_v)] * pages))
    return out.reshape(nb * nq, nh * hw)


def _conv_taps(cx, prev1, prev2, cw_ref):
    return cw_ref[0:1, :] * prev2 + cw_ref[1:2, :] * prev1 + cw_ref[2:3, :] * cx


def _outproj_conv_kernel(gb_ref, cx_ref, at_ref, r_ref, w_ref, g_ref, cw_ref, *rest, seq, carry_mode):
    cx = cx_ref[...]
    tm, cwid = cx.shape
    row = lax.broadcasted_iota(jnp.int32, (tm, 1), 0)
    roll1 = pltpu.roll(cx, 1, 0)
    roll2 = pltpu.roll(cx, 2, 0)
    if carry_mode:
        st_ref, o_ref, carry = rest

        @pl.when((pl.program_id(0) * tm) % seq == 0)
        def _():
            carry[0:2, :] = st_ref[...]

        c0 = carry[0:1, :]
        c1 = carry[1:2, :]
        prev1 = jnp.where(row >= 1, roll1, c1)
        prev2 = jnp.where(row >= 2, roll2, jnp.where(row == 1, c1, c0))
        carry[0:2, :] = cx[tm - 2:tm, :]
    else:
        e1_ref, e2_ref, o_ref = rest
        t = row % seq
        prev1 = jnp.where(t >= 1, roll1, e1_ref[...])
        prev2 = jnp.where(t >= 2, roll2, e2_ref[...])
    yconv = (gb_ref[...] * _conv_taps(cx, prev1, prev2, cw_ref)).astype(BF16)
    mix = _dot(yconv, w_ref[0:cwid, :]) + _dot(at_ref[...].astype(BF16), w_ref[cwid:, :])
    o_ref[...] = r_ref[...] + _rms(mix, g_ref[...])


def _outproj_conv(gb, cx, attn, resid, w, g, conv_w, state, seq):
    m, d = resid.shape
    cwid = gb.shape[1]
    tm = _row_tile(m, ROW_TILE)
    carry_mode = seq % tm == 0
    row = lambda width: pl.BlockSpec((tm, width), lambda i: (i, 0))
    const = lambda shape: pl.BlockSpec(shape, lambda i: (0,) * len(shape))
    in_specs = [row(cwid), row(cwid), row(attn.shape[1]), row(d), const(w.shape), const((1, d)),
                const(conv_w.shape)]
    if carry_mode:
        extra = [state]
        in_specs += [pl.BlockSpec((None, 2, cwid), lambda i: ((i * tm) // seq, 0, 0))]
        scratch = [pltpu.VMEM((SUBLANES, cwid), F32)]
    else:
        assert tm % seq == 0 and seq >= 2
        nb = m // seq
        zeros = jnp.zeros((nb, seq - 1, cwid), F32)
        e1 = jnp.concatenate([state[:, 1:2], zeros], axis=1).reshape(m, cwid)
        e2 = jnp.concatenate([state[:, 0:1], state[:, 1:2], zeros[:, 1:]], axis=1).reshape(m, cwid)
        extra = [e1, e2]
        in_specs += [row(cwid), row(cwid)]
        scratch = []
    return pl.pallas_call(
        functools.partial(_outproj_conv_kernel, seq=seq, carry_mode=carry_mode),
        grid=(m // tm,),
        in_specs=in_specs,
        out_specs=row(d),
        out_shape=jax.ShapeDtypeStruct((m, d), F32),
        scratch_shapes=scratch,
        compiler_params=_params("arbitrary"),
        name="outproj_conv",
    )(gb, cx, attn, resid, w, g, conv_w, *extra)


def _outproj_kernel(x_ref, r_ref, w_ref, g_ref, o_ref):
    o_ref[...] = r_ref[...] + _rms(_dot(x_ref[...], w_ref[...]), g_ref[...])


def _outproj(x, resid, w, g):
    m, d = resid.shape
    tm = _row_tile(m, ROW_TILE)
    row = lambda width: pl.BlockSpec((tm, width), lambda i: (i, 0))
    return pl.pallas_call(
        _outproj_kernel,
        grid=(m // tm,),
        in_specs=[row(x.shape[1]), row(d), pl.BlockSpec(w.shape, lambda i: (0, 0)),
                  pl.BlockSpec((1, d), lambda i: (0, 0))],
        out_specs=row(d),
        out_shape=jax.ShapeDtypeStruct((m, d), F32),
        compiler_params=_params("parallel"),
        name="outproj",
    )(x, resid, w, g)


def _ffn_kernel(x_ref, g_in_ref, wg_ref, wu_ref, wd_ref, g_out_ref, o_ref, xn_ref, acc_ref):
    f = pl.program_id(1)

    @pl.when(f == 0)
    def _():
        xn_ref[...] = _rms(x_ref[...], g_in_ref[...]).astype(BF16)
        acc_ref[...] = jnp.zeros(acc_ref.shape, F32)

    xn = xn_ref[...]
    gate = _dot(xn, wg_ref[...])
    up = _dot(xn, wu_ref[...])
    act = (gate * jax.nn.sigmoid(gate) * up).astype(BF16)
    acc_ref[...] += _dot(act, wd_ref[...])

    @pl.when(f == pl.num_programs(1) - 1)
    def _():
        o_ref[...] = x_ref[...] + _rms(acc_ref[...], g_out_ref[...])


def _ffn(x, g_in, w_gu, w_down, g_out):
    m, d = x.shape
    dff = w_down.shape[0]
    tm = _row_tile(m, FFN_ROW_TILE)
    tf = _row_tile(dff, FFN_COL_TILE)
    nf = dff // tf
    row = pl.BlockSpec((tm, d), lambda i, f: (i, 0))
    vec = pl.BlockSpec((1, d), lambda i, f: (0, 0))
    return pl.pallas_call(
        _ffn_kernel,
        grid=(m // tm, nf),
        in_specs=[row, vec,
                  pl.BlockSpec((d, tf), lambda i, f: (0, f)),
                  pl.BlockSpec((d, tf), lambda i, f: (0, nf + f)),
                  pl.BlockSpec((tf, d), lambda i, f: (f, 0)),
                  vec],
        out_specs=row,
        out_shape=jax.ShapeDtypeStruct((m, d), F32),
        scratch_shapes=[pltpu.VMEM((tm, d), BF16), pltpu.VMEM((tm, d), F32)],
        compiler_params=_params("parallel", "arbitrary"),
        name="ffn",
    )(x, g_in, w_gu, w_gu, w_down, g_out)


def _inproj1_kernel(x_ref, g_ref, w_ref, wg_ref, b_ref, q_ref, k_ref, v_ref, o_ref, gt_ref,
                    *, d, nh, kscale):
    xn = _rms(x_ref[...], g_ref[...]).astype(BF16)
    q_ref[...] = _dot(xn, w_ref[:, 0:d]).astype(BF16)
    k_ref[...] = (_dot(xn, w_ref[:, d:2 * d]) * kscale).astype(BF16)
    v_ref[...] = _dot(xn, w_ref[:, 2 * d:3 * d]).astype(BF16)
    o_ref[...] = _dot(xn, w_ref[:, 3 * d:4 * d])
    gates = _dot(xn, wg_ref[...]) + b_ref[...]
    lane = lax.broadcasted_iota(jnp.int32, gates.shape, 1)
    gt_ref[...] = jnp.where(lane < nh, gates, jax.nn.log_sigmoid(gates))


def _inproj1(x, g, w, wg, b, nh, kscale):
    m, d = x.shape
    tm = _row_tile(m, ROW_TILE)
    row = lambda width: pl.BlockSpec((tm, width), lambda i: (i, 0))
    const = lambda shape: pl.BlockSpec(shape, lambda i: (0,) * len(shape))
    shapes = [(d, BF16), (d, BF16), (d, BF16), (d, F32), (LANES, F32)]
    return pl.pallas_call(
        functools.partial(_inproj1_kernel, d=d, nh=nh, kscale=kscale),
        grid=(m // tm,),
        in_specs=[row(d), const((1, d)), const(w.shape), const(wg.shape), const((1, LANES))],
        out_specs=[row(wd) for wd, _ in shapes],
        out_shape=[jax.ShapeDtypeStruct((m, wd), dt) for wd, dt in shapes],
        compiler_params=_params("parallel"),
        name="inproj1",
    )(x, g, w, wg, b)


def _mlstm_kernel(q_ref, k_ref, v_ref, o_ref, gt_ref, c0_ref, n0_ref, m0_ref, mhg_ref,
                  hn_ref, c_out, n_out, m_out, *, chunk, nchunks, nh, single_step):
    gb = q_ref.shape[0]
    dh = q_ref.shape[2] // nh
    chains = [(bi, h) for bi in range(gb) for h in range(nh)]

    if not single_step:
        @pl.when(pl.program_id(1) == 0)
        def _():
            c_out[...] = c0_ref[...]
            n_out[...] = n0_ref[...]
            m_out[...] = m0_ref[...]

    L = chunk
    row = lax.broadcasted_iota(jnp.int32, (L, L), 0)
    col = lax.broadcasted_iota(jnp.int32, (L, L), 1)
    tri = col <= row
    eye = col == row

    for c in range(nchunks):
        sl = pl.ds(c * L, L)
        state_in = (c0_ref, n0_ref, m0_ref) if (single_step and c == 0) else (c_out, n_out, m_out)
        part = []
        for bi, h in chains:
            lanes = slice(h * dh, (h + 1) * dh)
            gt = gt_ref[bi, sl, :]
            li_c = gt[:, h:h + 1]
            lf_c = gt[:, nh + h:nh + h + 1]
            lf_r = jnp.sum(jnp.where(eye, lf_c, 0.0), axis=0, keepdims=True)
            li_r = jnp.sum(jnp.where(eye, li_c, 0.0), axis=0, keepdims=True)
            bc_c = jnp.sum(jnp.where(tri, lf_r, 0.0), axis=1, keepdims=True)
            bc_r = jnp.sum(jnp.where(row <= col, lf_c, 0.0), axis=0, keepdims=True)
            x_r = li_r - bc_r
            cm_c = jnp.max(jnp.where(tri, x_r, -jnp.inf), axis=1, keepdims=True)
            m_prev = state_in[2][bi, h, 0:1, 0:1]
            mt_c = bc_c + jnp.maximum(m_prev, cm_c)
            dmat = jnp.exp(jnp.where(tri, (bc_c - mt_c) + x_r, -jnp.inf))
            inter = jnp.exp(bc_c + m_prev - mt_c)
            q = q_ref[bi, sl, lanes]
            k = k_ref[bi, sl, lanes]
            c_old = state_in[0][bi, h]
            part.append(dict(lanes=lanes, li_c=li_c, bc_c=bc_c, mt_c=mt_c, m_prev=m_prev, dmat=dmat,
                             inter=inter, q=q, k=k, c_old=c_old, n_old=state_in[1][bi, h],
                             qk=_dot_nt(q, k), qc=_dot(q, c_old.astype(BF16))))

        for (bi, h), p in zip(chains, part):
            v = v_ref[bi, sl, p["lanes"]]
            sqk = p["qk"] * p["dmat"]
            num = p["inter"] * p["qc"] + _dot(sqk.astype(BF16), v)
            den = (p["inter"] * jnp.sum(p["q"].astype(F32) * p["n_old"], axis=1, keepdims=True)
                   + jnp.sum(sqk, axis=1, keepdims=True))
            h_til = num / jnp.maximum(jnp.abs(den), jnp.exp(-p["mt_c"]))
            hcell = jax.nn.sigmoid(o_ref[bi, sl, p["lanes"]]) * h_til
            hc = hcell - jnp.mean(hcell, axis=-1, keepdims=True)
            hn = hc * lax.rsqrt(jnp.mean(hc * hc, axis=-1, keepdims=True) + EPS) * mhg_ref[:, p["lanes"]]
            hn_ref[bi, sl, p["lanes"]] = hn.astype(hn_ref.dtype)

        for (bi, h), p in zip(chains, part):
            v = v_ref[bi, sl, p["lanes"]]
            m_end = p["mt_c"][L - 1:L, :]
            bc_end = p["bc_c"][L - 1:L, :]
            w_end = jnp.exp(bc_end - p["bc_c"] + p["li_c"] - m_end)
            decay = jnp.exp(bc_end + p["m_prev"] - m_end)
            wk = w_end * p["k"].astype(F32)
            c_out[bi, h] = decay * p["c_old"] + _dot_tn(wk.astype(BF16), v)
            n_out[bi, h] = decay * p["n_old"] + jnp.sum(wk, axis=0, keepdims=True)
            m_out[bi, h] = jnp.broadcast_to(m_end, m_out.shape[2:])


def _mlstm(q, k, v, o, gt, c0, n0, m0, mh_g, nh, chunk, rows_per_step):
    b, s, d = q.shape
    dh = d // nh
    gb = min(MLSTM_SEQS, b)
    assert rows_per_step % chunk == 0 and s % rows_per_step == 0 and b % gb == 0
    blk = lambda width: pl.BlockSpec((gb, rows_per_step, width), lambda bi, si: (bi, si, 0))
    st = lambda r, w: pl.BlockSpec((gb, nh, r, w), lambda bi, si: (bi, 0, 0, 0))
    return pl.pallas_call(
        functools.partial(_mlstm_kernel, chunk=chunk, nchunks=rows_per_step // chunk, nh=nh,
                          single_step=(s == rows_per_step)),
        grid=(b // gb, s // rows_per_step),
        in_specs=[blk(d), blk(d), blk(d), blk(d), blk(LANES),
                  st(dh, dh), st(1, dh), st(SUBLANES, LANES),
                  pl.BlockSpec((1, d), lambda bi, si: (0, 0))],
        out_specs=[blk(d), st(dh, dh), st(1, dh), st(SUBLANES, LANES)],
        out_shape=[jax.ShapeDtypeStruct((b, s, d), BF16),
                   jax.ShapeDtypeStruct((b, nh, dh, dh), F32),
                   jax.ShapeDtypeStruct((b, nh, 1, dh), F32),
                   jax.ShapeDtypeStruct((b, nh, SUBLANES, LANES), F32)],
        compiler_params=_params("parallel", "arbitrary"),
        name="mlstm",
    )(q, k, v, o, gt, c0, n0, m0, mh_g)


def _trunk(x, state_conv, cache_k, cache_v, page_table, state_c, state_n, state_m, wts):
    b, s, d = x.shape
    m = b * s
    depth = wts["norms"].shape[0]
    cw = wts["conv_w0"].shape[2]
    dh = wts["lam0"].shape[2]
    aw = (wts["w_in0"].shape[2] - 3 * cw) // 3
    nh_a = aw // (2 * dh)
    nh_m = state_c.shape[2]
    dh_m = d // nh_m
    h = x.reshape(m, d)
    ks, vs, convs, cs, ns, ms = [], [], [], [], [], []
    for layer in range(depth):
        j = layer // 2
        g = wts["norms"][layer]
        if layer % 2 == 0:
            lam_init = _lambda_init(layer)
            prompt = cache_k is None
            proj = _inproj0(h, g[0:1], wts["w_in0"][j], wts["w_in0_qv_t"][j], cw, aw,
                            dh ** -0.5 * LOG2_E, prompt)
            gb, cx, k, v = proj[:4]
            sg = wts["subln_g0"][j].reshape(1, 2 * dh)
            if prompt:
                kb, qt, vt = proj[4:]
                attn = _attn_prompt(qt, kb.reshape(b, s, aw), vt, wts["lam0"][j], sg, b, s, nh_a, dh,
                                    lam_init).reshape(m, aw)
            else:
                attn = _attn_decode(proj[4], k, v, cache_k, cache_v, j, page_table, wts["lam0"][j], sg,
                                    b, s, nh_a, dh, lam_init)
            h = _outproj_conv(gb, cx, attn, h, wts["w_out0"][j], g[1:2], wts["conv_w0"][j],
                              state_conv[j], s)
            ks.append(k.reshape(b, s, nh_a, 2 * dh))
            vs.append(v.reshape(b, s, nh_a, 2 * dh))
            convs.append(cx.reshape(b, s, cw)[:, s - 2:, :])
        else:
            q, k, v, o, gt = _inproj1(h, g[0:1], wts["w_in1"][j], wts["w_in1_gates"][j], wts["b_if1"][j],
                                      nh_m, dh_m ** -0.5)
            chunk = MLSTM_CHUNK if s % MLSTM_CHUNK == 0 else s
            sp = s
            r3 = lambda a: a.reshape(b, s, a.shape[1])
            q, k, v, o, gt = r3(q), r3(k), r3(v), r3(o), r3(gt)
            if chunk % SUBLANES != 0:
                sp = -(-s // SUBLANES) * SUBLANES
                chunk = sp
                pad = lambda a: jnp.pad(a, ((0, 0), (0, sp - s), (0, 0)))
                lane = lax.broadcasted_iota(jnp.int32, (b, sp - s, LANES), 2)
                gt_pad = jnp.where(lane < nh_m, -jnp.inf, 0.0).astype(F32)
                q, k, v, o = pad(q), pad(k), pad(v), pad(o)
                gt = jnp.concatenate([gt, gt_pad], axis=1)
            rows = min(MLSTM_ROWS, sp)
            m0 = jnp.broadcast_to(state_m[j][:, :, None, None], (b, nh_m, SUBLANES, LANES))
            hn, c_new, n_new, m_new = _mlstm(q, k, v, o, gt, state_c[j], state_n[j][:, :, None, :], m0,
                                             wts["mh_g1"][j].reshape(1, d), nh_m, chunk, rows)
            hn = hn[:, :s].reshape(m, d)
            h = _outproj(hn, h, wts["w_out1"][j], g[1:2])
            cs.append(c_new)
            ns.append(n_new[:, :, 0, :])
            ms.append(m_new[:, :, 0, 0])
        h = _ffn(h, g[2:3], wts["w_gu"][layer], wts["w_down"][layer], g[3:4])
    return (h.reshape(b, s, d), jnp.stack(ks), jnp.stack(vs), jnp.stack(convs),
            jnp.stack(cs), jnp.stack(ns), jnp.stack(ms))


def _qv_transposed(w_in0, cw):
    aw = (w_in0.shape[2] - 3 * cw) // 3
    wq = w_in0[:, :, 3 * cw:3 * cw + aw]
    wv = w_in0[:, :, 3 * cw + 2 * aw:]
    return jnp.swapaxes(jnp.concatenate([wq, wv], axis=2), 1, 2).astype(BF16)


def kernel(x_prompt, x_sample, cache_k, cache_v, state_conv, state_C, state_n, state_m, page_table,
           norms, w_in0, conv_w0, lam0, subln_g0, w_out0, w_in1, b_if1, mh_g1, w_out1, w_gu, w_down):
    d = x_prompt.shape[-1]
    nh_m = state_C.shape[2]
    n_odd = w_in1.shape[0]
    w_gates = jnp.pad(w_in1[:, :, 4 * d:], ((0, 0), (0, 0), (0, LANES - 2 * nh_m))).astype(BF16)
    b_pad = jnp.pad(b_if1.astype(F32), ((0, 0), (0, LANES - 2 * nh_m))).reshape(n_odd, 1, LANES)
    wts = dict(
        norms=norms.astype(F32), w_in0=w_in0.astype(BF16), conv_w0=conv_w0, lam0=lam0.astype(F32),
        w_in0_qv_t=_qv_transposed(w_in0, conv_w0.shape[2]),
        subln_g0=subln_g0.astype(F32), w_out0=w_out0.astype(BF16), w_in1=w_in1.astype(BF16),
        w_in1_gates=w_gates, b_if1=b_pad, mh_g1=mh_g1.astype(F32), w_out1=w_out1.astype(BF16),
        w_gu=w_gu.astype(BF16), w_down=w_down.astype(BF16))
    bp = x_prompt.shape[0]
    n_even = state_conv.shape[0]
    conv0 = jnp.zeros((n_even, bp) + state_conv.shape[2:], x_prompt.dtype)
    c0 = jnp.zeros((n_odd, bp) + state_C.shape[2:], F32)
    n0 = jnp.zeros((n_odd, bp) + state_n.shape[2:], F32)
    m0 = jnp.zeros((n_odd, bp) + state_m.shape[2:], F32)
    y_p, k_p, v_p, conv_p, c_p, n_p, m_p = _trunk(x_prompt, conv0, None, None, None, c0, n0, m0, wts)
    y_s, k_s, v_s, conv_s, c_s, n_s, m_s = _trunk(x_sample, state_conv, cache_k, cache_v, page_table,
                                                  state_C.astype(F32), state_n.astype(F32),
                                                  state_m.astype(F32), wts)
    return (y_p, y_s, k_p, v_p, conv_p, c_p, n_p, m_p, k_s, v_s, conv_s, c_s, n_s, m_s)
```

```python
import functools
import math

import jax
import jax.numpy as jnp
from jax import lax
from jax.experimental import pallas as pl
from jax.experimental.pallas import tpu as pltpu

EPS = 1e-6
LOG2_E = math.log2(math.e)
F32 = jnp.float32
BF16 = jnp.bfloat16

V7X_VMEM_BYTES = 64 * 1024 * 1024
VMEM_LIMIT_BYTES = V7X_VMEM_BYTES - 8 * 1024 * 1024
LANES = 128
SUBLANES = 8

ROW_TILE = 512
FFN_ROW_TILE = 512
FFN_COL_TILE = 256
ATTN_Q_TILE = 256
ATTN_K_TILE = 1024
DECODE_PAGES = 16
DECODE_GROUP = 16
MLSTM_CHUNK = 256
MLSTM_ROWS = 256
MLSTM_SEQS = 2


def _params(*semantics):
    return pltpu.CompilerParams(dimension_semantics=semantics, vmem_limit_bytes=VMEM_LIMIT_BYTES)


def _dot(a, b):
    return jnp.dot(a, b, preferred_element_type=F32)


def _dot_nt(a, b):
    return lax.dot_general(a, b, (((1,), (1,)), ((), ())), preferred_element_type=F32)


def _dot_tn(a, b):
    return lax.dot_general(a, b, (((0,), (0,)), ((), ())), preferred_element_type=F32)


def _rms(x, g):
    return x * lax.rsqrt(jnp.mean(x * x, axis=-1, keepdims=True) + EPS) * g


def _lambda_init(layer):
    return 0.8 - 0.6 * math.exp(-0.3 * layer)


def _lam_value(lam_ref, lam_init):
    lv = lam_ref[...]
    a = jnp.sum(lv[0:1] * lv[1:2], axis=-1, keepdims=True)
    b = jnp.sum(lv[2:3] * lv[3:4], axis=-1, keepdims=True)
    return jnp.exp(a) - jnp.exp(b) + lam_init


def _row_tile(m, want):
    t = min(want, m)
    assert m % t == 0, (m, t)
    return t


def _inproj0_kernel(x_ref, g_ref, w_ref, *rest, cw, aw, qscale, transposed):
    xn = _rms(x_ref[...], g_ref[...]).astype(BF16)

    def proj(lo, width):
        return _dot(xn, w_ref[:, lo:lo + width])

    if transposed:
        wt_ref, gb_ref, cx_ref, k_ref, v_ref, kb_ref, qt_ref, vt_ref = rest
        qt_ref[...] = (_dot_nt(wt_ref[0:aw, :], xn) * qscale).astype(BF16)
        vt_ref[...] = _dot_nt(wt_ref[aw:2 * aw, :], xn).astype(BF16)
    else:
        gb_ref, cx_ref, k_ref, v_ref, q_ref = rest
        q_ref[...] = (proj(3 * cw, aw) * qscale).astype(BF16)
    gb_ref[...] = proj(0, cw)
    cx_ref[...] = proj(cw, cw) * proj(2 * cw, cw)
    k = proj(3 * cw + aw, aw)
    v = proj(3 * cw + 2 * aw, aw)
    tm = k.shape[0]
    nh = aw // LANES
    for h in range(nh):
        k_ref[pl.ds(h, tm, stride=nh), :] = k[:, h * LANES:(h + 1) * LANES]
        v_ref[pl.ds(h, tm, stride=nh), :] = v[:, h * LANES:(h + 1) * LANES]
    if transposed:
        kb_ref[...] = k.astype(BF16)


def _inproj0(x, g, w, wt, cw, aw, qscale, transposed):
    m, d = x.shape
    tm = _row_tile(m, ROW_TILE)
    n = w.shape[1]
    row = lambda width: pl.BlockSpec((tm, width), lambda i: (i, 0))
    const = lambda shape: pl.BlockSpec(shape, lambda i: (0,) * len(shape))
    nh = aw // LANES
    kv_spec = pl.BlockSpec((tm * nh, LANES), lambda i: (i, 0))
    kv_shape = jax.ShapeDtypeStruct((m * nh, LANES), F32)
    out_specs = [row(cw), row(cw), kv_spec, kv_spec, row(aw)]
    out_shape = [jax.ShapeDtypeStruct((m, cw), F32), jax.ShapeDtypeStruct((m, cw), F32), kv_shape, kv_shape,
                 jax.ShapeDtypeStruct((m, aw), BF16)]
    in_specs = [row(d), const((1, d)), const((d, n))]
    args = [x, g, w]
    if transposed:
        in_specs.append(const(wt.shape))
        args.append(wt)
        out_specs += [pl.BlockSpec((aw, tm), lambda i: (0, i))] * 2
        out_shape += [jax.ShapeDtypeStruct((aw, m), BF16)] * 2
    return pl.pallas_call(
        functools.partial(_inproj0_kernel, cw=cw, aw=aw, qscale=qscale, transposed=transposed),
        grid=(m // tm,),
        in_specs=in_specs,
        out_specs=out_specs,
        out_shape=out_shape,
        compiler_params=_params("parallel"),
        name="inproj0",
    )(*args)


def _attn_prompt_kernel(lam_ref, sg_ref, qt_ref, k_ref, vt_ref, o_ref, *, tq, tk, dh, lam_init):
    nch = tk // tq
    g = pl.program_id(2)
    qt = qt_ref[...]
    sub = lax.broadcasted_iota(jnp.int32, (2 * dh, tq), 0)
    zero = jnp.zeros((2 * dh, tq), qt.dtype)
    qqs = []
    for c in range(nch):
        qc = qt[:, c * tq:(c + 1) * tq]
        qqs.append(jnp.concatenate([jnp.where(sub < dh, qc, zero), jnp.where(sub >= dh, qc, zero)], axis=1))
    cols = 2 * tq

    def run(blocks, carry):
        scores, vts, nks = [], [], []
        for j, masked in blocks:
            off = pl.multiple_of(j * tk, tk)
            k = k_ref[pl.ds(off, tk), :]
            vts.append(vt_ref[:, pl.ds(off, tk)])
            nks.append([(c + 1) * tq if masked else tk for c in range(nch)])
            scores.append([_dot(k[:nks[-1][c]], qqs[c]) for c in range(nch)])
        for (j, masked), sc, vt, nk in zip(blocks, scores, vts, nks):
            probs, stats = [], []
            for c in range(nch):
                m, l, _ = carry[c]
                s = sc[c]
                if masked:
                    r = lax.broadcasted_iota(jnp.int32, (nk[c], cols), 0)
                    cc = lax.broadcasted_iota(jnp.int32, (nk[c], cols), 1)
                    s = jnp.where(r <= c * tq + jnp.where(cc >= tq, cc - tq, cc), s, -jnp.inf)
                m_new = jnp.maximum(m, jnp.max(s, axis=0, keepdims=True))
                alpha = jnp.exp2(m - m_new)
                p = jnp.exp2(s - m_new)
                stats.append((m_new, alpha, alpha * l + jnp.sum(p, axis=0, keepdims=True)))
                probs.append(p.astype(BF16))
            out = []
            for c in range(nch):
                m_new, alpha, l = stats[c]
                acc = alpha * carry[c][2] + _dot(vt[:, :nk[c]], probs[c])
                out.append((m_new, l, acc))
            carry = tuple(out)
        return carry

    init = tuple((jnp.full((1, cols), -jnp.inf, F32), jnp.zeros((1, cols), F32),
                  jnp.zeros((2 * dh, cols), F32)) for _ in range(nch))
    carry = lax.fori_loop(0, g // 2, lambda jj, c: run([(2 * jj, False), (2 * jj + 1, False)], c), init)
    carry = lax.cond(g % 2 == 1,
                     lambda c: run([(g - 1, False), (g, True)], c),
                     lambda c: run([(g, True)], c), carry)

    lam = _lam_value(lam_ref, lam_init)
    for c in range(nch):
        _, l, acc = carry[c]
        o = acc / l
        a = o[:, :tq] - lam * o[:, tq:]
        y = a * lax.rsqrt(jnp.mean(a * a, axis=0, keepdims=True) + EPS) * sg_ref[...] * (1.0 - lam_init)
        o_ref[c * tq:(c + 1) * tq, :] = jnp.transpose(y).astype(o_ref.dtype)


def _attn_prompt(qt, kb, vt, lam, subln_g, b, s, nh, dh, lam_init):
    tk = _row_tile(s, ATTN_K_TILE)
    tq = _row_tile(tk, ATTN_Q_TILE)
    hw = 2 * dh
    assert hw == LANES
    nq = s // tk
    return pl.pallas_call(
        functools.partial(_attn_prompt_kernel, tq=tq, tk=tk, dh=dh, lam_init=lam_init),
        grid=(b, nh, nq),
        in_specs=[pl.BlockSpec(lam.shape, lambda bi, h, qi: (0, 0)),
                  pl.BlockSpec((hw, 1), lambda bi, h, qi: (0, 0)),
                  pl.BlockSpec((hw, tk), lambda bi, h, qi: (h, bi * nq + qi)),
                  pl.BlockSpec((None, s, hw), lambda bi, h, qi: (bi, 0, h)),
                  pl.BlockSpec((hw, s), lambda bi, h, qi: (h, bi))],
        out_specs=pl.BlockSpec((None, tk, hw), lambda bi, h, qi: (bi, qi, h)),
        out_shape=jax.ShapeDtypeStruct((b, s, nh * hw), BF16),
        compiler_params=_params("parallel", "parallel", "arbitrary"),
        name="attn_prompt",
    )(lam, subln_g.reshape(hw, 1), qt, kb, vt)


def _attn_decode_kernel(pt_ref, lam_ref, sg_ref, q_ref, kn_ref, vn_ref, *refs,
                        pages, group, nh, nq, lam_init):
    del pt_ref
    k_refs = refs[:pages]
    v_refs = refs[pages:2 * pages]
    o_ref = refs[2 * pages]
    m_ref, l_ref, acc_ref = refs[2 * pages + 1:]
    step = pl.program_id(1)
    q = q_ref[...]
    rows = q.shape[0]
    hw = q.shape[1]
    prow = k_refs[0].shape[0]

    def masks(width):
        r = lax.broadcasted_iota(jnp.int32, (rows, width), 0)
        c = lax.broadcasted_iota(jnp.int32, (rows, width), 1)
        return r, c, (c % nh) == (r // (2 * nq))

    def update(scores, values):
        m_old = m_ref[...]
        m_new = m_old
        for s in scores:
            m_new = jnp.maximum(m_new, jnp.max(s, axis=-1, keepdims=True))
        alpha = jnp.exp2(m_old - m_new)
        l = alpha * l_ref[...]
        acc = alpha * acc_ref[...]
        for s, v in zip(scores, values):
            p = jnp.exp2(s - m_new)
            l = l + jnp.sum(p, axis=-1, keepdims=True)
            acc = acc + _dot(p.astype(BF16), v)
        m_ref[...] = m_new
        l_ref[...] = l
        acc_ref[...] = acc

    @pl.when(step == 0)
    def _():
        m_ref[...] = jnp.full(m_ref.shape, -jnp.inf, F32)
        l_ref[...] = jnp.zeros(l_ref.shape, F32)
        acc_ref[...] = jnp.zeros(acc_ref.shape, F32)

    _, _, head_ok = masks(prow)
    for g0 in range(0, pages, group):
        scores = [jnp.where(head_ok, _dot_nt(q, k_refs[r][...].astype(BF16)), -jnp.inf)
                  for r in range(g0, g0 + group)]
        update(scores, [v_refs[r][...].astype(BF16) for r in range(g0, g0 + group)])

    @pl.when(step == pl.num_programs(1) - 1)
    def _():
        width = kn_ref.shape[0]
        r, c, ok = masks(width)
        ok = ok & ((c // nh) <= (r % nq)) & (c < nq * nh)
        s = jnp.where(ok, _dot_nt(q, kn_ref[...].astype(BF16)), -jnp.inf)
        update([s], [vn_ref[...].astype(BF16)])
        o = acc_ref[...] / l_ref[...]
        a = o - _lam_value(lam_ref, lam_init) * pltpu.roll(o, rows - nq, 0)
        y = _rms(a, sg_ref[...]) * (1.0 - lam_init)
        for h in range(nh):
            o_ref[:, h * hw:(h + 1) * hw] = y[h * 2 * nq:h * 2 * nq + nq].astype(o_ref.dtype)


def _attn_decode(qb, k_new, v_new, cache_k, cache_v, layer_idx, page_table, lam, subln_g,
                 nb, nq, nh, dh, lam_init):
    hw = 2 * dh
    assert 2 * nq == SUBLANES and hw == LANES and nq * nh <= LANES
    n_pages = page_table.shape[1]
    page = cache_k.shape[2]
    pages = min(DECODE_PAGES, n_pages)
    group = min(DECODE_GROUP, pages)
    assert n_pages % pages == 0 and pages % group == 0
    rows = nh * 2 * nq
    qh = jnp.tile(qb.reshape(nb, nq, nh, hw).transpose(0, 2, 1, 3), (1, 1, 2, 1)).reshape(nb, rows, hw)
    r = lax.broadcasted_iota(jnp.int32, (rows, hw), 0)
    c = lax.broadcasted_iota(jnp.int32, (rows, hw), 1)
    qbd = jnp.where(((r // nq) % 2) == (c // dh), qh, jnp.zeros_like(qh))
    pad_new = lambda a: jnp.pad(a.reshape(nb, nq * nh, hw), ((0, 0), (0, LANES - nq * nh), (0, 0)))
    view = lambda cache: cache.reshape(cache.shape[0], cache.shape[1], page * nh, hw)

    def cache_spec(slot):
        return pl.BlockSpec((None, None, page * nh, hw),
                            lambda b, p, pt: (layer_idx, pt[b, p * pages + slot], 0, 0))

    const = lambda shape: pl.BlockSpec(shape, lambda b, p, pt: (0,) * len(shape))
    per_b = lambda shape: pl.BlockSpec((None,) + shape, lambda b, p, pt: (b,) + (0,) * len(shape))
    grid_spec = pltpu.PrefetchScalarGridSpec(
        num_scalar_prefetch=1,
        grid=(nb, n_pages // pages),
        in_specs=[const(lam.shape), const((1, hw)), per_b((rows, hw)),
                  per_b((LANES, hw)), per_b((LANES, hw))]
                 + [cache_spec(i) for i in range(pages)] * 2,
        out_specs=per_b((nq, nh * hw)),
        scratch_shapes=[pltpu.VMEM((rows, 1), F32), pltpu.VMEM((rows, 1), F32),
                        pltpu.VMEM((rows, hw), F32)],
    )
    out = pl.pallas_call(
        functools.partial(_attn_decode_kernel, pages=pages, group=group, nh=nh, nq=nq, lam_init=lam_init),
        grid_spec=grid_spec,
        out_shape=jax.ShapeDtypeStruct((nb, nq, nh * hw), F32),
        compiler_params=_params("parallel", "arbitrary"),
        name="attn_decode",
    )(page_table, lam, subln_g, qbd, pad_new(k_new), pad_new(v_new),
      *([view(cache_k)] * pages), *([view(cache_v)] * pages))
    return out.reshape(nb * nq, nh * hw)


def _conv_taps(cx, prev1, prev2, cw_ref):
    return cw_ref[0:1, :] * prev2 + cw_ref[1:2, :] * prev1 + cw_ref[2:3, :] * cx


def _outproj_conv_kernel(gb_ref, cx_ref, at_ref, r_ref, w_ref, g_ref, cw_ref, *rest, seq, carry_mode):
    cx = cx_ref[...]
    tm, cwid = cx.shape
    row = lax.broadcasted_iota(jnp.int32, (tm, 1), 0)
    roll1 = pltpu.roll(cx, 1, 0)
    roll2 = pltpu.roll(cx, 2, 0)
    if carry_mode:
        st_ref, o_ref, carry = rest

        @pl.when((pl.program_id(0) * tm) % seq == 0)
        def _():
            carry[0:2, :] = st_ref[...]

        c0 = carry[0:1, :]
        c1 = carry[1:2, :]
        prev1 = jnp.where(row >= 1, roll1, c1)
        prev2 = jnp.where(row >= 2, roll2, jnp.where(row == 1, c1, c0))
        carry[0:2, :] = cx[tm - 2:tm, :]
    else:
        e1_ref, e2_ref, o_ref = rest
        t = row % seq
        prev1 = jnp.where(t >= 1, roll1, e1_ref[...])
        prev2 = jnp.where(t >= 2, roll2, e2_ref[...])
    yconv = (gb_ref[...] * _conv_taps(cx, prev1, prev2, cw_ref)).astype(BF16)
    mix = _dot(yconv, w_ref[0:cwid, :]) + _dot(at_ref[...].astype(BF16), w_ref[cwid:, :])
    o_ref[...] = r_ref[...] + _rms(mix, g_ref[...])


def _outproj_conv(gb, cx, attn, resid, w, g, conv_w, state, seq):
    m, d = resid.shape
    cwid = gb.shape[1]
    tm = _row_tile(m, ROW_TILE)
    carry_mode = seq % tm == 0
    row = lambda width: pl.BlockSpec((tm, width), lambda i: (i, 0))
    const = lambda shape: pl.BlockSpec(shape, lambda i: (0,) * len(shape))
    in_specs = [row(cwid), row(cwid), row(attn.shape[1]), row(d), const(w.shape), const((1, d)),
                const(conv_w.shape)]
    if carry_mode:
        extra = [state]
        in_specs += [pl.BlockSpec((None, 2, cwid), lambda i: ((i * tm) // seq, 0, 0))]
        scratch = [pltpu.VMEM((SUBLANES, cwid), F32)]
    else:
        assert tm % seq == 0 and seq >= 2
        nb = m // seq
        zeros = jnp.zeros((nb, seq - 1, cwid), F32)
        e1 = jnp.concatenate([state[:, 1:2], zeros], axis=1).reshape(m, cwid)
        e2 = jnp.concatenate([state[:, 0:1], state[:, 1:2], zeros[:, 1:]], axis=1).reshape(m, cwid)
        extra = [e1, e2]
        in_specs += [row(cwid), row(cwid)]
        scratch = []
    return pl.pallas_call(
        functools.partial(_outproj_conv_kernel, seq=seq, carry_mode=carry_mode),
        grid=(m // tm,),
        in_specs=in_specs,
        out_specs=row(d),
        out_shape=jax.ShapeDtypeStruct((m, d), F32),
        scratch_shapes=scratch,
        compiler_params=_params("arbitrary"),
        name="outproj_conv",
    )(gb, cx, attn, resid, w, g, conv_w, *extra)


def _outproj_kernel(x_ref, r_ref, w_ref, g_ref, o_ref):
    o_ref[...] = r_ref[...] + _rms(_dot(x_ref[...], w_ref[...]), g_ref[...])


def _outproj(x, resid, w, g):
    m, d = resid.shape
    tm = _row_tile(m, ROW_TILE)
    row = lambda width: pl.BlockSpec((tm, width), lambda i: (i, 0))
    return pl.pallas_call(
        _outproj_kernel,
        grid=(m // tm,),
        in_specs=[row(x.shape[1]), row(d), pl.BlockSpec(w.shape, lambda i: (0, 0)),
                  pl.BlockSpec((1, d), lambda i: (0, 0))],
        out_specs=row(d),
        out_shape=jax.ShapeDtypeStruct((m, d), F32),
        compiler_params=_params("parallel"),
        name="outproj",
    )(x, resid, w, g)


def _ffn_kernel(x_ref, g_in_ref, wgu_ref, wd_ref, g_out_ref, o_ref, act_ref, *, tf):
    x = x_ref[...]
    xn = _rms(x, g_in_ref[...]).astype(BF16)
    dff = wd_ref.shape[0]
    for lo in range(0, dff, tf):
        gate = _dot(xn, wgu_ref[:, lo:lo + tf])
        up = _dot(xn, wgu_ref[:, dff + lo:dff + lo + tf])
        act_ref[:, lo:lo + tf] = (gate * jax.nn.sigmoid(gate) * up).astype(BF16)
    o_ref[...] = x + _rms(_dot(act_ref[...], wd_ref[...]), g_out_ref[...])


def _ffn(x, g_in, w_gu, w_down, layer, g_out):
    m, d = x.shape
    dff = w_down.shape[1]
    tm = _row_tile(m, FFN_ROW_TILE)
    tf = _row_tile(dff, FFN_COL_TILE)
    row = pl.BlockSpec((tm, d), lambda i: (i, 0))
    vec = pl.BlockSpec((1, d), lambda i: (0, 0))
    resident = lambda shape: pl.BlockSpec((None,) + shape, lambda i: (layer, 0, 0),
                                          pipeline_mode=pl.Buffered(1))
    return pl.pallas_call(
        functools.partial(_ffn_kernel, tf=tf),
        grid=(m // tm,),
        in_specs=[row, vec, resident((d, 2 * dff)), resident((dff, d)), vec],
        out_specs=row,
        out_shape=jax.ShapeDtypeStruct((m, d), F32),
        scratch_shapes=[pltpu.VMEM((tm, dff), BF16)],
        compiler_params=_params("parallel"),
        name="ffn",
    )(x, g_in, w_gu, w_down, g_out)


def _inproj1_kernel(x_ref, g_ref, w_ref, wg_ref, b_ref, q_ref, k_ref, v_ref, o_ref, gt_ref,
                    *, d, nh, kscale):
    xn = _rms(x_ref[...], g_ref[...]).astype(BF16)
    q_ref[...] = _dot(xn, w_ref[:, 0:d]).astype(BF16)
    k_ref[...] = (_dot(xn, w_ref[:, d:2 * d]) * kscale).astype(BF16)
    v_ref[...] = _dot(xn, w_ref[:, 2 * d:3 * d]).astype(BF16)
    o_ref[...] = _dot(xn, w_ref[:, 3 * d:4 * d])
    gates = _dot(xn, wg_ref[...]) + b_ref[...]
    lane = lax.broadcasted_iota(jnp.int32, gates.shape, 1)
    gt_ref[...] = jnp.where(lane < nh, gates, jax.nn.log_sigmoid(gates))


def _inproj1(x, g, w, wg, b, nh, kscale):
    m, d = x.shape
    tm = _row_tile(m, ROW_TILE)
    row = lambda width: pl.BlockSpec((tm, width), lambda i: (i, 0))
    const = lambda shape: pl.BlockSpec(shape, lambda i: (0,) * len(shape))
    shapes = [(d, BF16), (d, BF16), (d, BF16), (d, F32), (LANES, F32)]
    return pl.pallas_call(
        functools.partial(_inproj1_kernel, d=d, nh=nh, kscale=kscale),
        grid=(m // tm,),
        in_specs=[row(d), const((1, d)), const(w.shape), const(wg.shape), const((1, LANES))],
        out_specs=[row(wd) for wd, _ in shapes],
        out_shape=[jax.ShapeDtypeStruct((m, wd), dt) for wd, dt in shapes],
        compiler_params=_params("parallel"),
        name="inproj1",
    )(x, g, w, wg, b)


def _mlstm_kernel(q_ref, k_ref, v_ref, o_ref, gt_ref, c0_ref, n0_ref, m0_ref, mhg_ref, *rest,
                  chunk, nchunks, nh, single_step, fuse_out):
    if fuse_out:
        r_ref, w_ref, g_ref, h_ref, c_out, n_out, m_out = rest
    else:
        h_ref, c_out, n_out, m_out = rest
    gb = q_ref.shape[0]
    dh = q_ref.shape[2] // nh
    chains = [(bi, h) for bi in range(gb) for h in range(nh)]

    if not single_step:
        @pl.when(pl.program_id(1) == 0)
        def _():
            c_out[...] = c0_ref[...]
            n_out[...] = n0_ref[...]
            m_out[...] = m0_ref[...]

    L = chunk
    row = lax.broadcasted_iota(jnp.int32, (L, L), 0)
    col = lax.broadcasted_iota(jnp.int32, (L, L), 1)
    tri = col <= row
    eye = col == row

    for c in range(nchunks):
        sl = pl.ds(c * L, L)
        state_in = (c0_ref, n0_ref, m0_ref) if (single_step and c == 0) else (c_out, n_out, m_out)
        part = []
        for bi, h in chains:
            lanes = slice(h * dh, (h + 1) * dh)
            gt = gt_ref[bi, sl, :]
            li_c = gt[:, h:h + 1]
            lf_c = gt[:, nh + h:nh + h + 1]
            lf_r = jnp.sum(jnp.where(eye, lf_c, 0.0), axis=0, keepdims=True)
            li_r = jnp.sum(jnp.where(eye, li_c, 0.0), axis=0, keepdims=True)
            bc_c = jnp.sum(jnp.where(tri, lf_r, 0.0), axis=1, keepdims=True)
            bc_r = jnp.sum(jnp.where(row <= col, lf_c, 0.0), axis=0, keepdims=True)
            x_r = li_r - bc_r
            cm_c = jnp.max(jnp.where(tri, x_r, -jnp.inf), axis=1, keepdims=True)
            m_prev = state_in[2][bi, h, 0:1, 0:1]
            mt_c = bc_c + jnp.maximum(m_prev, cm_c)
            dmat = jnp.exp(jnp.where(tri, (bc_c - mt_c) + x_r, -jnp.inf))
            inter = jnp.exp(bc_c + m_prev - mt_c)
            q = q_ref[bi, sl, lanes]
            k = k_ref[bi, sl, lanes]
            c_old = state_in[0][bi, h]
            part.append(dict(lanes=lanes, li_c=li_c, bc_c=bc_c, mt_c=mt_c, m_prev=m_prev, dmat=dmat,
                             inter=inter, q=q, k=k, c_old=c_old, n_old=state_in[1][bi, h],
                             qk=_dot_nt(q, k), qc=_dot(q, c_old.astype(BF16))))

        mix = [None] * gb
        for (bi, h), p in zip(chains, part):
            v = v_ref[bi, sl, p["lanes"]]
            sqk = p["qk"] * p["dmat"]
            num = p["inter"] * p["qc"] + _dot(sqk.astype(BF16), v)
            den = (p["inter"] * jnp.sum(p["q"].astype(F32) * p["n_old"], axis=1, keepdims=True)
                   + jnp.sum(sqk, axis=1, keepdims=True))
            h_til = num / jnp.maximum(jnp.abs(den), jnp.exp(-p["mt_c"]))
            hcell = jax.nn.sigmoid(o_ref[bi, sl, p["lanes"]]) * h_til
            hc = hcell - jnp.mean(hcell, axis=-1, keepdims=True)
            hn = hc * lax.rsqrt(jnp.mean(hc * hc, axis=-1, keepdims=True) + EPS) * mhg_ref[:, p["lanes"]]
            if fuse_out:
                part_mix = _dot(hn.astype(BF16), w_ref[p["lanes"], :])
                mix[bi] = part_mix if mix[bi] is None else mix[bi] + part_mix
            else:
                h_ref[bi, sl, p["lanes"]] = hn.astype(h_ref.dtype)
        if fuse_out:
            for bi in range(gb):
                h_ref[bi, sl, :] = r_ref[bi, sl, :] + _rms(mix[bi], g_ref[...])

        for (bi, h), p in zip(chains, part):
            v = v_ref[bi, sl, p["lanes"]]
            m_end = p["mt_c"][L - 1:L, :]
            bc_end = p["bc_c"][L - 1:L, :]
            w_end = jnp.exp(bc_end - p["bc_c"] + p["li_c"] - m_end)
            decay = jnp.exp(bc_end + p["m_prev"] - m_end)
            wk = w_end * p["k"].astype(F32)
            c_out[bi, h] = decay * p["c_old"] + _dot_tn(wk.astype(BF16), v)
            n_out[bi, h] = decay * p["n_old"] + jnp.sum(wk, axis=0, keepdims=True)
            m_out[bi, h] = jnp.broadcast_to(m_end, m_out.shape[2:])


def _mlstm(q, k, v, o, gt, c0, n0, m0, mh_g, resid, w_out, g_post, nh, chunk, rows_per_step):
    b, s, d = q.shape
    dh = d // nh
    gb = min(MLSTM_SEQS, b)
    assert rows_per_step % chunk == 0 and s % rows_per_step == 0 and b % gb == 0
    fuse_out = chunk >= LANES
    blk = lambda width: pl.BlockSpec((gb, rows_per_step, width), lambda bi, si: (bi, si, 0))
    st = lambda r, w: pl.BlockSpec((gb, nh, r, w), lambda bi, si: (bi, 0, 0, 0))
    vec = pl.BlockSpec((1, d), lambda bi, si: (0, 0))
    in_specs = [blk(d), blk(d), blk(d), blk(d), blk(LANES), st(dh, dh), st(1, dh), st(SUBLANES, LANES), vec]
    args = [q, k, v, o, gt, c0, n0, m0, mh_g]
    if fuse_out:
        in_specs += [blk(d), pl.BlockSpec((d, d), lambda bi, si: (0, 0)), vec]
        args += [resid, w_out, g_post]
    out = pl.pallas_call(
        functools.partial(_mlstm_kernel, chunk=chunk, nchunks=rows_per_step // chunk, nh=nh,
                          single_step=(s == rows_per_step), fuse_out=fuse_out),
        grid=(b // gb, s // rows_per_step),
        in_specs=in_specs,
        out_specs=[blk(d), st(dh, dh), st(1, dh), st(SUBLANES, LANES)],
        out_shape=[jax.ShapeDtypeStruct((b, s, d), F32 if fuse_out else BF16),
                   jax.ShapeDtypeStruct((b, nh, dh, dh), F32),
                   jax.ShapeDtypeStruct((b, nh, 1, dh), F32),
                   jax.ShapeDtypeStruct((b, nh, SUBLANES, LANES), F32)],
        compiler_params=_params("parallel", "arbitrary"),
        name="mlstm",
    )(*args)
    if fuse_out:
        return out
    h = _outproj(out[0].reshape(b * s, d), resid.reshape(b * s, d), w_out, g_post).reshape(b, s, d)
    return (h,) + tuple(out[1:])


def _trunk(x, state_conv, cache_k, cache_v, page_table, state_c, state_n, state_m, wts):
    b, s, d = x.shape
    m = b * s
    depth = wts["norms"].shape[0]
    cw = wts["conv_w0"].shape[2]
    dh = wts["lam0"].shape[2]
    aw = (wts["w_in0"].shape[2] - 3 * cw) // 3
    nh_a = aw // (2 * dh)
    nh_m = state_c.shape[2]
    dh_m = d // nh_m
    h = x.reshape(m, d)
    ks, vs, convs, cs, ns, ms = [], [], [], [], [], []
    for layer in range(depth):
        j = layer // 2
        g = wts["norms"][layer]
        if layer % 2 == 0:
            lam_init = _lambda_init(layer)
            prompt = cache_k is None
            proj = _inproj0(h, g[0:1], wts["w_in0"][j], wts["w_in0_qv_t"][j], cw, aw,
                            dh ** -0.5 * LOG2_E, prompt)
            gb, cx, k, v = proj[:4]
            sg = wts["subln_g0"][j].reshape(1, 2 * dh)
            if prompt:
                kb, qt, vt = proj[4:]
                attn = _attn_prompt(qt, kb.reshape(b, s, aw), vt, wts["lam0"][j], sg, b, s, nh_a, dh,
                                    lam_init).reshape(m, aw)
            else:
                attn = _attn_decode(proj[4], k, v, cache_k, cache_v, j, page_table, wts["lam0"][j], sg,
                                    b, s, nh_a, dh, lam_init)
            h = _outproj_conv(gb, cx, attn, h, wts["w_out0"][j], g[1:2], wts["conv_w0"][j],
                              state_conv[j], s)
            ks.append(k.reshape(b, s, nh_a, 2 * dh))
            vs.append(v.reshape(b, s, nh_a, 2 * dh))
            convs.append(cx.reshape(b, s, cw)[:, s - 2:, :])
        else:
            q, k, v, o, gt = _inproj1(h, g[0:1], wts["w_in1"][j], wts["w_in1_gates"][j], wts["b_if1"][j],
                                      nh_m, dh_m ** -0.5)
            chunk = MLSTM_CHUNK if s % MLSTM_CHUNK == 0 else s
            sp = s
            r3 = lambda a: a.reshape(b, s, a.shape[1])
            q, k, v, o, gt, resid = r3(q), r3(k), r3(v), r3(o), r3(gt), r3(h)
            if chunk % SUBLANES != 0:
                sp = -(-s // SUBLANES) * SUBLANES
                chunk = sp
                pad = lambda a: jnp.pad(a, ((0, 0), (0, sp - s), (0, 0)))
                lane = lax.broadcasted_iota(jnp.int32, (b, sp - s, LANES), 2)
                gt_pad = jnp.where(lane < nh_m, -jnp.inf, 0.0).astype(F32)
                q, k, v, o, resid = pad(q), pad(k), pad(v), pad(o), pad(resid)
                gt = jnp.concatenate([gt, gt_pad], axis=1)
            rows = min(MLSTM_ROWS, sp)
            m0 = jnp.broadcast_to(state_m[j][:, :, None, None], (b, nh_m, SUBLANES, LANES))
            h3, c_new, n_new, m_new = _mlstm(q, k, v, o, gt, state_c[j], state_n[j][:, :, None, :], m0,
                                             wts["mh_g1"][j].reshape(1, d), resid, wts["w_out1"][j], g[1:2],
                                             nh_m, chunk, rows)
            h = h3[:, :s].reshape(m, d)
            cs.append(c_new)
            ns.append(n_new[:, :, 0, :])
            ms.append(m_new[:, :, 0, 0])
        h = _ffn(h, g[2:3], wts["w_gu"], wts["w_down"], layer, g[3:4])
    return (h.reshape(b, s, d), jnp.stack(ks), jnp.stack(vs), jnp.stack(convs),
            jnp.stack(cs), jnp.stack(ns), jnp.stack(ms))


def _qv_transposed(w_in0, cw):
    aw = (w_in0.shape[2] - 3 * cw) // 3
    wq = w_in0[:, :, 3 * cw:3 * cw + aw]
    wv = w_in0[:, :, 3 * cw + 2 * aw:]
    return jnp.swapaxes(jnp.concatenate([wq, wv], axis=2), 1, 2).astype(BF16)


def kernel(x_prompt, x_sample, cache_k, cache_v, state_conv, state_C, state_n, state_m, page_table,
           norms, w_in0, conv_w0, lam0, subln_g0, w_out0, w_in1, b_if1, mh_g1, w_out1, w_gu, w_down):
    d = x_prompt.shape[-1]
    nh_m = state_C.shape[2]
    n_odd = w_in1.shape[0]
    w_gates = jnp.pad(w_in1[:, :, 4 * d:], ((0, 0), (0, 0), (0, LANES - 2 * nh_m))).astype(BF16)
    b_pad = jnp.pad(b_if1.astype(F32), ((0, 0), (0, LANES - 2 * nh_m))).reshape(n_odd, 1, LANES)
    wts = dict(
        norms=norms.astype(F32), w_in0=w_in0.astype(BF16), conv_w0=conv_w0, lam0=lam0.astype(F32),
        w_in0_qv_t=_qv_transposed(w_in0, conv_w0.shape[2]),
        subln_g0=subln_g0.astype(F32), w_out0=w_out0.astype(BF16), w_in1=w_in1.astype(BF16),
        w_in1_gates=w_gates, b_if1=b_pad, mh_g1=mh_g1.astype(F32), w_out1=w_out1.astype(BF16),
        w_gu=w_gu.astype(BF16), w_down=w_down.astype(BF16))
    bp = x_prompt.shape[0]
    n_even = state_conv.shape[0]
    conv0 = jnp.zeros((n_even, bp) + state_conv.shape[2:], x_prompt.dtype)
    c0 = jnp.zeros((n_odd, bp) + state_C.shape[2:], F32)
    n0 = jnp.zeros((n_odd, bp) + state_n.shape[2:], F32)
    m0 = jnp.zeros((n_odd, bp) + state_m.shape[2:], F32)
    y_p, k_p, v_p, conv_p, c_p, n_p, m_p = _trunk(x_prompt, conv0, None, None, None, c0, n0, m0, wts)
    y_s, k_s, v_s, conv_s, c_s, n_s, m_s = _trunk(x_sample, state_conv, cache_k, cache_v, page_table,
                                                  state_C.astype(F32), state_n.astype(F32),
                                                  state_m.astype(F32), wts)
    return (y_p, y_s, k_p, v_p, conv_p, c_p, n_p, m_p, k_s, v_s, conv_s, c_s, n_s, m_s)
```

```python
import functools
import math

import jax
import jax.numpy as jnp
from jax import lax
from jax.experimental import pallas as pl
from jax.experimental.pallas import tpu as pltpu

EPS = 1e-6
LOG2_E = math.log2(math.e)
F32 = jnp.float32
BF16 = jnp.bfloat16

V7X_VMEM_BYTES = 64 * 1024 * 1024
VMEM_LIMIT_BYTES = V7X_VMEM_BYTES - 8 * 1024 * 1024
LANES = 128
SUBLANES = 8

ROW_TILE = 512
FFN_ROW_TILE = 512
FFN_COL_TILE = 256
ATTN_Q_TILE = 256
ATTN_K_TILE = 1024
DECODE_PAGES = 16
DECODE_GROUP = 16
MLSTM_CHUNK = 256
MLSTM_ROWS = 256
MLSTM_SEQS = 2


def _params(*semantics):
    return pltpu.CompilerParams(dimension_semantics=semantics, vmem_limit_bytes=VMEM_LIMIT_BYTES)


def _dot(a, b):
    return jnp.dot(a, b, preferred_element_type=F32)


def _dot_nt(a, b):
    return lax.dot_general(a, b, (((1,), (1,)), ((), ())), preferred_element_type=F32)


def _dot_tn(a, b):
    return lax.dot_general(a, b, (((0,), (0,)), ((), ())), preferred_element_type=F32)


def _rms(x, g):
    return x * lax.rsqrt(jnp.mean(x * x, axis=-1, keepdims=True) + EPS) * g


def _lambda_init(layer):
    return 0.8 - 0.6 * math.exp(-0.3 * layer)


def _lam_value(lam_ref, lam_init):
    lv = lam_ref[...]
    a = jnp.sum(lv[0:1] * lv[1:2], axis=-1, keepdims=True)
    b = jnp.sum(lv[2:3] * lv[3:4], axis=-1, keepdims=True)
    return jnp.exp(a) - jnp.exp(b) + lam_init


def _row_tile(m, want):
    t = min(want, m)
    assert m % t == 0, (m, t)
    return t


def _inproj0_kernel(x_ref, g_ref, w_ref, *rest, cw, aw, qscale, transposed):
    xn = _rms(x_ref[...], g_ref[...]).astype(BF16)

    def proj(lo, width):
        return _dot(xn, w_ref[:, lo:lo + width])

    if transposed:
        wt_ref, gb_ref, cx_ref, k_ref, v_ref, kb_ref, qt_ref, vt_ref = rest
        qt_ref[...] = (_dot_nt(wt_ref[0:aw, :], xn) * qscale).astype(BF16)
        vt_ref[...] = _dot_nt(wt_ref[aw:2 * aw, :], xn).astype(BF16)
    else:
        gb_ref, cx_ref, k_ref, v_ref, q_ref = rest
        q_ref[...] = (proj(3 * cw, aw) * qscale).astype(BF16)
    gb_ref[...] = proj(0, cw)
    cx_ref[...] = proj(cw, cw) * proj(2 * cw, cw)
    k = proj(3 * cw + aw, aw)
    v = proj(3 * cw + 2 * aw, aw)
    tm = k.shape[0]
    nh = aw // LANES
    for h in range(nh):
        k_ref[pl.ds(h, tm, stride=nh), :] = k[:, h * LANES:(h + 1) * LANES]
        v_ref[pl.ds(h, tm, stride=nh), :] = v[:, h * LANES:(h + 1) * LANES]
    if transposed:
        kb_ref[...] = k.astype(BF16)


def _inproj0(x, g, w, wt, cw, aw, qscale, transposed):
    m, d = x.shape
    tm = _row_tile(m, ROW_TILE)
    n = w.shape[1]
    row = lambda width: pl.BlockSpec((tm, width), lambda i: (i, 0))
    const = lambda shape: pl.BlockSpec(shape, lambda i: (0,) * len(shape))
    nh = aw // LANES
    kv_spec = pl.BlockSpec((tm * nh, LANES), lambda i: (i, 0))
    kv_shape = jax.ShapeDtypeStruct((m * nh, LANES), F32)
    out_specs = [row(cw), row(cw), kv_spec, kv_spec, row(aw)]
    out_shape = [jax.ShapeDtypeStruct((m, cw), F32), jax.ShapeDtypeStruct((m, cw), F32), kv_shape, kv_shape,
                 jax.ShapeDtypeStruct((m, aw), BF16)]
    in_specs = [row(d), const((1, d)), const((d, n))]
    args = [x, g, w]
    if transposed:
        in_specs.append(const(wt.shape))
        args.append(wt)
        out_specs += [pl.BlockSpec((aw, tm), lambda i: (0, i))] * 2
        out_shape += [jax.ShapeDtypeStruct((aw, m), BF16)] * 2
    return pl.pallas_call(
        functools.partial(_inproj0_kernel, cw=cw, aw=aw, qscale=qscale, transposed=transposed),
        grid=(m // tm,),
        in_specs=in_specs,
        out_specs=out_specs,
        out_shape=out_shape,
        compiler_params=_params("parallel"),
        name="inproj0",
    )(*args)


def _attn_prompt_kernel(lam_ref, sg_ref, qt_ref, k_ref, vt_ref, o_ref, *, tq, tk, dh, lam_init):
    nch = tk // tq
    g = pl.program_id(2)
    qt = qt_ref[...]
    sub = lax.broadcasted_iota(jnp.int32, (2 * dh, tq), 0)
    zero = jnp.zeros((2 * dh, tq), qt.dtype)
    qqs = []
    for c in range(nch):
        qc = qt[:, c * tq:(c + 1) * tq]
        qqs.append(jnp.concatenate([jnp.where(sub < dh, qc, zero), jnp.where(sub >= dh, qc, zero)], axis=1))
    cols = 2 * tq

    def run(blocks, carry):
        scores, vts, nks = [], [], []
        for j, masked in blocks:
            off = pl.multiple_of(j * tk, tk)
            k = k_ref[pl.ds(off, tk), :]
            vts.append(vt_ref[:, pl.ds(off, tk)])
            nks.append([(c + 1) * tq if masked else tk for c in range(nch)])
            scores.append([_dot(k[:nks[-1][c]], qqs[c]) for c in range(nch)])
        for (j, masked), sc, vt, nk in zip(blocks, scores, vts, nks):
            probs, stats = [], []
            for c in range(nch):
                m, l, _ = carry[c]
                s = sc[c]
                if masked:
                    r = lax.broadcasted_iota(jnp.int32, (nk[c], cols), 0)
                    cc = lax.broadcasted_iota(jnp.int32, (nk[c], cols), 1)
                    s = jnp.where(r <= c * tq + jnp.where(cc >= tq, cc - tq, cc), s, -jnp.inf)
                m_new = jnp.maximum(m, jnp.max(s, axis=0, keepdims=True))
                alpha = jnp.exp2(m - m_new)
                p = jnp.exp2(s - m_new)
                stats.append((m_new, alpha, alpha * l + jnp.sum(p, axis=0, keepdims=True)))
                probs.append(p.astype(BF16))
            out = []
            for c in range(nch):
                m_new, alpha, l = stats[c]
                acc = alpha * carry[c][2] + _dot(vt[:, :nk[c]], probs[c])
                out.append((m_new, l, acc))
            carry = tuple(out)
        return carry

    init = tuple((jnp.full((1, cols), -jnp.inf, F32), jnp.zeros((1, cols), F32),
                  jnp.zeros((2 * dh, cols), F32)) for _ in range(nch))
    carry = lax.fori_loop(0, g // 2, lambda jj, c: run([(2 * jj, False), (2 * jj + 1, False)], c), init)
    carry = lax.cond(g % 2 == 1,
                     lambda c: run([(g - 1, False), (g, True)], c),
                     lambda c: run([(g, True)], c), carry)

    lam = _lam_value(lam_ref, lam_init)
    for c in range(nch):
        _, l, acc = carry[c]
        o = acc / l
        a = o[:, :tq] - lam * o[:, tq:]
        y = a * lax.rsqrt(jnp.mean(a * a, axis=0, keepdims=True) + EPS) * sg_ref[...] * (1.0 - lam_init)
        o_ref[c * tq:(c + 1) * tq, :] = jnp.transpose(y).astype(o_ref.dtype)


def _attn_prompt(qt, kb, vt, lam, subln_g, b, s, nh, dh, lam_init):
    tk = _row_tile(s, ATTN_K_TILE)
    tq = _row_tile(tk, ATTN_Q_TILE)
    hw = 2 * dh
    assert hw == LANES
    nq = s // tk
    return pl.pallas_call(
        functools.partial(_attn_prompt_kernel, tq=tq, tk=tk, dh=dh, lam_init=lam_init),
        grid=(b, nh, nq),
        in_specs=[pl.BlockSpec(lam.shape, lambda bi, h, qi: (0, 0)),
                  pl.BlockSpec((hw, 1), lambda bi, h, qi: (0, 0)),
                  pl.BlockSpec((hw, tk), lambda bi, h, qi: (h, bi * nq + qi)),
                  pl.BlockSpec((None, s, hw), lambda bi, h, qi: (bi, 0, h)),
                  pl.BlockSpec((hw, s), lambda bi, h, qi: (h, bi))],
        out_specs=pl.BlockSpec((None, tk, hw), lambda bi, h, qi: (bi, qi, h)),
        out_shape=jax.ShapeDtypeStruct((b, s, nh * hw), BF16),
        compiler_params=_params("parallel", "parallel", "arbitrary"),
        name="attn_prompt",
    )(lam, subln_g.reshape(hw, 1), qt, kb, vt)


def _attn_decode_kernel(pt_ref, lam_ref, sg_ref, q_ref, kn_ref, vn_ref, *refs,
                        pages, group, nh, nq, lam_init):
    del pt_ref
    k_refs = refs[:pages]
    v_refs = refs[pages:2 * pages]
    o_ref = refs[2 * pages]
    m_ref, l_ref, acc_ref = refs[2 * pages + 1:]
    step = pl.program_id(1)
    q = q_ref[...]
    rows = q.shape[0]
    hw = q.shape[1]
    prow = k_refs[0].shape[0]

    def masks(width):
        r = lax.broadcasted_iota(jnp.int32, (rows, width), 0)
        c = lax.broadcasted_iota(jnp.int32, (rows, width), 1)
        return r, c, (c % nh) == (r // (2 * nq))

    def update(scores, values):
        m_old = m_ref[...]
        m_new = m_old
        for s in scores:
            m_new = jnp.maximum(m_new, jnp.max(s, axis=-1, keepdims=True))
        alpha = jnp.exp2(m_old - m_new)
        l = alpha * l_ref[...]
        acc = alpha * acc_ref[...]
        for s, v in zip(scores, values):
            p = jnp.exp2(s - m_new)
            l = l + jnp.sum(p, axis=-1, keepdims=True)
            acc = acc + _dot(p.astype(BF16), v)
        m_ref[...] = m_new
        l_ref[...] = l
        acc_ref[...] = acc

    @pl.when(step == 0)
    def _():
        m_ref[...] = jnp.full(m_ref.shape, -jnp.inf, F32)
        l_ref[...] = jnp.zeros(l_ref.shape, F32)
        acc_ref[...] = jnp.zeros(acc_ref.shape, F32)

    _, _, head_ok = masks(prow)
    for g0 in range(0, pages, group):
        scores = [jnp.where(head_ok, _dot_nt(q, k_refs[r][...].astype(BF16)), -jnp.inf)
                  for r in range(g0, g0 + group)]
        update(scores, [v_refs[r][...].astype(BF16) for r in range(g0, g0 + group)])

    @pl.when(step == pl.num_programs(1) - 1)
    def _():
        width = kn_ref.shape[0]
        r, c, ok = masks(width)
        ok = ok & ((c // nh) <= (r % nq)) & (c < nq * nh)
        s = jnp.where(ok, _dot_nt(q, kn_ref[...].astype(BF16)), -jnp.inf)
        update([s], [vn_ref[...].astype(BF16)])
        o = acc_ref[...] / l_ref[...]
        a = o - _lam_value(lam_ref, lam_init) * pltpu.roll(o, rows - nq, 0)
        y = _rms(a, sg_ref[...]) * (1.0 - lam_init)
        for h in range(nh):
            o_ref[:, h * hw:(h + 1) * hw] = y[h * 2 * nq:h * 2 * nq + nq].astype(o_ref.dtype)


def _attn_decode(qb, k_new, v_new, cache_k, cache_v, layer_idx, page_table, lam, subln_g,
                 nb, nq, nh, dh, lam_init):
    hw = 2 * dh
    assert 2 * nq == SUBLANES and hw == LANES and nq * nh <= LANES
    n_pages = page_table.shape[1]
    page = cache_k.shape[2]
    pages = min(DECODE_PAGES, n_pages)
    group = min(DECODE_GROUP, pages)
    assert n_pages % pages == 0 and pages % group == 0
    rows = nh * 2 * nq
    qh = jnp.tile(qb.reshape(nb, nq, nh, hw).transpose(0, 2, 1, 3), (1, 1, 2, 1)).reshape(nb, rows, hw)
    r = lax.broadcasted_iota(jnp.int32, (rows, hw), 0)
    c = lax.broadcasted_iota(jnp.int32, (rows, hw), 1)
    qbd = jnp.where(((r // nq) % 2) == (c // dh), qh, jnp.zeros_like(qh))
    pad_new = lambda a: jnp.pad(a.reshape(nb, nq * nh, hw), ((0, 0), (0, LANES - nq * nh), (0, 0)))
    view = lambda cache: cache.reshape(cache.shape[0], cache.shape[1], page * nh, hw)

    def cache_spec(slot):
        return pl.BlockSpec((None, None, page * nh, hw),
                            lambda b, p, pt: (layer_idx, pt[b, p * pages + slot], 0, 0))

    const = lambda shape: pl.BlockSpec(shape, lambda b, p, pt: (0,) * len(shape))
    per_b = lambda shape: pl.BlockSpec((None,) + shape, lambda b, p, pt: (b,) + (0,) * len(shape))
    grid_spec = pltpu.PrefetchScalarGridSpec(
        num_scalar_prefetch=1,
        grid=(nb, n_pages // pages),
        in_specs=[const(lam.shape), const((1, hw)), per_b((rows, hw)),
                  per_b((LANES, hw)), per_b((LANES, hw))]
                 + [cache_spec(i) for i in range(pages)] * 2,
        out_specs=per_b((nq, nh * hw)),
        scratch_shapes=[pltpu.VMEM((rows, 1), F32), pltpu.VMEM((rows, 1), F32),
                        pltpu.VMEM((rows, hw), F32)],
    )
    out = pl.pallas_call(
        functools.partial(_attn_decode_kernel, pages=pages, group=group, nh=nh, nq=nq, lam_init=lam_init),
        grid_spec=grid_spec,
        out_shape=jax.ShapeDtypeStruct((nb, nq, nh * hw), F32),
        compiler_params=_params("parallel", "arbitrary"),
        name="attn_decode",
    )(page_table, lam, subln_g, qbd, pad_new(k_new), pad_new(v_new),
      *([view(cache_k)] * pages), *([view(cache_v)] * pages))
    return out.reshape(nb * nq, nh * hw)


def _conv_taps(cx, prev1, prev2, cw_ref):
    return cw_ref[0:1, :] * prev2 + cw_ref[1:2, :] * prev1 + cw_ref[2:3, :] * cx


def _ffn_rows(x, g_in_ref, wgu_ref, wd_ref, g_out_ref, act_ref, tf):
    xn = _rms(x, g_in_ref[...]).astype(BF16)
    dff = wd_ref.shape[0]
    for lo in range(0, dff, tf):
        gate = _dot(xn, wgu_ref[:, lo:lo + tf])
        up = _dot(xn, wgu_ref[:, dff + lo:dff + lo + tf])
        act_ref[:, lo:lo + tf] = (gate * jax.nn.sigmoid(gate) * up).astype(BF16)
    return x + _rms(_dot(act_ref[...], wd_ref[...]), g_out_ref[...])


def _outproj_conv_kernel(gb_ref, cx_ref, at_ref, r_ref, w_ref, g_ref, cw_ref,
                         g_in_ref, wgu_ref, wd_ref, g_out_ref, *rest, seq, carry_mode, tf):
    cx = cx_ref[...]
    tm, cwid = cx.shape
    row = lax.broadcasted_iota(jnp.int32, (tm, 1), 0)
    roll1 = pltpu.roll(cx, 1, 0)
    roll2 = pltpu.roll(cx, 2, 0)
    if carry_mode:
        st_ref, o_ref, act_ref, carry = rest

        @pl.when((pl.program_id(0) * tm) % seq == 0)
        def _():
            carry[0:2, :] = st_ref[...]

        c0 = carry[0:1, :]
        c1 = carry[1:2, :]
        prev1 = jnp.where(row >= 1, roll1, c1)
        prev2 = jnp.where(row >= 2, roll2, jnp.where(row == 1, c1, c0))
        carry[0:2, :] = cx[tm - 2:tm, :]
    else:
        e1_ref, e2_ref, o_ref, act_ref = rest
        t = row % seq
        prev1 = jnp.where(t >= 1, roll1, e1_ref[...])
        prev2 = jnp.where(t >= 2, roll2, e2_ref[...])
    yconv = (gb_ref[...] * _conv_taps(cx, prev1, prev2, cw_ref)).astype(BF16)
    mix = _dot(yconv, w_ref[0:cwid, :]) + _dot(at_ref[...].astype(BF16), w_ref[cwid:, :])
    h = r_ref[...] + _rms(mix, g_ref[...])
    o_ref[...] = _ffn_rows(h, g_in_ref, wgu_ref, wd_ref, g_out_ref, act_ref, tf)


def _outproj_conv_ffn(gb, cx, attn, resid, w, g, conv_w, state, seq, g_in, w_gu, w_down, layer, g_out):
    m, d = resid.shape
    cwid = gb.shape[1]
    dff = w_down.shape[1]
    tm = _row_tile(m, FFN_ROW_TILE)
    tf = _row_tile(dff, FFN_COL_TILE)
    carry_mode = seq % tm == 0
    row = lambda width: pl.BlockSpec((tm, width), lambda i: (i, 0))
    const = lambda shape: pl.BlockSpec(shape, lambda i: (0,) * len(shape), pipeline_mode=pl.Buffered(1))
    resident = lambda shape: pl.BlockSpec((None,) + shape, lambda i: (layer, 0, 0),
                                          pipeline_mode=pl.Buffered(1))
    in_specs = [row(cwid), row(cwid), row(attn.shape[1]), row(d), const(w.shape), const((1, d)),
                const(conv_w.shape), const((1, d)), resident((d, 2 * dff)), resident((dff, d)), const((1, d))]
    scratch = [pltpu.VMEM((tm, dff), BF16)]
    if carry_mode:
        extra = [state]
        in_specs += [pl.BlockSpec((None, 2, cwid), lambda i: ((i * tm) // seq, 0, 0))]
        scratch += [pltpu.VMEM((SUBLANES, cwid), F32)]
    else:
        assert tm % seq == 0 and seq >= 2
        nb = m // seq
        zeros = jnp.zeros((nb, seq - 1, cwid), F32)
        e1 = jnp.concatenate([state[:, 1:2], zeros], axis=1).reshape(m, cwid)
        e2 = jnp.concatenate([state[:, 0:1], state[:, 1:2], zeros[:, 1:]], axis=1).reshape(m, cwid)
        extra = [e1, e2]
        in_specs += [row(cwid), row(cwid)]
    return pl.pallas_call(
        functools.partial(_outproj_conv_kernel, seq=seq, carry_mode=carry_mode, tf=tf),
        grid=(m // tm,),
        in_specs=in_specs,
        out_specs=row(d),
        out_shape=jax.ShapeDtypeStruct((m, d), F32),
        scratch_shapes=scratch,
        compiler_params=_params("arbitrary"),
        name="outproj_conv_ffn",
    )(gb, cx, attn, resid, w, g, conv_w, g_in, w_gu, w_down, g_out, *extra)


def _outproj_kernel(x_ref, r_ref, w_ref, g_ref, o_ref):
    o_ref[...] = r_ref[...] + _rms(_dot(x_ref[...], w_ref[...]), g_ref[...])


def _outproj(x, resid, w, g):
    m, d = resid.shape
    tm = _row_tile(m, ROW_TILE)
    row = lambda width: pl.BlockSpec((tm, width), lambda i: (i, 0))
    return pl.pallas_call(
        _outproj_kernel,
        grid=(m // tm,),
        in_specs=[row(x.shape[1]), row(d), pl.BlockSpec(w.shape, lambda i: (0, 0)),
                  pl.BlockSpec((1, d), lambda i: (0, 0))],
        out_specs=row(d),
        out_shape=jax.ShapeDtypeStruct((m, d), F32),
        compiler_params=_params("parallel"),
        name="outproj",
    )(x, resid, w, g)


def _ffn_kernel(x_ref, g_in_ref, wgu_ref, wd_ref, g_out_ref, o_ref, act_ref, *, tf):
    o_ref[...] = _ffn_rows(x_ref[...], g_in_ref, wgu_ref, wd_ref, g_out_ref, act_ref, tf)


def _ffn(x, g_in, w_gu, w_down, layer, g_out):
    m, d = x.shape
    dff = w_down.shape[1]
    tm = _row_tile(m, FFN_ROW_TILE)
    tf = _row_tile(dff, FFN_COL_TILE)
    row = pl.BlockSpec((tm, d), lambda i: (i, 0))
    vec = pl.BlockSpec((1, d), lambda i: (0, 0))
    resident = lambda shape: pl.BlockSpec((None,) + shape, lambda i: (layer, 0, 0),
                                          pipeline_mode=pl.Buffered(1))
    return pl.pallas_call(
        functools.partial(_ffn_kernel, tf=tf),
        grid=(m // tm,),
        in_specs=[row, vec, resident((d, 2 * dff)), resident((dff, d)), vec],
        out_specs=row,
        out_shape=jax.ShapeDtypeStruct((m, d), F32),
        scratch_shapes=[pltpu.VMEM((tm, dff), BF16)],
        compiler_params=_params("parallel"),
        name="ffn",
    )(x, g_in, w_gu, w_down, g_out)


def _inproj1_kernel(x_ref, g_ref, w_ref, wg_ref, b_ref, q_ref, k_ref, v_ref, o_ref, gt_ref,
                    *, d, nh, kscale):
    xn = _rms(x_ref[...], g_ref[...]).astype(BF16)
    q_ref[...] = _dot(xn, w_ref[:, 0:d]).astype(BF16)
    k_ref[...] = (_dot(xn, w_ref[:, d:2 * d]) * kscale).astype(BF16)
    v_ref[...] = _dot(xn, w_ref[:, 2 * d:3 * d]).astype(BF16)
    o_ref[...] = _dot(xn, w_ref[:, 3 * d:4 * d])
    gates = _dot(xn, wg_ref[...]) + b_ref[...]
    lane = lax.broadcasted_iota(jnp.int32, gates.shape, 1)
    gt_ref[...] = jnp.where(lane < nh, gates, jax.nn.log_sigmoid(gates))


def _inproj1(x, g, w, wg, b, nh, kscale):
    m, d = x.shape
    tm = _row_tile(m, ROW_TILE)
    row = lambda width: pl.BlockSpec((tm, width), lambda i: (i, 0))
    const = lambda shape: pl.BlockSpec(shape, lambda i: (0,) * len(shape))
    shapes = [(d, BF16), (d, BF16), (d, BF16), (d, F32), (LANES, F32)]
    return pl.pallas_call(
        functools.partial(_inproj1_kernel, d=d, nh=nh, kscale=kscale),
        grid=(m // tm,),
        in_specs=[row(d), const((1, d)), const(w.shape), const(wg.shape), const((1, LANES))],
        out_specs=[row(wd) for wd, _ in shapes],
        out_shape=[jax.ShapeDtypeStruct((m, wd), dt) for wd, dt in shapes],
        compiler_params=_params("parallel"),
        name="inproj1",
    )(x, g, w, wg, b)


def _gate_vectors(gt, nh, tri, eye, row, col):
    L = gt.shape[0]
    if L % LANES == 0:
        lane = lax.broadcasted_iota(jnp.int32, gt.shape, 1)
        lf = jnp.where((lane >= nh) & (lane < 2 * nh), gt, 0.0)
        ones = jnp.where(tri, 1.0, 0.0).astype(BF16)
        hi = lf.astype(BF16)
        rest = lf - hi.astype(F32)
        mid = rest.astype(BF16)
        lo = (rest - mid.astype(F32)).astype(BF16)
        bc = _dot(ones, hi) + _dot(ones, mid) + _dot(ones, lo)
        gt_t = jnp.transpose(gt)
        bc_t = jnp.transpose(bc)
        li_c = [gt[:, h:h + 1] for h in range(nh)]
        li_r = [gt_t[h:h + 1, :] for h in range(nh)]
        bc_c = [bc[:, nh + h:nh + h + 1] for h in range(nh)]
        bc_r = [bc_t[nh + h:nh + h + 1, :] for h in range(nh)]
        return li_c, li_r, bc_c, bc_r
    li_c, li_r, bc_c, bc_r = [], [], [], []
    for h in range(nh):
        li = gt[:, h:h + 1]
        lf = gt[:, nh + h:nh + h + 1]
        lf_r = jnp.sum(jnp.where(eye, lf, 0.0), axis=0, keepdims=True)
        li_c.append(li)
        li_r.append(jnp.sum(jnp.where(eye, li, 0.0), axis=0, keepdims=True))
        bc_c.append(jnp.sum(jnp.where(tri, lf_r, 0.0), axis=1, keepdims=True))
        bc_r.append(jnp.sum(jnp.where(row <= col, lf, 0.0), axis=0, keepdims=True))
    return li_c, li_r, bc_c, bc_r


def _mlstm_kernel(q_ref, k_ref, v_ref, o_ref, gt_ref, c0_ref, n0_ref, m0_ref, mhg_ref, *rest,
                  chunk, nchunks, nh, single_step, fuse_out):
    if fuse_out:
        r_ref, w_ref, g_ref, h_ref, c_out, n_out, m_out = rest
    else:
        h_ref, c_out, n_out, m_out = rest
    gb = q_ref.shape[0]
    dh = q_ref.shape[2] // nh
    chains = [(bi, h) for bi in range(gb) for h in range(nh)]

    if not single_step:
        @pl.when(pl.program_id(1) == 0)
        def _():
            c_out[...] = c0_ref[...]
            n_out[...] = n0_ref[...]
            m_out[...] = m0_ref[...]

    L = chunk
    row = lax.broadcasted_iota(jnp.int32, (L, L), 0)
    col = lax.broadcasted_iota(jnp.int32, (L, L), 1)
    tri = col <= row
    eye = col == row

    for c in range(nchunks):
        sl = pl.ds(c * L, L)
        state_in = (c0_ref, n0_ref, m0_ref) if (single_step and c == 0) else (c_out, n_out, m_out)
        part = []
        gates = {}
        for bi, h in chains:
            lanes = slice(h * dh, (h + 1) * dh)
            if bi not in gates:
                gates[bi] = _gate_vectors(gt_ref[bi, sl, :], nh, tri, eye, row, col)
            li_c, li_r, bc_c, bc_r = [a[h] for a in gates[bi]]
            x_r = li_r - bc_r
            cm_c = jnp.max(jnp.where(tri, x_r, -jnp.inf), axis=1, keepdims=True)
            m_prev = state_in[2][bi, h, 0:1, 0:1]
            mt_c = bc_c + jnp.maximum(m_prev, cm_c)
            dmat = jnp.exp(jnp.where(tri, (bc_c - mt_c) + x_r, -jnp.inf))
            inter = jnp.exp(bc_c + m_prev - mt_c)
            q = q_ref[bi, sl, lanes]
            k = k_ref[bi, sl, lanes]
            c_old = state_in[0][bi, h]
            part.append(dict(lanes=lanes, li_c=li_c, bc_c=bc_c, mt_c=mt_c, m_prev=m_prev, dmat=dmat,
                             inter=inter, q=q, k=k, c_old=c_old, n_old=state_in[1][bi, h],
                             qk=_dot_nt(q, k), qc=_dot(q, c_old.astype(BF16))))

        mix = [None] * gb
        for (bi, h), p in zip(chains, part):
            v = v_ref[bi, sl, p["lanes"]]
            sqk = p["qk"] * p["dmat"]
            num = p["inter"] * p["qc"] + _dot(sqk.astype(BF16), v)
            den = (p["inter"] * jnp.sum(p["q"].astype(F32) * p["n_old"], axis=1, keepdims=True)
                   + jnp.sum(sqk, axis=1, keepdims=True))
            h_til = num / jnp.maximum(jnp.abs(den), jnp.exp(-p["mt_c"]))
            hcell = jax.nn.sigmoid(o_ref[bi, sl, p["lanes"]]) * h_til
            hc = hcell - jnp.mean(hcell, axis=-1, keepdims=True)
            hn = hc * lax.rsqrt(jnp.mean(hc * hc, axis=-1, keepdims=True) + EPS) * mhg_ref[:, p["lanes"]]
            if fuse_out:
                part_mix = _dot(hn.astype(BF16), w_ref[p["lanes"], :])
                mix[bi] = part_mix if mix[bi] is None else mix[bi] + part_mix
            else:
                h_ref[bi, sl, p["lanes"]] = hn.astype(h_ref.dtype)
        if fuse_out:
            for bi in range(gb):
                h_ref[bi, sl, :] = r_ref[bi, sl, :] + _rms(mix[bi], g_ref[...])

        for (bi, h), p in zip(chains, part):
            v = v_ref[bi, sl, p["lanes"]]
            m_end = p["mt_c"][L - 1:L, :]
            bc_end = p["bc_c"][L - 1:L, :]
            w_end = jnp.exp(bc_end - p["bc_c"] + p["li_c"] - m_end)
            decay = jnp.exp(bc_end + p["m_prev"] - m_end)
            wk = w_end * p["k"].astype(F32)
            c_out[bi, h] = decay * p["c_old"] + _dot_tn(wk.astype(BF16), v)
            n_out[bi, h] = decay * p["n_old"] + jnp.sum(wk, axis=0, keepdims=True)
            m_out[bi, h] = jnp.broadcast_to(m_end, m_out.shape[2:])


def _mlstm(q, k, v, o, gt, c0, n0, m0, mh_g, resid, w_out, g_post, nh, chunk, rows_per_step):
    b, s, d = q.shape
    dh = d // nh
    gb = min(MLSTM_SEQS, b)
    assert rows_per_step % chunk == 0 and s % rows_per_step == 0 and b % gb == 0
    fuse_out = chunk >= LANES
    blk = lambda width: pl.BlockSpec((gb, rows_per_step, width), lambda bi, si: (bi, si, 0))
    st = lambda r, w: pl.BlockSpec((gb, nh, r, w), lambda bi, si: (bi, 0, 0, 0))
    vec = pl.BlockSpec((1, d), lambda bi, si: (0, 0))
    in_specs = [blk(d), blk(d), blk(d), blk(d), blk(LANES), st(dh, dh), st(1, dh), st(SUBLANES, LANES), vec]
    args = [q, k, v, o, gt, c0, n0, m0, mh_g]
    if fuse_out:
        in_specs += [blk(d), pl.BlockSpec((d, d), lambda bi, si: (0, 0)), vec]
        args += [resid, w_out, g_post]
    out = pl.pallas_call(
        functools.partial(_mlstm_kernel, chunk=chunk, nchunks=rows_per_step // chunk, nh=nh,
                          single_step=(s == rows_per_step), fuse_out=fuse_out),
        grid=(b // gb, s // rows_per_step),
        in_specs=in_specs,
        out_specs=[blk(d), st(dh, dh), st(1, dh), st(SUBLANES, LANES)],
        out_shape=[jax.ShapeDtypeStruct((b, s, d), F32 if fuse_out else BF16),
                   jax.ShapeDtypeStruct((b, nh, dh, dh), F32),
                   jax.ShapeDtypeStruct((b, nh, 1, dh), F32),
                   jax.ShapeDtypeStruct((b, nh, SUBLANES, LANES), F32)],
        compiler_params=_params("parallel", "arbitrary"),
        name="mlstm",
    )(*args)
    if fuse_out:
        return out
    h = _outproj(out[0].reshape(b * s, d), resid.reshape(b * s, d), w_out, g_post).reshape(b, s, d)
    return (h,) + tuple(out[1:])


def _trunk(x, state_conv, cache_k, cache_v, page_table, state_c, state_n, state_m, wts):
    b, s, d = x.shape
    m = b * s
    depth = wts["norms"].shape[0]
    cw = wts["conv_w0"].shape[2]
    dh = wts["lam0"].shape[2]
    aw = (wts["w_in0"].shape[2] - 3 * cw) // 3
    nh_a = aw // (2 * dh)
    nh_m = state_c.shape[2]
    dh_m = d // nh_m
    h = x.reshape(m, d)
    ks, vs, convs, cs, ns, ms = [], [], [], [], [], []
    for layer in range(depth):
        j = layer // 2
        g = wts["norms"][layer]
        if layer % 2 == 0:
            lam_init = _lambda_init(layer)
            prompt = cache_k is None
            proj = _inproj0(h, g[0:1], wts["w_in0"][j], wts["w_in0_qv_t"][j], cw, aw,
                            dh ** -0.5 * LOG2_E, prompt)
            gb, cx, k, v = proj[:4]
            sg = wts["subln_g0"][j].reshape(1, 2 * dh)
            if prompt:
                kb, qt, vt = proj[4:]
                attn = _attn_prompt(qt, kb.reshape(b, s, aw), vt, wts["lam0"][j], sg, b, s, nh_a, dh,
                                    lam_init).reshape(m, aw)
            else:
                attn = _attn_decode(proj[4], k, v, cache_k, cache_v, j, page_table, wts["lam0"][j], sg,
                                    b, s, nh_a, dh, lam_init)
            h = _outproj_conv_ffn(gb, cx, attn, h, wts["w_out0"][j], g[1:2], wts["conv_w0"][j],
                                  state_conv[j], s, g[2:3], wts["w_gu"], wts["w_down"], layer, g[3:4])
            ks.append(k.reshape(b, s, nh_a, 2 * dh))
            vs.append(v.reshape(b, s, nh_a, 2 * dh))
            convs.append(cx.reshape(b, s, cw)[:, s - 2:, :])
        else:
            q, k, v, o, gt = _inproj1(h, g[0:1], wts["w_in1"][j], wts["w_in1_gates"][j], wts["b_if1"][j],
                                      nh_m, dh_m ** -0.5)
            chunk = MLSTM_CHUNK if s % MLSTM_CHUNK == 0 else s
            sp = s
            r3 = lambda a: a.reshape(b, s, a.shape[1])
            q, k, v, o, gt, resid = r3(q), r3(k), r3(v), r3(o), r3(gt), r3(h)
            if chunk % SUBLANES != 0:
                sp = -(-s // SUBLANES) * SUBLANES
                chunk = sp
                pad = lambda a: jnp.pad(a, ((0, 0), (0, sp - s), (0, 0)))
                lane = lax.broadcasted_iota(jnp.int32, (b, sp - s, LANES), 2)
                gt_pad = jnp.where(lane < nh_m, -jnp.inf, 0.0).astype(F32)
                q, k, v, o, resid = pad(q), pad(k), pad(v), pad(o), pad(resid)
                gt = jnp.concatenate([gt, gt_pad], axis=1)
            rows = min(MLSTM_ROWS, sp)
            m0 = jnp.broadcast_to(state_m[j][:, :, None, None], (b, nh_m, SUBLANES, LANES))
            h3, c_new, n_new, m_new = _mlstm(q, k, v, o, gt, state_c[j], state_n[j][:, :, None, :], m0,
                                             wts["mh_g1"][j].reshape(1, d), resid, wts["w_out1"][j], g[1:2],
                                             nh_m, chunk, rows)
            h = h3[:, :s].reshape(m, d)
            cs.append(c_new)
            ns.append(n_new[:, :, 0, :])
            ms.append(m_new[:, :, 0, 0])
            h = _ffn(h, g[2:3], wts["w_gu"], wts["w_down"], layer, g[3:4])
    return (h.reshape(b, s, d), jnp.stack(ks), jnp.stack(vs), jnp.stack(convs),
            jnp.stack(cs), jnp.stack(ns), jnp.stack(ms))


def _qv_transposed(w_in0, cw):
    aw = (w_in0.shape[2] - 3 * cw) // 3
    wq = w_in0[:, :, 3 * cw:3 * cw + aw]
    wv = w_in0[:, :, 3 * cw + 2 * aw:]
    return jnp.swapaxes(jnp.concatenate([wq, wv], axis=2), 1, 2).astype(BF16)


def kernel(x_prompt, x_sample, cache_k, cache_v, state_conv, state_C, state_n, state_m, page_table,
           norms, w_in0, conv_w0, lam0, subln_g0, w_out0, w_in1, b_if1, mh_g1, w_out1, w_gu, w_down):
    d = x_prompt.shape[-1]
    nh_m = state_C.shape[2]
    n_odd = w_in1.shape[0]
    w_gates = jnp.pad(w_in1[:, :, 4 * d:], ((0, 0), (0, 0), (0, LANES - 2 * nh_m))).astype(BF16)
    b_pad = jnp.pad(b_if1.astype(F32), ((0, 0), (0, LANES - 2 * nh_m))).reshape(n_odd, 1, LANES)
    wts = dict(
        norms=norms.astype(F32), w_in0=w_in0.astype(BF16), conv_w0=conv_w0, lam0=lam0.astype(F32),
        w_in0_qv_t=_qv_transposed(w_in0, conv_w0.shape[2]),
        subln_g0=subln_g0.astype(F32), w_out0=w_out0.astype(BF16), w_in1=w_in1.astype(BF16),
        w_in1_gates=w_gates, b_if1=b_pad, mh_g1=mh_g1.astype(F32), w_out1=w_out1.astype(BF16),
        w_gu=w_gu.astype(BF16), w_down=w_down.astype(BF16))
    bp = x_prompt.shape[0]
    n_even = state_conv.shape[0]
    conv0 = jnp.zeros((n_even, bp) + state_conv.shape[2:], x_prompt.dtype)
    c0 = jnp.zeros((n_odd, bp) + state_C.shape[2:], F32)
    n0 = jnp.zeros((n_odd, bp) + state_n.shape[2:], F32)
    m0 = jnp.zeros((n_odd, bp) + state_m.shape[2:], F32)
    y_p, k_p, v_p, conv_p, c_p, n_p, m_p = _trunk(x_prompt, conv0, None, None, None, c0, n0, m0, wts)
    y_s, k_s, v_s, conv_s, c_s, n_s, m_s = _trunk(x_sample, state_conv, cache_k, cache_v, page_table,
                                                  state_C.astype(F32), state_n.astype(F32),
                                                  state_m.astype(F32), wts)
    return (y_p, y_s, k_p, v_p, conv_p, c_p, n_p, m_p, k_s, v_s, conv_s, c_s, n_s, m_s)
```

```python
import functools
import math

import jax
import jax.numpy as jnp
from jax import lax
from jax.experimental import pallas as pl
from jax.experimental.pallas import tpu as pltpu

EPS = 1e-6
LOG2_E = math.log2(math.e)
F32 = jnp.float32
BF16 = jnp.bfloat16

V7X_VMEM_BYTES = 64 * 1024 * 1024
VMEM_LIMIT_BYTES = V7X_VMEM_BYTES - 8 * 1024 * 1024
LANES = 128
SUBLANES = 8

ROW_TILE = 512
FFN_ROW_TILE = 512
FFN_COL_TILE = 256
ATTN_Q_TILE = 256
ATTN_K_TILE = 1024
DECODE_PAGES = 16
MLSTM_CHUNK = 512
MLSTM_ROWS = 512
MLSTM_SEQS = 1


def _params(*semantics):
    return pltpu.CompilerParams(dimension_semantics=semantics, vmem_limit_bytes=VMEM_LIMIT_BYTES)


def _dot(a, b):
    return jnp.dot(a, b, preferred_element_type=F32)


def _dot_nt(a, b):
    return lax.dot_general(a, b, (((1,), (1,)), ((), ())), preferred_element_type=F32)


def _dot_tn(a, b):
    return lax.dot_general(a, b, (((0,), (0,)), ((), ())), preferred_element_type=F32)


def _rms(x, g):
    return x * lax.rsqrt(jnp.mean(x * x, axis=-1, keepdims=True) + EPS) * g


def _lambda_init(layer):
    return 0.8 - 0.6 * math.exp(-0.3 * layer)


def _lam_value(lam_ref, lam_init):
    lv = lam_ref[...]
    a = jnp.sum(lv[0:1] * lv[1:2], axis=-1, keepdims=True)
    b = jnp.sum(lv[2:3] * lv[3:4], axis=-1, keepdims=True)
    return jnp.exp(a) - jnp.exp(b) + lam_init


def _row_tile(m, want):
    t = min(want, m)
    assert m % t == 0, (m, t)
    return t


def _inproj0_kernel(x_ref, g_ref, w_ref, *rest, cw, aw, qscale, transposed):
    xn = _rms(x_ref[...], g_ref[...]).astype(BF16)

    def proj(lo, width):
        return _dot(xn, w_ref[:, lo:lo + width])

    if transposed:
        wt_ref, gb_ref, cx_ref, k_ref, v_ref, kb_ref, qt_ref, vt_ref = rest
        qt_ref[...] = (_dot_nt(wt_ref[0:aw, :], xn) * qscale).astype(BF16)
        vt_ref[...] = _dot_nt(wt_ref[aw:2 * aw, :], xn).astype(BF16)
    else:
        gb_ref, cx_ref, k_ref, v_ref, q_ref = rest
        q_ref[...] = (proj(3 * cw, aw) * qscale).astype(BF16)
    gb_ref[...] = proj(0, cw)
    cx_ref[...] = proj(cw, cw) * proj(2 * cw, cw)
    k = proj(3 * cw + aw, aw)
    v = proj(3 * cw + 2 * aw, aw)
    tm = k.shape[0]
    nh = aw // LANES
    for h in range(nh):
        k_ref[pl.ds(h, tm, stride=nh), :] = k[:, h * LANES:(h + 1) * LANES]
        v_ref[pl.ds(h, tm, stride=nh), :] = v[:, h * LANES:(h + 1) * LANES]
    if transposed:
        kb_ref[...] = k.astype(BF16)


def _inproj0(x, g, w, wt, cw, aw, qscale, transposed):
    m, d = x.shape
    tm = _row_tile(m, ROW_TILE)
    n = w.shape[1]
    row = lambda width: pl.BlockSpec((tm, width), lambda i: (i, 0))
    const = lambda shape: pl.BlockSpec(shape, lambda i: (0,) * len(shape))
    nh = aw // LANES
    kv_spec = pl.BlockSpec((tm * nh, LANES), lambda i: (i, 0))
    kv_shape = jax.ShapeDtypeStruct((m * nh, LANES), F32)
    out_specs = [row(cw), row(cw), kv_spec, kv_spec, row(aw)]
    out_shape = [jax.ShapeDtypeStruct((m, cw), F32), jax.ShapeDtypeStruct((m, cw), F32), kv_shape, kv_shape,
                 jax.ShapeDtypeStruct((m, aw), BF16)]
    in_specs = [row(d), const((1, d)), const((d, n))]
    args = [x, g, w]
    if transposed:
        in_specs.append(const(wt.shape))
        args.append(wt)
        out_specs += [pl.BlockSpec((aw, tm), lambda i: (0, i))] * 2
        out_shape += [jax.ShapeDtypeStruct((aw, m), BF16)] * 2
    return pl.pallas_call(
        functools.partial(_inproj0_kernel, cw=cw, aw=aw, qscale=qscale, transposed=transposed),
        grid=(m // tm,),
        in_specs=in_specs,
        out_specs=out_specs,
        out_shape=out_shape,
        compiler_params=_params("parallel"),
        name="inproj0",
    )(*args)


def _attn_prompt_kernel(lam_ref, sg_ref, qt_ref, k_ref, vt_ref, o_ref, *, tq, tk, dh, lam_init):
    nch = tk // tq
    g = pl.program_id(2)
    qt = qt_ref[...]
    sub = lax.broadcasted_iota(jnp.int32, (2 * dh, tq), 0)
    zero = jnp.zeros((2 * dh, tq), qt.dtype)
    qqs = []
    for c in range(nch):
        qc = qt[:, c * tq:(c + 1) * tq]
        qqs.append(jnp.concatenate([jnp.where(sub < dh, qc, zero), jnp.where(sub >= dh, qc, zero)], axis=1))
    cols = 2 * tq

    def run(blocks, carry):
        scores, vts, nks = [], [], []
        for j, masked in blocks:
            off = pl.multiple_of(j * tk, tk)
            k = k_ref[pl.ds(off, tk), :]
            vts.append(vt_ref[:, pl.ds(off, tk)])
            nks.append([(c + 1) * tq if masked else tk for c in range(nch)])
            scores.append([_dot(k[:nks[-1][c]], qqs[c]) for c in range(nch)])
        for (j, masked), sc, vt, nk in zip(blocks, scores, vts, nks):
            probs, stats = [], []
            for c in range(nch):
                m, l, _ = carry[c]
                s = sc[c]
                if masked:
                    r = lax.broadcasted_iota(jnp.int32, (nk[c], cols), 0)
                    cc = lax.broadcasted_iota(jnp.int32, (nk[c], cols), 1)
                    s = jnp.where(r <= c * tq + jnp.where(cc >= tq, cc - tq, cc), s, -jnp.inf)
                m_new = jnp.maximum(m, jnp.max(s, axis=0, keepdims=True))
                alpha = jnp.exp2(m - m_new)
                p = jnp.exp2(s - m_new)
                stats.append((m_new, alpha, alpha * l + jnp.sum(p, axis=0, keepdims=True)))
                probs.append(p.astype(BF16))
            out = []
            for c in range(nch):
                m_new, alpha, l = stats[c]
                acc = alpha * carry[c][2] + _dot(vt[:, :nk[c]], probs[c])
                out.append((m_new, l, acc))
            carry = tuple(out)
        return carry

    init = tuple((jnp.full((1, cols), -jnp.inf, F32), jnp.zeros((1, cols), F32),
                  jnp.zeros((2 * dh, cols), F32)) for _ in range(nch))
    carry = lax.fori_loop(0, g // 2, lambda jj, c: run([(2 * jj, False), (2 * jj + 1, False)], c), init)
    carry = lax.cond(g % 2 == 1,
                     lambda c: run([(g - 1, False), (g, True)], c),
                     lambda c: run([(g, True)], c), carry)

    lam = _lam_value(lam_ref, lam_init)
    for c in range(nch):
        _, l, acc = carry[c]
        o = acc / l
        a = o[:, :tq] - lam * o[:, tq:]
        y = a * lax.rsqrt(jnp.mean(a * a, axis=0, keepdims=True) + EPS) * sg_ref[...] * (1.0 - lam_init)
        o_ref[c * tq:(c + 1) * tq, :] = jnp.transpose(y).astype(o_ref.dtype)


def _attn_prompt(qt, kb, vt, lam, subln_g, b, s, nh, dh, lam_init):
    tk = _row_tile(s, ATTN_K_TILE)
    tq = _row_tile(tk, ATTN_Q_TILE)
    hw = 2 * dh
    assert hw == LANES
    nq = s // tk
    return pl.pallas_call(
        functools.partial(_attn_prompt_kernel, tq=tq, tk=tk, dh=dh, lam_init=lam_init),
        grid=(b, nh, nq),
        in_specs=[pl.BlockSpec(lam.shape, lambda bi, h, qi: (0, 0)),
                  pl.BlockSpec((hw, 1), lambda bi, h, qi: (0, 0)),
                  pl.BlockSpec((hw, tk), lambda bi, h, qi: (h, bi * nq + qi)),
                  pl.BlockSpec((None, s, hw), lambda bi, h, qi: (bi, 0, h)),
                  pl.BlockSpec((hw, s), lambda bi, h, qi: (h, bi))],
        out_specs=pl.BlockSpec((None, tk, hw), lambda bi, h, qi: (bi, qi, h)),
        out_shape=jax.ShapeDtypeStruct((b, s, nh * hw), BF16),
        compiler_params=_params("parallel", "parallel", "arbitrary"),
        name="attn_prompt",
    )(lam, subln_g.reshape(hw, 1), qt, kb, vt)


def _attn_decode_kernel(pt_ref, lam_ref, sg_ref, q_ref, kn_ref, vn_ref, ck_ref, cv_ref, o_ref,
                        kbuf, vbuf, sem, m_ref, l_ref, acc_ref, *, pages, layer_idx, nh, nq, lam_init):
    b = pl.program_id(0)
    step = pl.program_id(1)
    n_steps = pl.num_programs(1)
    n = b * n_steps + step
    half = n % 2
    is_last = n == pl.num_programs(0) * n_steps - 1
    q = q_ref[...]
    rows = q.shape[0]
    hw = q.shape[1]
    prow = kbuf.shape[2]

    def page_reads(bb, ss, r, hf):
        pg = pt_ref[bb, ss * pages + r]
        return (pltpu.make_async_copy(ck_ref.at[layer_idx, pg], kbuf.at[hf, r], sem.at[0, hf]),
                pltpu.make_async_copy(cv_ref.at[layer_idx, pg], vbuf.at[hf, r], sem.at[1, hf]))

    @pl.when(n == 0)
    def _():
        for r in range(pages):
            for cp in page_reads(b, step, r, half):
                cp.start()

    wrap = step == n_steps - 1
    nxt_b = jnp.where(is_last, b, jnp.where(wrap, b + 1, b))
    nxt_s = jnp.where(is_last, step, jnp.where(wrap, 0, step + 1))

    for r in range(pages):
        for cp in page_reads(b, step, r, half):
            cp.wait()

    def masks(width):
        r = lax.broadcasted_iota(jnp.int32, (rows, width), 0)
        c = lax.broadcasted_iota(jnp.int32, (rows, width), 1)
        return r, c, (c % nh) == (r // (2 * nq))

    def update(scores, values):
        m_old = m_ref[...]
        m_new = m_old
        for s in scores:
            m_new = jnp.maximum(m_new, jnp.max(s, axis=-1, keepdims=True))
        alpha = jnp.exp2(m_old - m_new)
        l = alpha * l_ref[...]
        acc = alpha * acc_ref[...]
        for s, v in zip(scores, values):
            p = jnp.exp2(s - m_new)
            l = l + jnp.sum(p, axis=-1, keepdims=True)
            acc = acc + _dot(p.astype(BF16), v)
        m_ref[...] = m_new
        l_ref[...] = l
        acc_ref[...] = acc

    @pl.when(step == 0)
    def _():
        m_ref[...] = jnp.full(m_ref.shape, -jnp.inf, F32)
        l_ref[...] = jnp.zeros(l_ref.shape, F32)
        acc_ref[...] = jnp.zeros(acc_ref.shape, F32)

    _, _, head_ok = masks(prow)
    scores = []
    for r in range(pages):
        for cp in page_reads(nxt_b, nxt_s, r, 1 - half):
            cp.start()
        scores.append(jnp.where(head_ok, _dot_nt(q, kbuf[half, r].astype(BF16)), -jnp.inf))
    update(scores, [vbuf[half, r].astype(BF16) for r in range(pages)])

    @pl.when(is_last)
    def _():
        for r in range(pages):
            for cp in page_reads(nxt_b, nxt_s, r, 1 - half):
                cp.wait()

    @pl.when(step == n_steps - 1)
    def _():
        width = kn_ref.shape[0]
        r, c, ok = masks(width)
        ok = ok & ((c // nh) <= (r % nq)) & (c < nq * nh)
        s = jnp.where(ok, _dot_nt(q, kn_ref[...].astype(BF16)), -jnp.inf)
        update([s], [vn_ref[...].astype(BF16)])
        o = acc_ref[...] / l_ref[...]
        a = o - _lam_value(lam_ref, lam_init) * pltpu.roll(o, rows - nq, 0)
        y = _rms(a, sg_ref[...]) * (1.0 - lam_init)
        for h in range(nh):
            o_ref[:, h * hw:(h + 1) * hw] = y[h * 2 * nq:h * 2 * nq + nq].astype(o_ref.dtype)


def _attn_decode(qb, k_new, v_new, cache_k, cache_v, layer_idx, page_table, lam, subln_g,
                 nb, nq, nh, dh, lam_init):
    hw = 2 * dh
    assert 2 * nq == SUBLANES and hw == LANES and nq * nh <= LANES
    n_pages = page_table.shape[1]
    page = cache_k.shape[2]
    pages = min(DECODE_PAGES, n_pages)
    assert n_pages % pages == 0
    rows = nh * 2 * nq
    qh = jnp.tile(qb.reshape(nb, nq, nh, hw).transpose(0, 2, 1, 3), (1, 1, 2, 1)).reshape(nb, rows, hw)
    r = lax.broadcasted_iota(jnp.int32, (rows, hw), 0)
    c = lax.broadcasted_iota(jnp.int32, (rows, hw), 1)
    qbd = jnp.where(((r // nq) % 2) == (c // dh), qh, jnp.zeros_like(qh))
    pad_new = lambda a: jnp.pad(a.reshape(nb, nq * nh, hw), ((0, 0), (0, LANES - nq * nh), (0, 0)))
    view = lambda cache: cache.reshape(cache.shape[0], cache.shape[1], page * nh, hw)

    const = lambda shape: pl.BlockSpec(shape, lambda b, p, pt: (0,) * len(shape))
    per_b = lambda shape: pl.BlockSpec((None,) + shape, lambda b, p, pt: (b,) + (0,) * len(shape))
    in_hbm = pl.BlockSpec(memory_space=pl.ANY)
    page_buf = pltpu.VMEM((2, pages, page * nh, hw), cache_k.dtype)
    grid_spec = pltpu.PrefetchScalarGridSpec(
        num_scalar_prefetch=1,
        grid=(nb, n_pages // pages),
        in_specs=[const(lam.shape), const((1, hw)), per_b((rows, hw)),
                  per_b((LANES, hw)), per_b((LANES, hw)), in_hbm, in_hbm],
        out_specs=per_b((nq, nh * hw)),
        scratch_shapes=[page_buf, page_buf, pltpu.SemaphoreType.DMA((2, 2)),
                        pltpu.VMEM((rows, 1), F32), pltpu.VMEM((rows, 1), F32),
                        pltpu.VMEM((rows, hw), F32)],
    )
    out = pl.pallas_call(
        functools.partial(_attn_decode_kernel, pages=pages, layer_idx=layer_idx, nh=nh, nq=nq,
                          lam_init=lam_init),
        grid_spec=grid_spec,
        out_shape=jax.ShapeDtypeStruct((nb, nq, nh * hw), F32),
        compiler_params=_params("arbitrary", "arbitrary"),
        name="attn_decode",
    )(page_table, lam, subln_g, qbd, pad_new(k_new), pad_new(v_new), view(cache_k), view(cache_v))
    return out.reshape(nb * nq, nh * hw)


def _conv_taps(cx, prev1, prev2, cw_ref):
    return cw_ref[0:1, :] * prev2 + cw_ref[1:2, :] * prev1 + cw_ref[2:3, :] * cx


def _ffn_rows(x, g_in_ref, wgu_ref, wd_ref, g_out_ref, act_ref, tf):
    xn = _rms(x, g_in_ref[...]).astype(BF16)
    dff = wd_ref.shape[0]
    for lo in range(0, dff, tf):
        gate = _dot(xn, wgu_ref[:, lo:lo + tf])
        up = _dot(xn, wgu_ref[:, dff + lo:dff + lo + tf])
        act_ref[:, lo:lo + tf] = (gate * jax.nn.sigmoid(gate) * up).astype(BF16)
    return x + _rms(_dot(act_ref[...], wd_ref[...]), g_out_ref[...])


def _outproj_conv_kernel(gb_ref, cx_ref, at_ref, r_ref, w_ref, g_ref, cw_ref,
                         g_in_ref, wgu_ref, wd_ref, g_out_ref, *rest, seq, carry_mode, tf):
    cx = cx_ref[...]
    tm, cwid = cx.shape
    row = lax.broadcasted_iota(jnp.int32, (tm, 1), 0)
    roll1 = pltpu.roll(cx, 1, 0)
    roll2 = pltpu.roll(cx, 2, 0)
    if carry_mode:
        st_ref, o_ref, act_ref, carry = rest

        @pl.when((pl.program_id(0) * tm) % seq == 0)
        def _():
            carry[0:2, :] = st_ref[...]

        c0 = carry[0:1, :]
        c1 = carry[1:2, :]
        prev1 = jnp.where(row >= 1, roll1, c1)
        prev2 = jnp.where(row >= 2, roll2, jnp.where(row == 1, c1, c0))
        carry[0:2, :] = cx[tm - 2:tm, :]
    else:
        e1_ref, e2_ref, o_ref, act_ref = rest
        t = row % seq
        prev1 = jnp.where(t >= 1, roll1, e1_ref[...])
        prev2 = jnp.where(t >= 2, roll2, e2_ref[...])
    yconv = (gb_ref[...] * _conv_taps(cx, prev1, prev2, cw_ref)).astype(BF16)
    mix = _dot(yconv, w_ref[0:cwid, :]) + _dot(at_ref[...].astype(BF16), w_ref[cwid:, :])
    h = r_ref[...] + _rms(mix, g_ref[...])
    o_ref[...] = _ffn_rows(h, g_in_ref, wgu_ref, wd_ref, g_out_ref, act_ref, tf)


def _outproj_conv_ffn(gb, cx, attn, resid, w, g, conv_w, state, seq, g_in, w_gu, w_down, layer, g_out):
    m, d = resid.shape
    cwid = gb.shape[1]
    dff = w_down.shape[1]
    tm = _row_tile(m, FFN_ROW_TILE)
    tf = _row_tile(dff, FFN_COL_TILE)
    carry_mode = seq % tm == 0
    row = lambda width: pl.BlockSpec((tm, width), lambda i: (i, 0))
    const = lambda shape: pl.BlockSpec(shape, lambda i: (0,) * len(shape), pipeline_mode=pl.Buffered(1))
    resident = lambda shape: pl.BlockSpec((None,) + shape, lambda i: (layer, 0, 0),
                                          pipeline_mode=pl.Buffered(1))
    in_specs = [row(cwid), row(cwid), row(attn.shape[1]), row(d), const(w.shape), const((1, d)),
                const(conv_w.shape), const((1, d)), resident((d, 2 * dff)), resident((dff, d)), const((1, d))]
    scratch = [pltpu.VMEM((tm, dff), BF16)]
    if carry_mode:
        extra = [state]
        in_specs += [pl.BlockSpec((None, 2, cwid), lambda i: ((i * tm) // seq, 0, 0))]
        scratch += [pltpu.VMEM((SUBLANES, cwid), F32)]
    else:
        assert tm % seq == 0 and seq >= 2
        nb = m // seq
        zeros = jnp.zeros((nb, seq - 1, cwid), F32)
        e1 = jnp.concatenate([state[:, 1:2], zeros], axis=1).reshape(m, cwid)
        e2 = jnp.concatenate([state[:, 0:1], state[:, 1:2], zeros[:, 1:]], axis=1).reshape(m, cwid)
        extra = [e1, e2]
        in_specs += [row(cwid), row(cwid)]
    return pl.pallas_call(
        functools.partial(_outproj_conv_kernel, seq=seq, carry_mode=carry_mode, tf=tf),
        grid=(m // tm,),
        in_specs=in_specs,
        out_specs=row(d),
        out_shape=jax.ShapeDtypeStruct((m, d), F32),
        scratch_shapes=scratch,
        compiler_params=_params("arbitrary"),
        name="outproj_conv_ffn",
    )(gb, cx, attn, resid, w, g, conv_w, g_in, w_gu, w_down, g_out, *extra)


def _outproj_kernel(x_ref, r_ref, w_ref, g_ref, o_ref):
    o_ref[...] = r_ref[...] + _rms(_dot(x_ref[...], w_ref[...]), g_ref[...])


def _outproj(x, resid, w, g):
    m, d = resid.shape
    tm = _row_tile(m, ROW_TILE)
    row = lambda width: pl.BlockSpec((tm, width), lambda i: (i, 0))
    return pl.pallas_call(
        _outproj_kernel,
        grid=(m // tm,),
        in_specs=[row(x.shape[1]), row(d), pl.BlockSpec(w.shape, lambda i: (0, 0)),
                  pl.BlockSpec((1, d), lambda i: (0, 0))],
        out_specs=row(d),
        out_shape=jax.ShapeDtypeStruct((m, d), F32),
        compiler_params=_params("parallel"),
        name="outproj",
    )(x, resid, w, g)


def _ffn_kernel(x_ref, g_in_ref, wgu_ref, wd_ref, g_out_ref, o_ref, act_ref, *, tf):
    o_ref[...] = _ffn_rows(x_ref[...], g_in_ref, wgu_ref, wd_ref, g_out_ref, act_ref, tf)


def _ffn(x, g_in, w_gu, w_down, layer, g_out):
    m, d = x.shape
    dff = w_down.shape[1]
    tm = _row_tile(m, FFN_ROW_TILE)
    tf = _row_tile(dff, FFN_COL_TILE)
    row = pl.BlockSpec((tm, d), lambda i: (i, 0))
    vec = pl.BlockSpec((1, d), lambda i: (0, 0))
    resident = lambda shape: pl.BlockSpec((None,) + shape, lambda i: (layer, 0, 0),
                                          pipeline_mode=pl.Buffered(1))
    return pl.pallas_call(
        functools.partial(_ffn_kernel, tf=tf),
        grid=(m // tm,),
        in_specs=[row, vec, resident((d, 2 * dff)), resident((dff, d)), vec],
        out_specs=row,
        out_shape=jax.ShapeDtypeStruct((m, d), F32),
        scratch_shapes=[pltpu.VMEM((tm, dff), BF16)],
        compiler_params=_params("parallel"),
        name="ffn",
    )(x, g_in, w_gu, w_down, g_out)


def _inproj1_kernel(x_ref, g_ref, w_ref, wg_ref, b_ref, q_ref, k_ref, v_ref, o_ref, gt_ref,
                    *, d, nh, kscale):
    xn = _rms(x_ref[...], g_ref[...]).astype(BF16)
    q_ref[...] = _dot(xn, w_ref[:, 0:d]).astype(BF16)
    k_ref[...] = (_dot(xn, w_ref[:, d:2 * d]) * kscale).astype(BF16)
    v_ref[...] = _dot(xn, w_ref[:, 2 * d:3 * d]).astype(BF16)
    o_ref[...] = _dot(xn, w_ref[:, 3 * d:4 * d])
    gates = _dot(xn, wg_ref[...]) + b_ref[...]
    lane = lax.broadcasted_iota(jnp.int32, gates.shape, 1)
    gt_ref[...] = jnp.where(lane < nh, gates, jax.nn.log_sigmoid(gates))


def _inproj1(x, g, w, wg, b, nh, kscale):
    m, d = x.shape
    tm = _row_tile(m, ROW_TILE)
    row = lambda width: pl.BlockSpec((tm, width), lambda i: (i, 0))
    const = lambda shape: pl.BlockSpec(shape, lambda i: (0,) * len(shape))
    shapes = [(d, BF16), (d, BF16), (d, BF16), (d, F32), (LANES, F32)]
    return pl.pallas_call(
        functools.partial(_inproj1_kernel, d=d, nh=nh, kscale=kscale),
        grid=(m // tm,),
        in_specs=[row(d), const((1, d)), const(w.shape), const(wg.shape), const((1, LANES))],
        out_specs=[row(wd) for wd, _ in shapes],
        out_shape=[jax.ShapeDtypeStruct((m, wd), dt) for wd, dt in shapes],
        compiler_params=_params("parallel"),
        name="inproj1",
    )(x, g, w, wg, b)


def _gate_vectors(gt, nh, tri, eye, row, col):
    L = gt.shape[0]
    if L % LANES == 0:
        lane = lax.broadcasted_iota(jnp.int32, gt.shape, 1)
        lf = jnp.where((lane >= nh) & (lane < 2 * nh), gt, 0.0)
        ones = jnp.where(tri, 1.0, 0.0).astype(BF16)
        hi = lf.astype(BF16)
        rest = lf - hi.astype(F32)
        mid = rest.astype(BF16)
        lo = (rest - mid.astype(F32)).astype(BF16)
        bc = _dot(ones, hi) + _dot(ones, mid) + _dot(ones, lo)
        gt_t = jnp.transpose(gt)
        bc_t = jnp.transpose(bc)
        li_c = [gt[:, h:h + 1] for h in range(nh)]
        li_r = [gt_t[h:h + 1, :] for h in range(nh)]
        bc_c = [bc[:, nh + h:nh + h + 1] for h in range(nh)]
        bc_r = [bc_t[nh + h:nh + h + 1, :] for h in range(nh)]
        return li_c, li_r, bc_c, bc_r
    li_c, li_r, bc_c, bc_r = [], [], [], []
    for h in range(nh):
        li = gt[:, h:h + 1]
        lf = gt[:, nh + h:nh + h + 1]
        lf_r = jnp.sum(jnp.where(eye, lf, 0.0), axis=0, keepdims=True)
        li_c.append(li)
        li_r.append(jnp.sum(jnp.where(eye, li, 0.0), axis=0, keepdims=True))
        bc_c.append(jnp.sum(jnp.where(tri, lf_r, 0.0), axis=1, keepdims=True))
        bc_r.append(jnp.sum(jnp.where(row <= col, lf, 0.0), axis=0, keepdims=True))
    return li_c, li_r, bc_c, bc_r


def _mlstm_kernel(q_ref, k_ref, v_ref, o_ref, gt_ref, c0_ref, n0_ref, m0_ref, mhg_ref, *rest,
                  chunk, nchunks, nh, single_step, fuse_out):
    if fuse_out:
        r_ref, w_ref, g_ref, h_ref, c_out, n_out, m_out = rest
    else:
        h_ref, c_out, n_out, m_out = rest
    gb = q_ref.shape[0]
    dh = q_ref.shape[2] // nh
    chains = [(bi, h) for bi in range(gb) for h in range(nh)]

    if not single_step:
        @pl.when(pl.program_id(1) == 0)
        def _():
            c_out[...] = c0_ref[...]
            n_out[...] = n0_ref[...]
            m_out[...] = m0_ref[...]

    L = chunk
    row = lax.broadcasted_iota(jnp.int32, (L, L), 0)
    col = lax.broadcasted_iota(jnp.int32, (L, L), 1)
    tri = col <= row
    eye = col == row

    for c in range(nchunks):
        sl = pl.ds(c * L, L)
        state_in = (c0_ref, n0_ref, m0_ref) if (single_step and c == 0) else (c_out, n_out, m_out)
        part = []
        gates = {}
        for bi, h in chains:
            lanes = slice(h * dh, (h + 1) * dh)
            if bi not in gates:
                gates[bi] = _gate_vectors(gt_ref[bi, sl, :], nh, tri, eye, row, col)
            li_c, li_r, bc_c, bc_r = [a[h] for a in gates[bi]]
            x_r = li_r - bc_r
            cm_c = jnp.max(jnp.where(tri, x_r, -jnp.inf), axis=1, keepdims=True)
            m_prev = state_in[2][bi, h, 0:1, 0:1]
            mt_c = bc_c + jnp.maximum(m_prev, cm_c)
            dmat = jnp.exp(jnp.where(tri, (bc_c - mt_c) + x_r, -jnp.inf))
            inter = jnp.exp(bc_c + m_prev - mt_c)
            q = q_ref[bi, sl, lanes]
            k = k_ref[bi, sl, lanes]
            c_old = state_in[0][bi, h]
            part.append(dict(lanes=lanes, li_c=li_c, bc_c=bc_c, mt_c=mt_c, m_prev=m_prev, dmat=dmat,
                             inter=inter, q=q, k=k, c_old=c_old, n_old=state_in[1][bi, h],
                             qk=_dot_nt(q, k), qc=_dot(q, c_old.astype(BF16))))

        mix = [None] * gb
        for (bi, h), p in zip(chains, part):
            v = v_ref[bi, sl, p["lanes"]]
            sqk = p["qk"] * p["dmat"]
            num = p["inter"] * p["qc"] + _dot(sqk.astype(BF16), v)
            den = (p["inter"] * jnp.sum(p["q"].astype(F32) * p["n_old"], axis=1, keepdims=True)
                   + jnp.sum(sqk, axis=1, keepdims=True))
            h_til = num / jnp.maximum(jnp.abs(den), jnp.exp(-p["mt_c"]))
            hcell = jax.nn.sigmoid(o_ref[bi, sl, p["lanes"]]) * h_til
            hc = hcell - jnp.mean(hcell, axis=-1, keepdims=True)
            hn = hc * lax.rsqrt(jnp.mean(hc * hc, axis=-1, keepdims=True) + EPS) * mhg_ref[:, p["lanes"]]
            if fuse_out:
                part_mix = _dot(hn.astype(BF16), w_ref[p["lanes"], :])
                mix[bi] = part_mix if mix[bi] is None else mix[bi] + part_mix
            else:
                h_ref[bi, sl, p["lanes"]] = hn.astype(h_ref.dtype)
        if fuse_out:
            for bi in range(gb):
                h_ref[bi, sl, :] = r_ref[bi, sl, :] + _rms(mix[bi], g_ref[...])

        for (bi, h), p in zip(chains, part):
            v = v_ref[bi, sl, p["lanes"]]
            m_end = p["mt_c"][L - 1:L, :]
            bc_end = p["bc_c"][L - 1:L, :]
            w_end = jnp.exp(bc_end - p["bc_c"] + p["li_c"] - m_end)
            decay = jnp.exp(bc_end + p["m_prev"] - m_end)
            wk = w_end * p["k"].astype(F32)
            c_out[bi, h] = decay * p["c_old"] + _dot_tn(wk.astype(BF16), v)
            n_out[bi, h] = decay * p["n_old"] + jnp.sum(wk, axis=0, keepdims=True)
            m_out[bi, h] = jnp.broadcast_to(m_end, m_out.shape[2:])


def _mlstm(q, k, v, o, gt, c0, n0, m0, mh_g, resid, w_out, g_post, nh, chunk, rows_per_step):
    b, s, d = q.shape
    dh = d // nh
    gb = min(MLSTM_SEQS, b)
    assert rows_per_step % chunk == 0 and s % rows_per_step == 0 and b % gb == 0
    fuse_out = chunk >= LANES
    blk = lambda width: pl.BlockSpec((gb, rows_per_step, width), lambda bi, si: (bi, si, 0))
    st = lambda r, w: pl.BlockSpec((gb, nh, r, w), lambda bi, si: (bi, 0, 0, 0))
    vec = pl.BlockSpec((1, d), lambda bi, si: (0, 0))
    in_specs = [blk(d), blk(d), blk(d), blk(d), blk(LANES), st(dh, dh), st(1, dh), st(SUBLANES, LANES), vec]
    args = [q, k, v, o, gt, c0, n0, m0, mh_g]
    if fuse_out:
        in_specs += [blk(d), pl.BlockSpec((d, d), lambda bi, si: (0, 0)), vec]
        args += [resid, w_out, g_post]
    out = pl.pallas_call(
        functools.partial(_mlstm_kernel, chunk=chunk, nchunks=rows_per_step // chunk, nh=nh,
                          single_step=(s == rows_per_step), fuse_out=fuse_out),
        grid=(b // gb, s // rows_per_step),
        in_specs=in_specs,
        out_specs=[blk(d), st(dh, dh), st(1, dh), st(SUBLANES, LANES)],
        out_shape=[jax.ShapeDtypeStruct((b, s, d), F32 if fuse_out else BF16),
                   jax.ShapeDtypeStruct((b, nh, dh, dh), F32),
                   jax.ShapeDtypeStruct((b, nh, 1, dh), F32),
                   jax.ShapeDtypeStruct((b, nh, SUBLANES, LANES), F32)],
        compiler_params=_params("parallel", "arbitrary"),
        name="mlstm",
    )(*args)
    if fuse_out:
        return out
    h = _outproj(out[0].reshape(b * s, d), resid.reshape(b * s, d), w_out, g_post).reshape(b, s, d)
    return (h,) + tuple(out[1:])


def _trunk(x, state_conv, cache_k, cache_v, page_table, state_c, state_n, state_m, wts):
    b, s, d = x.shape
    m = b * s
    depth = wts["norms"].shape[0]
    cw = wts["conv_w0"].shape[2]
    dh = wts["lam0"].shape[2]
    aw = (wts["w_in0"].shape[2] - 3 * cw) // 3
    nh_a = aw // (2 * dh)
    nh_m = state_c.shape[2]
    dh_m = d // nh_m
    h = x.reshape(m, d)
    ks, vs, convs, cs, ns, ms = [], [], [], [], [], []
    for layer in range(depth):
        j = layer // 2
        g = wts["norms"][layer]
        if layer % 2 == 0:
            lam_init = _lambda_init(layer)
            prompt = cache_k is None
            proj = _inproj0(h, g[0:1], wts["w_in0"][j], wts["w_in0_qv_t"][j], cw, aw,
                            dh ** -0.5 * LOG2_E, prompt)
            gb, cx, k, v = proj[:4]
            sg = wts["subln_g0"][j].reshape(1, 2 * dh)
            if prompt:
                kb, qt, vt = proj[4:]
                attn = _attn_prompt(qt, kb.reshape(b, s, aw), vt, wts["lam0"][j], sg, b, s, nh_a, dh,
                                    lam_init).reshape(m, aw)
            else:
                attn = _attn_decode(proj[4], k, v, cache_k, cache_v, j, page_table, wts["lam0"][j], sg,
                                    b, s, nh_a, dh, lam_init)
            h = _outproj_conv_ffn(gb, cx, attn, h, wts["w_out0"][j], g[1:2], wts["conv_w0"][j],
                                  state_conv[j], s, g[2:3], wts["w_gu"], wts["w_down"], layer, g[3:4])
            ks.append(k.reshape(b, s, nh_a, 2 * dh))
            vs.append(v.reshape(b, s, nh_a, 2 * dh))
            convs.append(cx.reshape(b, s, cw)[:, s - 2:, :])
        else:
            q, k, v, o, gt = _inproj1(h, g[0:1], wts["w_in1"][j], wts["w_in1_gates"][j], wts["b_if1"][j],
                                      nh_m, dh_m ** -0.5)
            chunk = MLSTM_CHUNK if s % MLSTM_CHUNK == 0 else s
            sp = s
            r3 = lambda a: a.reshape(b, s, a.shape[1])
            q, k, v, o, gt, resid = r3(q), r3(k), r3(v), r3(o), r3(gt), r3(h)
            if chunk % SUBLANES != 0:
                sp = -(-s // SUBLANES) * SUBLANES
                chunk = sp
                pad = lambda a: jnp.pad(a, ((0, 0), (0, sp - s), (0, 0)))
                lane = lax.broadcasted_iota(jnp.int32, (b, sp - s, LANES), 2)
                gt_pad = jnp.where(lane < nh_m, -jnp.inf, 0.0).astype(F32)
                q, k, v, o, resid = pad(q), pad(k), pad(v), pad(o), pad(resid)
                gt = jnp.concatenate([gt, gt_pad], axis=1)
            rows = min(MLSTM_ROWS, sp)
            m0 = jnp.broadcast_to(state_m[j][:, :, None, None], (b, nh_m, SUBLANES, LANES))
            h3, c_new, n_new, m_new = _mlstm(q, k, v, o, gt, state_c[j], state_n[j][:, :, None, :], m0,
                                             wts["mh_g1"][j].reshape(1, d), resid, wts["w_out1"][j], g[1:2],
                                             nh_m, chunk, rows)
            h = h3[:, :s].reshape(m, d)
            cs.append(c_new)
            ns.append(n_new[:, :, 0, :])
            ms.append(m_new[:, :, 0, 0])
            h = _ffn(h, g[2:3], wts["w_gu"], wts["w_down"], layer, g[3:4])
    return (h.reshape(b, s, d), jnp.stack(ks), jnp.stack(vs), jnp.stack(convs),
            jnp.stack(cs), jnp.stack(ns), jnp.stack(ms))


def _qv_transposed(w_in0, cw):
    aw = (w_in0.shape[2] - 3 * cw) // 3
    wq = w_in0[:, :, 3 * cw:3 * cw + aw]
    wv = w_in0[:, :, 3 * cw + 2 * aw:]
    return jnp.swapaxes(jnp.concatenate([wq, wv], axis=2), 1, 2).astype(BF16)


def kernel(x_prompt, x_sample, cache_k, cache_v, state_conv, state_C, state_n, state_m, page_table,
           norms, w_in0, conv_w0, lam0, subln_g0, w_out0, w_in1, b_if1, mh_g1, w_out1, w_gu, w_down):
    d = x_prompt.shape[-1]
    nh_m = state_C.shape[2]
    n_odd = w_in1.shape[0]
    w_gates = jnp.pad(w_in1[:, :, 4 * d:], ((0, 0), (0, 0), (0, LANES - 2 * nh_m))).astype(BF16)
    b_pad = jnp.pad(b_if1.astype(F32), ((0, 0), (0, LANES - 2 * nh_m))).reshape(n_odd, 1, LANES)
    wts = dict(
        norms=norms.astype(F32), w_in0=w_in0.astype(BF16), conv_w0=conv_w0, lam0=lam0.astype(F32),
        w_in0_qv_t=_qv_transposed(w_in0, conv_w0.shape[2]),
        subln_g0=subln_g0.astype(F32), w_out0=w_out0.astype(BF16), w_in1=w_in1.astype(BF16),
        w_in1_gates=w_gates, b_if1=b_pad, mh_g1=mh_g1.astype(F32), w_out1=w_out1.astype(BF16),
        w_gu=w_gu.astype(BF16), w_down=w_down.astype(BF16))
    bp = x_prompt.shape[0]
    n_even = state_conv.shape[0]
    conv0 = jnp.zeros((n_even, bp) + state_conv.shape[2:], x_prompt.dtype)
    c0 = jnp.zeros((n_odd, bp) + state_C.shape[2:], F32)
    n0 = jnp.zeros((n_odd, bp) + state_n.shape[2:], F32)
    m0 = jnp.zeros((n_odd, bp) + state_m.shape[2:], F32)
    y_p, k_p, v_p, conv_p, c_p, n_p, m_p = _trunk(x_prompt, conv0, None, None, None, c0, n0, m0, wts)
    y_s, k_s, v_s, conv_s, c_s, n_s, m_s = _trunk(x_sample, state_conv, cache_k, cache_v, page_table,
                                                  state_C.astype(F32), state_n.astype(F32),
                                                  state_m.astype(F32), wts)
    return (y_p, y_s, k_p, v_p, conv_p, c_p, n_p, m_p, k_s, v_s, conv_s, c_s, n_s, m_s)
```

```python
import functools
import math

import jax
import jax.numpy as jnp
from jax import lax
from jax.experimental import pallas as pl
from jax.experimental.pallas import tpu as pltpu

EPS = 1e-6
LOG2_E = math.log2(math.e)
F32 = jnp.float32
BF16 = jnp.bfloat16

V7X_VMEM_BYTES = 64 * 1024 * 1024
VMEM_LIMIT_BYTES = V7X_VMEM_BYTES - 8 * 1024 * 1024
LANES = 128
SUBLANES = 8

ROW_TILE = 512
FFN_ROW_TILE = 512
FFN_COL_TILE = 256
ATTN_Q_TILE = 256
ATTN_K_TILE = 1024
DECODE_PAGES = 16
DECODE_SLOTS = 3
MLSTM_CHUNK = 512
MLSTM_ROWS = 512
MLSTM_SEQS = 1


def _params(*semantics):
    return pltpu.CompilerParams(dimension_semantics=semantics, vmem_limit_bytes=VMEM_LIMIT_BYTES)


def _dot(a, b):
    return jnp.dot(a, b, preferred_element_type=F32)


def _dot_nt(a, b):
    return lax.dot_general(a, b, (((1,), (1,)), ((), ())), preferred_element_type=F32)


def _dot_tn(a, b):
    return lax.dot_general(a, b, (((0,), (0,)), ((), ())), preferred_element_type=F32)


def _rms(x, g):
    return x * lax.rsqrt(jnp.mean(x * x, axis=-1, keepdims=True) + EPS) * g


def _lambda_init(layer):
    return 0.8 - 0.6 * math.exp(-0.3 * layer)


def _lam_value(lam_ref, lam_init):
    lv = lam_ref[...]
    a = jnp.sum(lv[0:1] * lv[1:2], axis=-1, keepdims=True)
    b = jnp.sum(lv[2:3] * lv[3:4], axis=-1, keepdims=True)
    return jnp.exp(a) - jnp.exp(b) + lam_init


def _row_tile(m, want):
    t = min(want, m)
    assert m % t == 0, (m, t)
    return t


def _inproj0_kernel(x_ref, g_ref, w_ref, *rest, cw, aw, qscale, transposed):
    xn = _rms(x_ref[...], g_ref[...]).astype(BF16)

    def proj(lo, width):
        return _dot(xn, w_ref[:, lo:lo + width])

    if transposed:
        wt_ref, gb_ref, cx_ref, k_ref, v_ref, kb_ref, qt_ref, vt_ref = rest
        qt_ref[...] = (_dot_nt(wt_ref[0:aw, :], xn) * qscale).astype(BF16)
        vt_ref[...] = _dot_nt(wt_ref[aw:2 * aw, :], xn).astype(BF16)
    else:
        gb_ref, cx_ref, k_ref, v_ref, q_ref = rest
        q_ref[...] = (proj(3 * cw, aw) * qscale).astype(BF16)
    gb_ref[...] = proj(0, cw)
    cx_ref[...] = proj(cw, cw) * proj(2 * cw, cw)
    k = proj(3 * cw + aw, aw)
    v = proj(3 * cw + 2 * aw, aw)
    tm = k.shape[0]
    nh = aw // LANES
    for h in range(nh):
        k_ref[pl.ds(h, tm, stride=nh), :] = k[:, h * LANES:(h + 1) * LANES]
        v_ref[pl.ds(h, tm, stride=nh), :] = v[:, h * LANES:(h + 1) * LANES]
    if transposed:
        kb_ref[...] = k.astype(BF16)


def _inproj0(x, g, w, wt, cw, aw, qscale, transposed):
    m, d = x.shape
    tm = _row_tile(m, ROW_TILE)
    n = w.shape[1]
    row = lambda width: pl.BlockSpec((tm, width), lambda i: (i, 0))
    const = lambda shape: pl.BlockSpec(shape, lambda i: (0,) * len(shape))
    nh = aw // LANES
    kv_spec = pl.BlockSpec((tm * nh, LANES), lambda i: (i, 0))
    kv_shape = jax.ShapeDtypeStruct((m * nh, LANES), F32)
    out_specs = [row(cw), row(cw), kv_spec, kv_spec, row(aw)]
    out_shape = [jax.ShapeDtypeStruct((m, cw), F32), jax.ShapeDtypeStruct((m, cw), F32), kv_shape, kv_shape,
                 jax.ShapeDtypeStruct((m, aw), BF16)]
    in_specs = [row(d), const((1, d)), const((d, n))]
    args = [x, g, w]
    if transposed:
        in_specs.append(const(wt.shape))
        args.append(wt)
        out_specs += [pl.BlockSpec((aw, tm), lambda i: (0, i))] * 2
        out_shape += [jax.ShapeDtypeStruct((aw, m), BF16)] * 2
    return pl.pallas_call(
        functools.partial(_inproj0_kernel, cw=cw, aw=aw, qscale=qscale, transposed=transposed),
        grid=(m // tm,),
        in_specs=in_specs,
        out_specs=out_specs,
        out_shape=out_shape,
        compiler_params=_params("parallel"),
        name="inproj0",
    )(*args)


def _attn_prompt_kernel(lam_ref, sg_ref, qt_ref, k_ref, vt_ref, o_ref, *, tq, tk, dh, lam_init):
    nch = tk // tq
    g = pl.program_id(2)
    qt = qt_ref[...]
    sub = lax.broadcasted_iota(jnp.int32, (2 * dh, tq), 0)
    zero = jnp.zeros((2 * dh, tq), qt.dtype)
    qqs = []
    for c in range(nch):
        qc = qt[:, c * tq:(c + 1) * tq]
        qqs.append(jnp.concatenate([jnp.where(sub < dh, qc, zero), jnp.where(sub >= dh, qc, zero)], axis=1))
    cols = 2 * tq

    def run(blocks, carry):
        scores, vts, nks = [], [], []
        for j, masked in blocks:
            off = pl.multiple_of(j * tk, tk)
            k = k_ref[pl.ds(off, tk), :]
            vts.append(vt_ref[:, pl.ds(off, tk)])
            nks.append([(c + 1) * tq if masked else tk for c in range(nch)])
            scores.append([_dot(k[:nks[-1][c]], qqs[c]) for c in range(nch)])
        for (j, masked), sc, vt, nk in zip(blocks, scores, vts, nks):
            probs, stats = [], []
            for c in range(nch):
                m, l, _ = carry[c]
                s = sc[c]
                if masked:
                    r = lax.broadcasted_iota(jnp.int32, (nk[c], cols), 0)
                    cc = lax.broadcasted_iota(jnp.int32, (nk[c], cols), 1)
                    s = jnp.where(r <= c * tq + jnp.where(cc >= tq, cc - tq, cc), s, -jnp.inf)
                m_new = jnp.maximum(m, jnp.max(s, axis=0, keepdims=True))
                alpha = jnp.exp2(m - m_new)
                p = jnp.exp2(s - m_new)
                stats.append((m_new, alpha, alpha * l + jnp.sum(p, axis=0, keepdims=True)))
                probs.append(p.astype(BF16))
            out = []
            for c in range(nch):
                m_new, alpha, l = stats[c]
                acc = alpha * carry[c][2] + _dot(vt[:, :nk[c]], probs[c])
                out.append((m_new, l, acc))
            carry = tuple(out)
        return carry

    init = tuple((jnp.full((1, cols), -jnp.inf, F32), jnp.zeros((1, cols), F32),
                  jnp.zeros((2 * dh, cols), F32)) for _ in range(nch))
    carry = lax.fori_loop(0, g // 2, lambda jj, c: run([(2 * jj, False), (2 * jj + 1, False)], c), init)
    carry = lax.cond(g % 2 == 1,
                     lambda c: run([(g - 1, False), (g, True)], c),
                     lambda c: run([(g, True)], c), carry)

    lam = _lam_value(lam_ref, lam_init)
    for c in range(nch):
        _, l, acc = carry[c]
        o = acc / l
        a = o[:, :tq] - lam * o[:, tq:]
        y = a * lax.rsqrt(jnp.mean(a * a, axis=0, keepdims=True) + EPS) * sg_ref[...] * (1.0 - lam_init)
        o_ref[c * tq:(c + 1) * tq, :] = jnp.transpose(y).astype(o_ref.dtype)


def _attn_prompt(qt, kb, vt, lam, subln_g, b, s, nh, dh, lam_init):
    tk = _row_tile(s, ATTN_K_TILE)
    tq = _row_tile(tk, ATTN_Q_TILE)
    hw = 2 * dh
    assert hw == LANES
    nq = s // tk
    return pl.pallas_call(
        functools.partial(_attn_prompt_kernel, tq=tq, tk=tk, dh=dh, lam_init=lam_init),
        grid=(b, nh, nq),
        in_specs=[pl.BlockSpec(lam.shape, lambda bi, h, qi: (0, 0)),
                  pl.BlockSpec((hw, 1), lambda bi, h, qi: (0, 0)),
                  pl.BlockSpec((hw, tk), lambda bi, h, qi: (h, bi * nq + qi)),
                  pl.BlockSpec((None, s, hw), lambda bi, h, qi: (bi, 0, h)),
                  pl.BlockSpec((hw, s), lambda bi, h, qi: (h, bi))],
        out_specs=pl.BlockSpec((None, tk, hw), lambda bi, h, qi: (bi, qi, h)),
        out_shape=jax.ShapeDtypeStruct((b, s, nh * hw), BF16),
        compiler_params=_params("parallel", "parallel", "arbitrary"),
        name="attn_prompt",
    )(lam, subln_g.reshape(hw, 1), qt, kb, vt)


def _attn_decode_kernel(pt_ref, lam_ref, sg_ref, q_ref, kn_ref, vn_ref, ck_ref, cv_ref, o_ref,
                        kbuf, vbuf, sem, m_ref, l_ref, acc_ref, *, pages, layer_idx, nh, nq, lam_init):
    b = pl.program_id(0)
    step = pl.program_id(1)
    n_steps = pl.num_programs(1)
    total = pl.num_programs(0) * n_steps
    n_slots = kbuf.shape[0]
    ahead = n_slots - 1
    n = b * n_steps + step
    slot = n % n_slots
    q = q_ref[...]
    rows = q.shape[0]
    hw = q.shape[1]
    prow = kbuf.shape[2]

    def page_reads(t, r, sl):
        pg = pt_ref[t // n_steps, (t % n_steps) * pages + r]
        return (pltpu.make_async_copy(ck_ref.at[layer_idx, pg], kbuf.at[sl, r], sem.at[0, sl]),
                pltpu.make_async_copy(cv_ref.at[layer_idx, pg], vbuf.at[sl, r], sem.at[1, sl]))

    @pl.when(n == 0)
    def _():
        for t in range(ahead):
            for r in range(pages):
                for cp in page_reads(t, r, t):
                    cp.start()

    fetch = jnp.where(n + ahead < total, n + ahead, n)
    fetch_slot = (n + ahead) % n_slots

    for r in range(pages):
        for cp in page_reads(n, r, slot):
            cp.wait()

    def masks(width):
        r = lax.broadcasted_iota(jnp.int32, (rows, width), 0)
        c = lax.broadcasted_iota(jnp.int32, (rows, width), 1)
        return r, c, (c % nh) == (r // (2 * nq))

    def update(scores, values):
        m_old = m_ref[...]
        m_new = m_old
        for s in scores:
            m_new = jnp.maximum(m_new, jnp.max(s, axis=-1, keepdims=True))
        alpha = jnp.exp2(m_old - m_new)
        l = alpha * l_ref[...]
        acc = alpha * acc_ref[...]
        for s, v in zip(scores, values):
            p = jnp.exp2(s - m_new)
            l = l + jnp.sum(p, axis=-1, keepdims=True)
            acc = acc + _dot(p.astype(BF16), v)
        m_ref[...] = m_new
        l_ref[...] = l
        acc_ref[...] = acc

    @pl.when(step == 0)
    def _():
        m_ref[...] = jnp.full(m_ref.shape, -jnp.inf, F32)
        l_ref[...] = jnp.zeros(l_ref.shape, F32)
        acc_ref[...] = jnp.zeros(acc_ref.shape, F32)

    _, _, head_ok = masks(prow)
    scores = []
    for r in range(pages):
        for cp in page_reads(fetch, r, fetch_slot):
            cp.start()
        scores.append(jnp.where(head_ok, _dot_nt(q, kbuf[slot, r].astype(BF16)), -jnp.inf))
    update(scores, [vbuf[slot, r].astype(BF16) for r in range(pages)])

    @pl.when(n == total - 1)
    def _():
        for d in range(1, ahead + 1):
            for r in range(pages):
                for cp in page_reads(n, r, (n + d) % n_slots):
                    cp.wait()

    @pl.when(step == n_steps - 1)
    def _():
        width = kn_ref.shape[0]
        r, c, ok = masks(width)
        ok = ok & ((c // nh) <= (r % nq)) & (c < nq * nh)
        s = jnp.where(ok, _dot_nt(q, kn_ref[...].astype(BF16)), -jnp.inf)
        update([s], [vn_ref[...].astype(BF16)])
        o = acc_ref[...] / l_ref[...]
        a = o - _lam_value(lam_ref, lam_init) * pltpu.roll(o, rows - nq, 0)
        y = _rms(a, sg_ref[...]) * (1.0 - lam_init)
        for h in range(nh):
            o_ref[:, h * hw:(h + 1) * hw] = y[h * 2 * nq:h * 2 * nq + nq].astype(o_ref.dtype)


def _attn_decode(qb, k_new, v_new, cache_k, cache_v, layer_idx, page_table, lam, subln_g,
                 nb, nq, nh, dh, lam_init):
    hw = 2 * dh
    assert 2 * nq == SUBLANES and hw == LANES and nq * nh <= LANES
    n_pages = page_table.shape[1]
    page = cache_k.shape[2]
    pages = min(DECODE_PAGES, n_pages)
    assert n_pages % pages == 0
    rows = nh * 2 * nq
    qh = jnp.tile(qb.reshape(nb, nq, nh, hw).transpose(0, 2, 1, 3), (1, 1, 2, 1)).reshape(nb, rows, hw)
    r = lax.broadcasted_iota(jnp.int32, (rows, hw), 0)
    c = lax.broadcasted_iota(jnp.int32, (rows, hw), 1)
    qbd = jnp.where(((r // nq) % 2) == (c // dh), qh, jnp.zeros_like(qh))
    pad_new = lambda a: jnp.pad(a.reshape(nb, nq * nh, hw), ((0, 0), (0, LANES - nq * nh), (0, 0)))
    view = lambda cache: cache.reshape(cache.shape[0], cache.shape[1], page * nh, hw)

    const = lambda shape: pl.BlockSpec(shape, lambda b, p, pt: (0,) * len(shape))
    per_b = lambda shape: pl.BlockSpec((None,) + shape, lambda b, p, pt: (b,) + (0,) * len(shape))
    in_hbm = pl.BlockSpec(memory_space=pl.ANY)
    assert nb * (n_pages // pages) >= DECODE_SLOTS
    page_buf = pltpu.VMEM((DECODE_SLOTS, pages, page * nh, hw), cache_k.dtype)
    grid_spec = pltpu.PrefetchScalarGridSpec(
        num_scalar_prefetch=1,
        grid=(nb, n_pages // pages),
        in_specs=[const(lam.shape), const((1, hw)), per_b((rows, hw)),
                  per_b((LANES, hw)), per_b((LANES, hw)), in_hbm, in_hbm],
        out_specs=per_b((nq, nh * hw)),
        scratch_shapes=[page_buf, page_buf, pltpu.SemaphoreType.DMA((2, DECODE_SLOTS)),
                        pltpu.VMEM((rows, 1), F32), pltpu.VMEM((rows, 1), F32),
                        pltpu.VMEM((rows, hw), F32)],
    )
    out = pl.pallas_call(
        functools.partial(_attn_decode_kernel, pages=pages, layer_idx=layer_idx, nh=nh, nq=nq,
                          lam_init=lam_init),
        grid_spec=grid_spec,
        out_shape=jax.ShapeDtypeStruct((nb, nq, nh * hw), F32),
        compiler_params=_params("arbitrary", "arbitrary"),
        name="attn_decode",
    )(page_table, lam, subln_g, qbd, pad_new(k_new), pad_new(v_new), view(cache_k), view(cache_v))
    return out.reshape(nb * nq, nh * hw)


def _conv_taps(cx, prev1, prev2, cw_ref):
    return cw_ref[0:1, :] * prev2 + cw_ref[1:2, :] * prev1 + cw_ref[2:3, :] * cx


def _ffn_rows(x, g_in_ref, wgu_ref, wd_ref, g_out_ref, act_ref, tf):
    xn = _rms(x, g_in_ref[...]).astype(BF16)
    dff = wd_ref.shape[0]
    for lo in range(0, dff, tf):
        gate = _dot(xn, wgu_ref[:, lo:lo + tf])
        up = _dot(xn, wgu_ref[:, dff + lo:dff + lo + tf])
        act_ref[:, lo:lo + tf] = (gate * jax.nn.sigmoid(gate) * up).astype(BF16)
    return x + _rms(_dot(act_ref[...], wd_ref[...]), g_out_ref[...])


def _outproj_conv_kernel(gb_ref, cx_ref, at_ref, r_ref, w_ref, g_ref, cw_ref,
                         g_in_ref, wgu_ref, wd_ref, g_out_ref, *rest, seq, carry_mode, tf):
    cx = cx_ref[...]
    tm, cwid = cx.shape
    row = lax.broadcasted_iota(jnp.int32, (tm, 1), 0)
    roll1 = pltpu.roll(cx, 1, 0)
    roll2 = pltpu.roll(cx, 2, 0)
    if carry_mode:
        st_ref, o_ref, act_ref, carry = rest

        @pl.when((pl.program_id(0) * tm) % seq == 0)
        def _():
            carry[0:2, :] = st_ref[...]

        c0 = carry[0:1, :]
        c1 = carry[1:2, :]
        prev1 = jnp.where(row >= 1, roll1, c1)
        prev2 = jnp.where(row >= 2, roll2, jnp.where(row == 1, c1, c0))
        carry[0:2, :] = cx[tm - 2:tm, :]
    else:
        e1_ref, e2_ref, o_ref, act_ref = rest
        t = row % seq
        prev1 = jnp.where(t >= 1, roll1, e1_ref[...])
        prev2 = jnp.where(t >= 2, roll2, e2_ref[...])
    yconv = (gb_ref[...] * _conv_taps(cx, prev1, prev2, cw_ref)).astype(BF16)
    mix = _dot(yconv, w_ref[0:cwid, :]) + _dot(at_ref[...].astype(BF16), w_ref[cwid:, :])
    h = r_ref[...] + _rms(mix, g_ref[...])
    o_ref[...] = _ffn_rows(h, g_in_ref, wgu_ref, wd_ref, g_out_ref, act_ref, tf)


def _outproj_conv_ffn(gb, cx, attn, resid, w, g, conv_w, state, seq, g_in, w_gu, w_down, layer, g_out):
    m, d = resid.shape
    cwid = gb.shape[1]
    dff = w_down.shape[1]
    tm = _row_tile(m, FFN_ROW_TILE)
    tf = _row_tile(dff, FFN_COL_TILE)
    carry_mode = seq % tm == 0
    row = lambda width: pl.BlockSpec((tm, width), lambda i: (i, 0))
    const = lambda shape: pl.BlockSpec(shape, lambda i: (0,) * len(shape), pipeline_mode=pl.Buffered(1))
    resident = lambda shape: pl.BlockSpec((None,) + shape, lambda i: (layer, 0, 0),
                                          pipeline_mode=pl.Buffered(1))
    in_specs = [row(cwid), row(cwid), row(attn.shape[1]), row(d), const(w.shape), const((1, d)),
                const(conv_w.shape), const((1, d)), resident((d, 2 * dff)), resident((dff, d)), const((1, d))]
    scratch = [pltpu.VMEM((tm, dff), BF16)]
    if carry_mode:
        extra = [state]
        in_specs += [pl.BlockSpec((None, 2, cwid), lambda i: ((i * tm) // seq, 0, 0))]
        scratch += [pltpu.VMEM((SUBLANES, cwid), F32)]
    else:
        assert tm % seq == 0 and seq >= 2
        nb = m // seq
        zeros = jnp.zeros((nb, seq - 1, cwid), F32)
        e1 = jnp.concatenate([state[:, 1:2], zeros], axis=1).reshape(m, cwid)
        e2 = jnp.concatenate([state[:, 0:1], state[:, 1:2], zeros[:, 1:]], axis=1).reshape(m, cwid)
        extra = [e1, e2]
        in_specs += [row(cwid), row(cwid)]
    return pl.pallas_call(
        functools.partial(_outproj_conv_kernel, seq=seq, carry_mode=carry_mode, tf=tf),
        grid=(m // tm,),
        in_specs=in_specs,
        out_specs=row(d),
        out_shape=jax.ShapeDtypeStruct((m, d), F32),
        scratch_shapes=scratch,
        compiler_params=_params("arbitrary"),
        name="outproj_conv_ffn",
    )(gb, cx, attn, resid, w, g, conv_w, g_in, w_gu, w_down, g_out, *extra)


def _outproj_kernel(x_ref, r_ref, w_ref, g_ref, o_ref):
    o_ref[...] = r_ref[...] + _rms(_dot(x_ref[...], w_ref[...]), g_ref[...])


def _outproj(x, resid, w, g):
    m, d = resid.shape
    tm = _row_tile(m, ROW_TILE)
    row = lambda width: pl.BlockSpec((tm, width), lambda i: (i, 0))
    return pl.pallas_call(
        _outproj_kernel,
        grid=(m // tm,),
        in_specs=[row(x.shape[1]), row(d), pl.BlockSpec(w.shape, lambda i: (0, 0)),
                  pl.BlockSpec((1, d), lambda i: (0, 0))],
        out_specs=row(d),
        out_shape=jax.ShapeDtypeStruct((m, d), F32),
        compiler_params=_params("parallel"),
        name="outproj",
    )(x, resid, w, g)


def _ffn_kernel(x_ref, g_in_ref, wgu_ref, wd_ref, g_out_ref, o_ref, act_ref, *, tf):
    o_ref[...] = _ffn_rows(x_ref[...], g_in_ref, wgu_ref, wd_ref, g_out_ref, act_ref, tf)


def _ffn(x, g_in, w_gu, w_down, layer, g_out):
    m, d = x.shape
    dff = w_down.shape[1]
    tm = _row_tile(m, FFN_ROW_TILE)
    tf = _row_tile(dff, FFN_COL_TILE)
    row = pl.BlockSpec((tm, d), lambda i: (i, 0))
    vec = pl.BlockSpec((1, d), lambda i: (0, 0))
    resident = lambda shape: pl.BlockSpec((None,) + shape, lambda i: (layer, 0, 0),
                                          pipeline_mode=pl.Buffered(1))
    return pl.pallas_call(
        functools.partial(_ffn_kernel, tf=tf),
        grid=(m // tm,),
        in_specs=[row, vec, resident((d, 2 * dff)), resident((dff, d)), vec],
        out_specs=row,
        out_shape=jax.ShapeDtypeStruct((m, d), F32),
        scratch_shapes=[pltpu.VMEM((tm, dff), BF16)],
        compiler_params=_params("parallel"),
        name="ffn",
    )(x, g_in, w_gu, w_down, g_out)


def _inproj1_kernel(x_ref, g_ref, w_ref, wg_ref, b_ref, q_ref, k_ref, v_ref, o_ref, gt_ref,
                    *, d, nh, kscale):
    xn = _rms(x_ref[...], g_ref[...]).astype(BF16)
    q_ref[...] = _dot(xn, w_ref[:, 0:d]).astype(BF16)
    k_ref[...] = (_dot(xn, w_ref[:, d:2 * d]) * kscale).astype(BF16)
    v_ref[...] = _dot(xn, w_ref[:, 2 * d:3 * d]).astype(BF16)
    o_ref[...] = _dot(xn, w_ref[:, 3 * d:4 * d])
    gates = _dot(xn, wg_ref[...]) + b_ref[...]
    lane = lax.broadcasted_iota(jnp.int32, gates.shape, 1)
    gt_ref[...] = jnp.where(lane < nh, gates, jax.nn.log_sigmoid(gates))


def _inproj1(x, g, w, wg, b, nh, kscale):
    m, d = x.shape
    tm = _row_tile(m, ROW_TILE)
    row = lambda width: pl.BlockSpec((tm, width), lambda i: (i, 0))
    const = lambda shape: pl.BlockSpec(shape, lambda i: (0,) * len(shape))
    shapes = [(d, BF16), (d, BF16), (d, BF16), (d, F32), (LANES, F32)]
    return pl.pallas_call(
        functools.partial(_inproj1_kernel, d=d, nh=nh, kscale=kscale),
        grid=(m // tm,),
        in_specs=[row(d), const((1, d)), const(w.shape), const(wg.shape), const((1, LANES))],
        out_specs=[row(wd) for wd, _ in shapes],
        out_shape=[jax.ShapeDtypeStruct((m, wd), dt) for wd, dt in shapes],
        compiler_params=_params("parallel"),
        name="inproj1",
    )(x, g, w, wg, b)


def _gate_vectors(gt, nh, tri, eye, row, col):
    L = gt.shape[0]
    if L % LANES == 0:
        lane = lax.broadcasted_iota(jnp.int32, gt.shape, 1)
        lf = jnp.where((lane >= nh) & (lane < 2 * nh), gt, 0.0)
        ones = jnp.where(tri, 1.0, 0.0).astype(BF16)
        hi = lf.astype(BF16)
        rest = lf - hi.astype(F32)
        mid = rest.astype(BF16)
        lo = (rest - mid.astype(F32)).astype(BF16)
        bc = _dot(ones, hi) + _dot(ones, mid) + _dot(ones, lo)
        gt_t = jnp.transpose(gt)
        bc_t = jnp.transpose(bc)
        li_c = [gt[:, h:h + 1] for h in range(nh)]
        li_r = [gt_t[h:h + 1, :] for h in range(nh)]
        bc_c = [bc[:, nh + h:nh + h + 1] for h in range(nh)]
        bc_r = [bc_t[nh + h:nh + h + 1, :] for h in range(nh)]
        return li_c, li_r, bc_c, bc_r
    li_c, li_r, bc_c, bc_r = [], [], [], []
    for h in range(nh):
        li = gt[:, h:h + 1]
        lf = gt[:, nh + h:nh + h + 1]
        lf_r = jnp.sum(jnp.where(eye, lf, 0.0), axis=0, keepdims=True)
        li_c.append(li)
        li_r.append(jnp.sum(jnp.where(eye, li, 0.0), axis=0, keepdims=True))
        bc_c.append(jnp.sum(jnp.where(tri, lf_r, 0.0), axis=1, keepdims=True))
        bc_r.append(jnp.sum(jnp.where(row <= col, lf, 0.0), axis=0, keepdims=True))
    return li_c, li_r, bc_c, bc_r


def _mlstm_kernel(q_ref, k_ref, v_ref, o_ref, gt_ref, c0_ref, n0_ref, m0_ref, mhg_ref, *rest,
                  chunk, nchunks, nh, single_step, fuse_out):
    if fuse_out:
        r_ref, w_ref, g_ref, h_ref, c_out, n_out, m_out = rest
    else:
        h_ref, c_out, n_out, m_out = rest
    gb = q_ref.shape[0]
    dh = q_ref.shape[2] // nh
    chains = [(bi, h) for bi in range(gb) for h in range(nh)]

    if not single_step:
        @pl.when(pl.program_id(1) == 0)
        def _():
            c_out[...] = c0_ref[...]
            n_out[...] = n0_ref[...]
            m_out[...] = m0_ref[...]

    L = chunk
    row = lax.broadcasted_iota(jnp.int32, (L, L), 0)
    col = lax.broadcasted_iota(jnp.int32, (L, L), 1)
    tri = col <= row
    eye = col == row

    for c in range(nchunks):
        sl = pl.ds(c * L, L)
        state_in = (c0_ref, n0_ref, m0_ref) if (single_step and c == 0) else (c_out, n_out, m_out)
        part = []
        gates = {}
        for bi, h in chains:
            lanes = slice(h * dh, (h + 1) * dh)
            if bi not in gates:
                gates[bi] = _gate_vectors(gt_ref[bi, sl, :], nh, tri, eye, row, col)
            li_c, li_r, bc_c, bc_r = [a[h] for a in gates[bi]]
            x_r = li_r - bc_r
            cm_c = jnp.max(jnp.where(tri, x_r, -jnp.inf), axis=1, keepdims=True)
            m_prev = state_in[2][bi, h, 0:1, 0:1]
            mt_c = bc_c + jnp.maximum(m_prev, cm_c)
            dmat = jnp.exp(jnp.where(tri, (bc_c - mt_c) + x_r, -jnp.inf))
            inter = jnp.exp(bc_c + m_prev - mt_c)
            q = q_ref[bi, sl, lanes]
            k = k_ref[bi, sl, lanes]
            c_old = state_in[0][bi, h]
            part.append(dict(lanes=lanes, li_c=li_c, bc_c=bc_c, mt_c=mt_c, m_prev=m_prev, dmat=dmat,
                             inter=inter, q=q, k=k, c_old=c_old, n_old=state_in[1][bi, h],
                             qk=_dot_nt(q, k), qc=_dot(q, c_old.astype(BF16))))

        mix = [None] * gb
        for (bi, h), p in zip(chains, part):
            v = v_ref[bi, sl, p["lanes"]]
            sqk = p["qk"] * p["dmat"]
            num = p["inter"] * p["qc"] + _dot(sqk.astype(BF16), v)
            den = (p["inter"] * jnp.sum(p["q"].astype(F32) * p["n_old"], axis=1, keepdims=True)
                   + jnp.sum(sqk, axis=1, keepdims=True))
            h_til = num / jnp.maximum(jnp.abs(den), jnp.exp(-p["mt_c"]))
            hcell = jax.nn.sigmoid(o_ref[bi, sl, p["lanes"]]) * h_til
            hc = hcell - jnp.mean(hcell, axis=-1, keepdims=True)
            hn = hc * lax.rsqrt(jnp.mean(hc * hc, axis=-1, keepdims=True) + EPS) * mhg_ref[:, p["lanes"]]
            if fuse_out:
                part_mix = _dot(hn.astype(BF16), w_ref[p["lanes"], :])
                mix[bi] = part_mix if mix[bi] is None else mix[bi] + part_mix
            else:
                h_ref[bi, sl, p["lanes"]] = hn.astype(h_ref.dtype)
        if fuse_out:
            for bi in range(gb):
                h_ref[bi, sl, :] = r_ref[bi, sl, :] + _rms(mix[bi], g_ref[...])

        for (bi, h), p in zip(chains, part):
            v = v_ref[bi, sl, p["lanes"]]
            m_end = p["mt_c"][L - 1:L, :]
            bc_end = p["bc_c"][L - 1:L, :]
            w_end = jnp.exp(bc_end - p["bc_c"] + p["li_c"] - m_end)
            decay = jnp.exp(bc_end + p["m_prev"] - m_end)
            wk = w_end * p["k"].astype(F32)
            c_out[bi, h] = decay * p["c_old"] + _dot_tn(wk.astype(BF16), v)
            n_out[bi, h] = decay * p["n_old"] + jnp.sum(wk, axis=0, keepdims=True)
            m_out[bi, h] = jnp.broadcast_to(m_end, m_out.shape[2:])


def _mlstm(q, k, v, o, gt, c0, n0, m0, mh_g, resid, w_out, g_post, nh, chunk, rows_per_step):
    b, s, d = q.shape
    dh = d // nh
    gb = min(MLSTM_SEQS, b)
    assert rows_per_step % chunk == 0 and s % rows_per_step == 0 and b % gb == 0
    fuse_out = chunk >= LANES
    blk = lambda width: pl.BlockSpec((gb, rows_per_step, width), lambda bi, si: (bi, si, 0))
    st = lambda r, w: pl.BlockSpec((gb, nh, r, w), lambda bi, si: (bi, 0, 0, 0))
    vec = pl.BlockSpec((1, d), lambda bi, si: (0, 0))
    in_specs = [blk(d), blk(d), blk(d), blk(d), blk(LANES), st(dh, dh), st(1, dh), st(SUBLANES, LANES), vec]
    args = [q, k, v, o, gt, c0, n0, m0, mh_g]
    if fuse_out:
        in_specs += [blk(d), pl.BlockSpec((d, d), lambda bi, si: (0, 0)), vec]
        args += [resid, w_out, g_post]
    out = pl.pallas_call(
        functools.partial(_mlstm_kernel, chunk=chunk, nchunks=rows_per_step // chunk, nh=nh,
                          single_step=(s == rows_per_step), fuse_out=fuse_out),
        grid=(b // gb, s // rows_per_step),
        in_specs=in_specs,
        out_specs=[blk(d), st(dh, dh), st(1, dh), st(SUBLANES, LANES)],
        out_shape=[jax.ShapeDtypeStruct((b, s, d), F32 if fuse_out else BF16),
                   jax.ShapeDtypeStruct((b, nh, dh, dh), F32),
                   jax.ShapeDtypeStruct((b, nh, 1, dh), F32),
                   jax.ShapeDtypeStruct((b, nh, SUBLANES, LANES), F32)],
        compiler_params=_params("parallel", "arbitrary"),
        name="mlstm",
    )(*args)
    if fuse_out:
        return out
    h = _outproj(out[0].reshape(b * s, d), resid.reshape(b * s, d), w_out, g_post).reshape(b, s, d)
    return (h,) + tuple(out[1:])


def _trunk(x, state_conv, cache_k, cache_v, page_table, state_c, state_n, state_m, wts):
    b, s, d = x.shape
    m = b * s
    depth = wts["norms"].shape[0]
    cw = wts["conv_w0"].shape[2]
    dh = wts["lam0"].shape[2]
    aw = (wts["w_in0"].shape[2] - 3 * cw) // 3
    nh_a = aw // (2 * dh)
    nh_m = state_c.shape[2]
    dh_m = d // nh_m
    h = x.reshape(m, d)
    ks, vs, convs, cs, ns, ms = [], [], [], [], [], []
    for layer in range(depth):
        j = layer // 2
        g = wts["norms"][layer]
        if layer % 2 == 0:
            lam_init = _lambda_init(layer)
            prompt = cache_k is None
            proj = _inproj0(h, g[0:1], wts["w_in0"][j], wts["w_in0_qv_t"][j], cw, aw,
                            dh ** -0.5 * LOG2_E, prompt)
            gb, cx, k, v = proj[:4]
            sg = wts["subln_g0"][j].reshape(1, 2 * dh)
            if prompt:
                kb, qt, vt = proj[4:]
                attn = _attn_prompt(qt, kb.reshape(b, s, aw), vt, wts["lam0"][j], sg, b, s, nh_a, dh,
                                    lam_init).reshape(m, aw)
            else:
                attn = _attn_decode(proj[4], k, v, cache_k, cache_v, j, page_table, wts["lam0"][j], sg,
                                    b, s, nh_a, dh, lam_init)
            h = _outproj_conv_ffn(gb, cx, attn, h, wts["w_out0"][j], g[1:2], wts["conv_w0"][j],
                                  state_conv[j], s, g[2:3], wts["w_gu"], wts["w_down"], layer, g[3:4])
            ks.append(k.reshape(b, s, nh_a, 2 * dh))
            vs.append(v.reshape(b, s, nh_a, 2 * dh))
            convs.append(cx.reshape(b, s, cw)[:, s - 2:, :])
        else:
            q, k, v, o, gt = _inproj1(h, g[0:1], wts["w_in1"][j], wts["w_in1_gates"][j], wts["b_if1"][j],
                                      nh_m, dh_m ** -0.5)
            chunk = MLSTM_CHUNK if s % MLSTM_CHUNK == 0 else s
            sp = s
            r3 = lambda a: a.reshape(b, s, a.shape[1])
            q, k, v, o, gt, resid = r3(q), r3(k), r3(v), r3(o), r3(gt), r3(h)
            if chunk % SUBLANES != 0:
                sp = -(-s // SUBLANES) * SUBLANES
                chunk = sp
                pad = lambda a: jnp.pad(a, ((0, 0), (0, sp - s), (0, 0)))
                lane = lax.broadcasted_iota(jnp.int32, (b, sp - s, LANES), 2)
                gt_pad = jnp.where(lane < nh_m, -jnp.inf, 0.0).astype(F32)
                q, k, v, o, resid = pad(q), pad(k), pad(v), pad(o), pad(resid)
                gt = jnp.concatenate([gt, gt_pad], axis=1)
            rows = min(MLSTM_ROWS, sp)
            m0 = jnp.broadcast_to(state_m[j][:, :, None, None], (b, nh_m, SUBLANES, LANES))
            h3, c_new, n_new, m_new = _mlstm(q, k, v, o, gt, state_c[j], state_n[j][:, :, None, :], m0,
                                             wts["mh_g1"][j].reshape(1, d), resid, wts["w_out1"][j], g[1:2],
                                             nh_m, chunk, rows)
            h = h3[:, :s].reshape(m, d)
            cs.append(c_new)
            ns.append(n_new[:, :, 0, :])
            ms.append(m_new[:, :, 0, 0])
            h = _ffn(h, g[2:3], wts["w_gu"], wts["w_down"], layer, g[3:4])
    return (h.reshape(b, s, d), jnp.stack(ks), jnp.stack(vs), jnp.stack(convs),
            jnp.stack(cs), jnp.stack(ns), jnp.stack(ms))


def _qv_transposed(w_in0, cw):
    aw = (w_in0.shape[2] - 3 * cw) // 3
    wq = w_in0[:, :, 3 * cw:3 * cw + aw]
    wv = w_in0[:, :, 3 * cw + 2 * aw:]
    return jnp.swapaxes(jnp.concatenate([wq, wv], axis=2), 1, 2).astype(BF16)


def kernel(x_prompt, x_sample, cache_k, cache_v, state_conv, state_C, state_n, state_m, page_table,
           norms, w_in0, conv_w0, lam0, subln_g0, w_out0, w_in1, b_if1, mh_g1, w_out1, w_gu, w_down):
    d = x_prompt.shape[-1]
    nh_m = state_C.shape[2]
    n_odd = w_in1.shape[0]
    w_gates = jnp.pad(w_in1[:, :, 4 * d:], ((0, 0), (0, 0), (0, LANES - 2 * nh_m))).astype(BF16)
    b_pad = jnp.pad(b_if1.astype(F32), ((0, 0), (0, LANES - 2 * nh_m))).reshape(n_odd, 1, LANES)
    wts = dict(
        norms=norms.astype(F32), w_in0=w_in0.astype(BF16), conv_w0=conv_w0, lam0=lam0.astype(F32),
        w_in0_qv_t=_qv_transposed(w_in0, conv_w0.shape[2]),
        subln_g0=subln_g0.astype(F32), w_out0=w_out0.astype(BF16), w_in1=w_in1.astype(BF16),
        w_in1_gates=w_gates, b_if1=b_pad, mh_g1=mh_g1.astype(F32), w_out1=w_out1.astype(BF16),
        w_gu=w_gu.astype(BF16), w_down=w_down.astype(BF16))
    bp = x_prompt.shape[0]
    n_even = state_conv.shape[0]
    conv0 = jnp.zeros((n_even, bp) + state_conv.shape[2:], x_prompt.dtype)
    c0 = jnp.zeros((n_odd, bp) + state_C.shape[2:], F32)
    n0 = jnp.zeros((n_odd, bp) + state_n.shape[2:], F32)
    m0 = jnp.zeros((n_odd, bp) + state_m.shape[2:], F32)
    y_p, k_p, v_p, conv_p, c_p, n_p, m_p = _trunk(x_prompt, conv0, None, None, None, c0, n0, m0, wts)
    y_s, k_s, v_s, conv_s, c_s, n_s, m_s = _trunk(x_sample, state_conv, cache_k, cache_v, page_table,
                                                  state_C.astype(F32), state_n.astype(F32),
                                                  state_m.astype(F32), wts)
    return (y_p, y_s, k_p, v_p, conv_p, c_p, n_p, m_p, k_s, v_s, conv_s, c_s, n_s, m_s)
```

```python
import functools
import math

import jax
import jax.numpy as jnp
from jax import lax
from jax.experimental import pallas as pl
from jax.experimental.pallas import tpu as pltpu

EPS = 1e-6
LOG2_E = math.log2(math.e)
F32 = jnp.float32
BF16 = jnp.bfloat16

V7X_VMEM_BYTES = 64 * 1024 * 1024
VMEM_LIMIT_BYTES = V7X_VMEM_BYTES - 8 * 1024 * 1024
LANES = 128
SUBLANES = 8

ROW_TILE = 512
FFN_ROW_TILE = 512
FFN_COL_TILE = 256
ATTN_Q_TILE = 256
ATTN_K_TILE = 1024
DECODE_PAGES = 16
DECODE_SLOTS = 3
MLSTM_CHUNK = 512
MLSTM_ROWS = 512
MLSTM_SEQS = 1
MLSTM_SEQS_SHORT = 2


def _params(*semantics):
    return pltpu.CompilerParams(dimension_semantics=semantics, vmem_limit_bytes=VMEM_LIMIT_BYTES)


def _dot(a, b):
    return jnp.dot(a, b, preferred_element_type=F32)


def _dot_nt(a, b):
    return lax.dot_general(a, b, (((1,), (1,)), ((), ())), preferred_element_type=F32)


def _dot_tn(a, b):
    return lax.dot_general(a, b, (((0,), (0,)), ((), ())), preferred_element_type=F32)


def _rms(x, g):
    return x * lax.rsqrt(jnp.mean(x * x, axis=-1, keepdims=True) + EPS) * g


def _lambda_init(layer):
    return 0.8 - 0.6 * math.exp(-0.3 * layer)


def _lam_value(lam_ref, lam_init):
    lv = lam_ref[...]
    a = jnp.sum(lv[0:1] * lv[1:2], axis=-1, keepdims=True)
    b = jnp.sum(lv[2:3] * lv[3:4], axis=-1, keepdims=True)
    return jnp.exp(a) - jnp.exp(b) + lam_init


def _row_tile(m, want):
    t = min(want, m)
    assert m % t == 0, (m, t)
    return t


def _inproj0_kernel(x_ref, g_ref, w_ref, *rest, cw, aw, qscale, transposed):
    xn = _rms(x_ref[...], g_ref[...]).astype(BF16)

    def proj(lo, width):
        return _dot(xn, w_ref[:, lo:lo + width])

    q = proj(3 * cw, aw) * qscale
    k = proj(3 * cw + aw, aw)
    v = proj(3 * cw + 2 * aw, aw)
    if transposed:
        gb_ref, cx_ref, k_ref, v_ref, kb_ref, qt_ref, vt_ref = rest
        qt_ref[...] = jnp.transpose(q).astype(BF16)
        vt_ref[...] = jnp.transpose(v).astype(BF16)
    else:
        gb_ref, cx_ref, k_ref, v_ref, q_ref = rest
        q_ref[...] = q.astype(BF16)
    gb_ref[...] = proj(0, cw)
    cx_ref[...] = proj(cw, cw) * proj(2 * cw, cw)
    tm = k.shape[0]
    nh = aw // LANES
    for h in range(nh):
        k_ref[pl.ds(h, tm, stride=nh), :] = k[:, h * LANES:(h + 1) * LANES]
        v_ref[pl.ds(h, tm, stride=nh), :] = v[:, h * LANES:(h + 1) * LANES]
    if transposed:
        kb_ref[...] = k.astype(BF16)


def _inproj0(x, g, w, cw, aw, qscale, transposed):
    m, d = x.shape
    tm = _row_tile(m, ROW_TILE)
    n = w.shape[1]
    row = lambda width: pl.BlockSpec((tm, width), lambda i: (i, 0))
    const = lambda shape: pl.BlockSpec(shape, lambda i: (0,) * len(shape))
    nh = aw // LANES
    kv_spec = pl.BlockSpec((tm * nh, LANES), lambda i: (i, 0))
    kv_shape = jax.ShapeDtypeStruct((m * nh, LANES), F32)
    out_specs = [row(cw), row(cw), kv_spec, kv_spec, row(aw)]
    out_shape = [jax.ShapeDtypeStruct((m, cw), F32), jax.ShapeDtypeStruct((m, cw), F32), kv_shape, kv_shape,
                 jax.ShapeDtypeStruct((m, aw), BF16)]
    in_specs = [row(d), const((1, d)), const((d, n))]
    args = [x, g, w]
    if transposed:
        out_specs += [pl.BlockSpec((aw, tm), lambda i: (0, i))] * 2
        out_shape += [jax.ShapeDtypeStruct((aw, m), BF16)] * 2
    return pl.pallas_call(
        functools.partial(_inproj0_kernel, cw=cw, aw=aw, qscale=qscale, transposed=transposed),
        grid=(m // tm,),
        in_specs=in_specs,
        out_specs=out_specs,
        out_shape=out_shape,
        compiler_params=_params("parallel"),
        name="inproj0",
    )(*args)


def _attn_prompt_kernel(lam_ref, sg_ref, qt_ref, k_ref, vt_ref, o_ref, *, tq, tk, dh, lam_init):
    nch = tk // tq
    g = pl.program_id(2)
    qt = qt_ref[...]
    sub = lax.broadcasted_iota(jnp.int32, (2 * dh, tq), 0)
    zero = jnp.zeros((2 * dh, tq), qt.dtype)
    qqs = []
    for c in range(nch):
        qc = qt[:, c * tq:(c + 1) * tq]
        qqs.append(jnp.concatenate([jnp.where(sub < dh, qc, zero), jnp.where(sub >= dh, qc, zero)], axis=1))
    cols = 2 * tq

    def run(blocks, carry):
        scores, vts, nks = [], [], []
        for j, masked in blocks:
            off = pl.multiple_of(j * tk, tk)
            k = k_ref[pl.ds(off, tk), :]
            vts.append(vt_ref[:, pl.ds(off, tk)])
            nks.append([(c + 1) * tq if masked else tk for c in range(nch)])
            scores.append([_dot(k[:nks[-1][c]], qqs[c]) for c in range(nch)])
        for (j, masked), sc, vt, nk in zip(blocks, scores, vts, nks):
            probs, stats = [], []
            for c in range(nch):
                m, l, _ = carry[c]
                s = sc[c]
                if masked:
                    r = lax.broadcasted_iota(jnp.int32, (nk[c], cols), 0)
                    cc = lax.broadcasted_iota(jnp.int32, (nk[c], cols), 1)
                    s = jnp.where(r <= c * tq + jnp.where(cc >= tq, cc - tq, cc), s, -jnp.inf)
                m_new = jnp.maximum(m, jnp.max(s, axis=0, keepdims=True))
                alpha = jnp.exp2(m - m_new)
                p = jnp.exp2(s - m_new)
                stats.append((m_new, alpha, alpha * l + jnp.sum(p, axis=0, keepdims=True)))
                probs.append(p.astype(BF16))
            out = []
            for c in range(nch):
                m_new, alpha, l = stats[c]
                acc = alpha * carry[c][2] + _dot(vt[:, :nk[c]], probs[c])
                out.append((m_new, l, acc))
            carry = tuple(out)
        return carry

    init = tuple((jnp.full((1, cols), -jnp.inf, F32), jnp.zeros((1, cols), F32),
                  jnp.zeros((2 * dh, cols), F32)) for _ in range(nch))
    carry = lax.fori_loop(0, g // 2, lambda jj, c: run([(2 * jj, False), (2 * jj + 1, False)], c), init)
    carry = lax.cond(g % 2 == 1,
                     lambda c: run([(g - 1, False), (g, True)], c),
                     lambda c: run([(g, True)], c), carry)

    lam = _lam_value(lam_ref, lam_init)
    for c in range(nch):
        _, l, acc = carry[c]
        o = acc / l
        a = o[:, :tq] - lam * o[:, tq:]
        y = a * lax.rsqrt(jnp.mean(a * a, axis=0, keepdims=True) + EPS) * sg_ref[...] * (1.0 - lam_init)
        o_ref[c * tq:(c + 1) * tq, :] = jnp.transpose(y).astype(o_ref.dtype)


def _attn_prompt(qt, kb, vt, lam, subln_g, b, s, nh, dh, lam_init):
    tk = _row_tile(s, ATTN_K_TILE)
    tq = _row_tile(tk, ATTN_Q_TILE)
    hw = 2 * dh
    assert hw == LANES
    nq = s // tk
    return pl.pallas_call(
        functools.partial(_attn_prompt_kernel, tq=tq, tk=tk, dh=dh, lam_init=lam_init),
        grid=(b, nh, nq),
        in_specs=[pl.BlockSpec(lam.shape, lambda bi, h, qi: (0, 0)),
                  pl.BlockSpec((hw, 1), lambda bi, h, qi: (0, 0)),
                  pl.BlockSpec((hw, tk), lambda bi, h, qi: (h, bi * nq + qi)),
                  pl.BlockSpec((None, s, hw), lambda bi, h, qi: (bi, 0, h)),
                  pl.BlockSpec((hw, s), lambda bi, h, qi: (h, bi))],
        out_specs=pl.BlockSpec((None, tk, hw), lambda bi, h, qi: (bi, qi, h)),
        out_shape=jax.ShapeDtypeStruct((b, s, nh * hw), BF16),
        compiler_params=_params("parallel", "parallel", "arbitrary"),
        name="attn_prompt",
    )(lam, subln_g.reshape(hw, 1), qt, kb, vt)


def _attn_decode_kernel(pt_ref, lam_ref, sg_ref, q_ref, kn_ref, vn_ref, ck_ref, cv_ref, o_ref,
                        kbuf, vbuf, sem, m_ref, l_ref, acc_ref, *, pages, layer_idx, nh, nq, lam_init):
    b = pl.program_id(0)
    step = pl.program_id(1)
    n_steps = pl.num_programs(1)
    total = pl.num_programs(0) * n_steps
    n_slots = kbuf.shape[0]
    ahead = n_slots - 1
    n = b * n_steps + step
    slot = n % n_slots
    q = q_ref[...]
    rows = q.shape[0]
    hw = q.shape[1]
    prow = kbuf.shape[2]

    def page_reads(t, r, sl):
        pg = pt_ref[t // n_steps, (t % n_steps) * pages + r]
        return (pltpu.make_async_copy(ck_ref.at[layer_idx, pg], kbuf.at[sl, r], sem.at[0, sl]),
                pltpu.make_async_copy(cv_ref.at[layer_idx, pg], vbuf.at[sl, r], sem.at[1, sl]))

    @pl.when(n == 0)
    def _():
        for t in range(ahead):
            for r in range(pages):
                for cp in page_reads(t, r, t):
                    cp.start()

    fetch = jnp.where(n + ahead < total, n + ahead, n)
    fetch_slot = (n + ahead) % n_slots

    for r in range(pages):
        for cp in page_reads(n, r, slot):
            cp.wait()

    def masks(width):
        r = lax.broadcasted_iota(jnp.int32, (rows, width), 0)
        c = lax.broadcasted_iota(jnp.int32, (rows, width), 1)
        return r, c, (c % nh) == (r // (2 * nq))

    def update(scores, values):
        m_old = m_ref[...]
        m_new = m_old
        for s in scores:
            m_new = jnp.maximum(m_new, jnp.max(s, axis=-1, keepdims=True))
        alpha = jnp.exp2(m_old - m_new)
        l = alpha * l_ref[...]
        acc = alpha * acc_ref[...]
        for s, v in zip(scores, values):
            p = jnp.exp2(s - m_new)
            l = l + jnp.sum(p, axis=-1, keepdims=True)
            acc = acc + _dot(p.astype(BF16), v)
        m_ref[...] = m_new
        l_ref[...] = l
        acc_ref[...] = acc

    @pl.when(step == 0)
    def _():
        m_ref[...] = jnp.full(m_ref.shape, -jnp.inf, F32)
        l_ref[...] = jnp.zeros(l_ref.shape, F32)
        acc_ref[...] = jnp.zeros(acc_ref.shape, F32)

    _, _, head_ok = masks(prow)
    scores = []
    for r in range(pages):
        for cp in page_reads(fetch, r, fetch_slot):
            cp.start()
        scores.append(jnp.where(head_ok, _dot_nt(q, kbuf[slot, r].astype(BF16)), -jnp.inf))
    update(scores, [vbuf[slot, r].astype(BF16) for r in range(pages)])

    @pl.when(n == total - 1)
    def _():
        for d in range(1, ahead + 1):
            for r in range(pages):
                for cp in page_reads(n, r, (n + d) % n_slots):
                    cp.wait()

    @pl.when(step == n_steps - 1)
    def _():
        width = kn_ref.shape[0]
        r, c, ok = masks(width)
        ok = ok & ((c // nh) <= (r % nq)) & (c < nq * nh)
        s = jnp.where(ok, _dot_nt(q, kn_ref[...].astype(BF16)), -jnp.inf)
        update([s], [vn_ref[...].astype(BF16)])
        o = acc_ref[...] / l_ref[...]
        a = o - _lam_value(lam_ref, lam_init) * pltpu.roll(o, rows - nq, 0)
        y = _rms(a, sg_ref[...]) * (1.0 - lam_init)
        for h in range(nh):
            o_ref[:, h * hw:(h + 1) * hw] = y[h * 2 * nq:h * 2 * nq + nq].astype(o_ref.dtype)


def _attn_decode(qb, k_new, v_new, cache_k, cache_v, layer_idx, page_table, lam, subln_g,
                 nb, nq, nh, dh, lam_init):
    hw = 2 * dh
    assert 2 * nq == SUBLANES and hw == LANES and nq * nh <= LANES
    n_pages = page_table.shape[1]
    page = cache_k.shape[2]
    pages = min(DECODE_PAGES, n_pages)
    assert n_pages % pages == 0
    rows = nh * 2 * nq
    qh = jnp.tile(qb.reshape(nb, nq, nh, hw).transpose(0, 2, 1, 3), (1, 1, 2, 1)).reshape(nb, rows, hw)
    r = lax.broadcasted_iota(jnp.int32, (rows, hw), 0)
    c = lax.broadcasted_iota(jnp.int32, (rows, hw), 1)
    qbd = jnp.where(((r // nq) % 2) == (c // dh), qh, jnp.zeros_like(qh))
    pad_new = lambda a: jnp.pad(a.reshape(nb, nq * nh, hw), ((0, 0), (0, LANES - nq * nh), (0, 0)))
    view = lambda cache: cache.reshape(cache.shape[0], cache.shape[1], page * nh, hw)

    const = lambda shape: pl.BlockSpec(shape, lambda b, p, pt: (0,) * len(shape))
    per_b = lambda shape: pl.BlockSpec((None,) + shape, lambda b, p, pt: (b,) + (0,) * len(shape))
    in_hbm = pl.BlockSpec(memory_space=pl.ANY)
    assert nb * (n_pages // pages) >= DECODE_SLOTS
    page_buf = pltpu.VMEM((DECODE_SLOTS, pages, page * nh, hw), cache_k.dtype)
    grid_spec = pltpu.PrefetchScalarGridSpec(
        num_scalar_prefetch=1,
        grid=(nb, n_pages // pages),
        in_specs=[const(lam.shape), const((1, hw)), per_b((rows, hw)),
                  per_b((LANES, hw)), per_b((LANES, hw)), in_hbm, in_hbm],
        out_specs=per_b((nq, nh * hw)),
        scratch_shapes=[page_buf, page_buf, pltpu.SemaphoreType.DMA((2, DECODE_SLOTS)),
                        pltpu.VMEM((rows, 1), F32), pltpu.VMEM((rows, 1), F32),
                        pltpu.VMEM((rows, hw), F32)],
    )
    out = pl.pallas_call(
        functools.partial(_attn_decode_kernel, pages=pages, layer_idx=layer_idx, nh=nh, nq=nq,
                          lam_init=lam_init),
        grid_spec=grid_spec,
        out_shape=jax.ShapeDtypeStruct((nb, nq, nh * hw), F32),
        compiler_params=_params("arbitrary", "arbitrary"),
        name="attn_decode",
    )(page_table, lam, subln_g, qbd, pad_new(k_new), pad_new(v_new), view(cache_k), view(cache_v))
    return out.reshape(nb * nq, nh * hw)


def _conv_taps(cx, prev1, prev2, cw_ref):
    return cw_ref[0:1, :] * prev2 + cw_ref[1:2, :] * prev1 + cw_ref[2:3, :] * cx


def _ffn_rows(x, g_in_ref, wgu_ref, wd_ref, g_out_ref, act_ref, tf):
    xn = _rms(x, g_in_ref[...]).astype(BF16)
    dff = wd_ref.shape[0]
    for lo in range(0, dff, tf):
        gate = _dot(xn, wgu_ref[:, lo:lo + tf])
        up = _dot(xn, wgu_ref[:, dff + lo:dff + lo + tf])
        act_ref[:, lo:lo + tf] = (gate * jax.nn.sigmoid(gate) * up).astype(BF16)
    return x + _rms(_dot(act_ref[...], wd_ref[...]), g_out_ref[...])


def _outproj_conv_kernel(gb_ref, cx_ref, at_ref, r_ref, w_ref, g_ref, cw_ref,
                         g_in_ref, wgu_ref, wd_ref, g_out_ref, *rest, seq, carry_mode, tf):
    cx = cx_ref[...]
    tm, cwid = cx.shape
    row = lax.broadcasted_iota(jnp.int32, (tm, 1), 0)
    roll1 = pltpu.roll(cx, 1, 0)
    roll2 = pltpu.roll(cx, 2, 0)
    if carry_mode:
        st_ref, o_ref, act_ref, carry = rest

        @pl.when((pl.program_id(0) * tm) % seq == 0)
        def _():
            carry[0:2, :] = st_ref[...]

        c0 = carry[0:1, :]
        c1 = carry[1:2, :]
        prev1 = jnp.where(row >= 1, roll1, c1)
        prev2 = jnp.where(row >= 2, roll2, jnp.where(row == 1, c1, c0))
        carry[0:2, :] = cx[tm - 2:tm, :]
    else:
        e1_ref, e2_ref, o_ref, act_ref = rest
        t = row % seq
        prev1 = jnp.where(t >= 1, roll1, e1_ref[...])
        prev2 = jnp.where(t >= 2, roll2, e2_ref[...])
    yconv = (gb_ref[...] * _conv_taps(cx, prev1, prev2, cw_ref)).astype(BF16)
    mix = _dot(yconv, w_ref[0:cwid, :]) + _dot(at_ref[...].astype(BF16), w_ref[cwid:, :])
    h = r_ref[...] + _rms(mix, g_ref[...])
    o_ref[...] = _ffn_rows(h, g_in_ref, wgu_ref, wd_ref, g_out_ref, act_ref, tf)


def _outproj_conv_ffn(gb, cx, attn, resid, w, g, conv_w, state, seq, g_in, w_gu, w_down, layer, g_out):
    m, d = resid.shape
    cwid = gb.shape[1]
    dff = w_down.shape[1]
    tm = _row_tile(m, FFN_ROW_TILE)
    tf = _row_tile(dff, FFN_COL_TILE)
    carry_mode = seq % tm == 0
    row = lambda width: pl.BlockSpec((tm, width), lambda i: (i, 0))
    const = lambda shape: pl.BlockSpec(shape, lambda i: (0,) * len(shape), pipeline_mode=pl.Buffered(1))
    resident = lambda shape: pl.BlockSpec((None,) + shape, lambda i: (layer, 0, 0),
                                          pipeline_mode=pl.Buffered(1))
    in_specs = [row(cwid), row(cwid), row(attn.shape[1]), row(d), const(w.shape), const((1, d)),
                const(conv_w.shape), const((1, d)), resident((d, 2 * dff)), resident((dff, d)), const((1, d))]
    scratch = [pltpu.VMEM((tm, dff), BF16)]
    if carry_mode:
        extra = [state]
        in_specs += [pl.BlockSpec((None, 2, cwid), lambda i: ((i * tm) // seq, 0, 0))]
        scratch += [pltpu.VMEM((SUBLANES, cwid), F32)]
    else:
        assert tm % seq == 0 and seq >= 2
        nb = m // seq
        zeros = jnp.zeros((nb, seq - 1, cwid), F32)
        e1 = jnp.concatenate([state[:, 1:2], zeros], axis=1).reshape(m, cwid)
        e2 = jnp.concatenate([state[:, 0:1], state[:, 1:2], zeros[:, 1:]], axis=1).reshape(m, cwid)
        extra = [e1, e2]
        in_specs += [row(cwid), row(cwid)]
    return pl.pallas_call(
        functools.partial(_outproj_conv_kernel, seq=seq, carry_mode=carry_mode, tf=tf),
        grid=(m // tm,),
        in_specs=in_specs,
        out_specs=row(d),
        out_shape=jax.ShapeDtypeStruct((m, d), F32),
        scratch_shapes=scratch,
        compiler_params=_params("arbitrary"),
        name="outproj_conv_ffn",
    )(gb, cx, attn, resid, w, g, conv_w, g_in, w_gu, w_down, g_out, *extra)


def _outproj_kernel(x_ref, r_ref, w_ref, g_ref, o_ref):
    o_ref[...] = r_ref[...] + _rms(_dot(x_ref[...], w_ref[...]), g_ref[...])


def _outproj(x, resid, w, g):
    m, d = resid.shape
    tm = _row_tile(m, ROW_TILE)
    row = lambda width: pl.BlockSpec((tm, width), lambda i: (i, 0))
    return pl.pallas_call(
        _outproj_kernel,
        grid=(m // tm,),
        in_specs=[row(x.shape[1]), row(d), pl.BlockSpec(w.shape, lambda i: (0, 0)),
                  pl.BlockSpec((1, d), lambda i: (0, 0))],
        out_specs=row(d),
        out_shape=jax.ShapeDtypeStruct((m, d), F32),
        compiler_params=_params("parallel"),
        name="outproj",
    )(x, resid, w, g)


def _ffn_kernel(x_ref, g_in_ref, wgu_ref, wd_ref, g_out_ref, o_ref, act_ref, *, tf):
    o_ref[...] = _ffn_rows(x_ref[...], g_in_ref, wgu_ref, wd_ref, g_out_ref, act_ref, tf)


def _ffn(x, g_in, w_gu, w_down, layer, g_out):
    m, d = x.shape
    dff = w_down.shape[1]
    tm = _row_tile(m, FFN_ROW_TILE)
    tf = _row_tile(dff, FFN_COL_TILE)
    row = pl.BlockSpec((tm, d), lambda i: (i, 0))
    vec = pl.BlockSpec((1, d), lambda i: (0, 0))
    resident = lambda shape: pl.BlockSpec((None,) + shape, lambda i: (layer, 0, 0),
                                          pipeline_mode=pl.Buffered(1))
    return pl.pallas_call(
        functools.partial(_ffn_kernel, tf=tf),
        grid=(m // tm,),
        in_specs=[row, vec, resident((d, 2 * dff)), resident((dff, d)), vec],
        out_specs=row,
        out_shape=jax.ShapeDtypeStruct((m, d), F32),
        scratch_shapes=[pltpu.VMEM((tm, dff), BF16)],
        compiler_params=_params("parallel"),
        name="ffn",
    )(x, g_in, w_gu, w_down, g_out)


def _inproj1_kernel(x_ref, g_ref, w_ref, wg_ref, b_ref, q_ref, k_ref, v_ref, o_ref, gt_ref,
                    *, d, nh, kscale):
    xn = _rms(x_ref[...], g_ref[...]).astype(BF16)
    q_ref[...] = _dot(xn, w_ref[:, 0:d]).astype(BF16)
    k_ref[...] = (_dot(xn, w_ref[:, d:2 * d]) * kscale).astype(BF16)
    v_ref[...] = _dot(xn, w_ref[:, 2 * d:3 * d]).astype(BF16)
    o_ref[...] = _dot(xn, w_ref[:, 3 * d:4 * d])
    gates = _dot(xn, wg_ref[...]) + b_ref[...]
    lane = lax.broadcasted_iota(jnp.int32, gates.shape, 1)
    gt_ref[...] = jnp.where(lane < nh, gates, jax.nn.log_sigmoid(gates))


def _inproj1(x, g, w, wg, b, nh, kscale):
    m, d = x.shape
    tm = _row_tile(m, ROW_TILE)
    row = lambda width: pl.BlockSpec((tm, width), lambda i: (i, 0))
    const = lambda shape: pl.BlockSpec(shape, lambda i: (0,) * len(shape))
    shapes = [(d, BF16), (d, BF16), (d, BF16), (d, F32), (LANES, F32)]
    return pl.pallas_call(
        functools.partial(_inproj1_kernel, d=d, nh=nh, kscale=kscale),
        grid=(m // tm,),
        in_specs=[row(d), const((1, d)), const(w.shape), const(wg.shape), const((1, LANES))],
        out_specs=[row(wd) for wd, _ in shapes],
        out_shape=[jax.ShapeDtypeStruct((m, wd), dt) for wd, dt in shapes],
        compiler_params=_params("parallel"),
        name="inproj1",
    )(x, g, w, wg, b)


def _gate_vectors(gt, nh, tri, eye, row, col):
    L = gt.shape[0]
    if L % LANES == 0:
        lane = lax.broadcasted_iota(jnp.int32, gt.shape, 1)
        lf = jnp.where((lane >= nh) & (lane < 2 * nh), gt, 0.0)
        ones = jnp.where(tri, 1.0, 0.0).astype(BF16)
        hi = lf.astype(BF16)
        rest = lf - hi.astype(F32)
        mid = rest.astype(BF16)
        lo = (rest - mid.astype(F32)).astype(BF16)
        bc = _dot(ones, hi) + _dot(ones, mid) + _dot(ones, lo)
        gt_t = jnp.transpose(gt)
        bc_t = jnp.transpose(bc)
        li_c = [gt[:, h:h + 1] for h in range(nh)]
        li_r = [gt_t[h:h + 1, :] for h in range(nh)]
        bc_c = [bc[:, nh + h:nh + h + 1] for h in range(nh)]
        bc_r = [bc_t[nh + h:nh + h + 1, :] for h in range(nh)]
        return li_c, li_r, bc_c, bc_r
    li_c, li_r, bc_c, bc_r = [], [], [], []
    for h in range(nh):
        li = gt[:, h:h + 1]
        lf = gt[:, nh + h:nh + h + 1]
        lf_r = jnp.sum(jnp.where(eye, lf, 0.0), axis=0, keepdims=True)
        li_c.append(li)
        li_r.append(jnp.sum(jnp.where(eye, li, 0.0), axis=0, keepdims=True))
        bc_c.append(jnp.sum(jnp.where(tri, lf_r, 0.0), axis=1, keepdims=True))
        bc_r.append(jnp.sum(jnp.where(row <= col, lf, 0.0), axis=0, keepdims=True))
    return li_c, li_r, bc_c, bc_r


def _mlstm_kernel(q_ref, k_ref, v_ref, o_ref, gt_ref, c0_ref, n0_ref, m0_ref, mhg_ref, *rest,
                  chunk, nchunks, nh, single_step, fuse_out):
    if fuse_out:
        r_ref, w_ref, g_ref, h_ref, c_out, n_out, m_out = rest
    else:
        h_ref, c_out, n_out, m_out = rest
    gb = q_ref.shape[0]
    dh = q_ref.shape[2] // nh
    chains = [(bi, h) for bi in range(gb) for h in range(nh)]

    if not single_step:
        @pl.when(pl.program_id(1) == 0)
        def _():
            c_out[...] = c0_ref[...]
            n_out[...] = n0_ref[...]
            m_out[...] = m0_ref[...]

    L = chunk
    row = lax.broadcasted_iota(jnp.int32, (L, L), 0)
    col = lax.broadcasted_iota(jnp.int32, (L, L), 1)
    tri = col <= row
    eye = col == row

    for c in range(nchunks):
        sl = pl.ds(c * L, L)
        state_in = (c0_ref, n0_ref, m0_ref) if (single_step and c == 0) else (c_out, n_out, m_out)
        part = []
        gates = {}
        for bi, h in chains:
            lanes = slice(h * dh, (h + 1) * dh)
            if bi not in gates:
                gates[bi] = _gate_vectors(gt_ref[bi, sl, :], nh, tri, eye, row, col)
            li_c, li_r, bc_c, bc_r = [a[h] for a in gates[bi]]
            x_r = li_r - bc_r
            cm_c = jnp.max(jnp.where(tri, x_r, -jnp.inf), axis=1, keepdims=True)
            m_prev = state_in[2][bi, h, 0:1, 0:1]
            mt_c = bc_c + jnp.maximum(m_prev, cm_c)
            dmat = jnp.exp(jnp.where(tri, (bc_c - mt_c) + x_r, -jnp.inf))
            inter = jnp.exp(bc_c + m_prev - mt_c)
            q = q_ref[bi, sl, lanes]
            k = k_ref[bi, sl, lanes]
            c_old = state_in[0][bi, h]
            part.append(dict(lanes=lanes, li_c=li_c, bc_c=bc_c, mt_c=mt_c, m_prev=m_prev, dmat=dmat,
                             inter=inter, q=q, k=k, c_old=c_old, n_old=state_in[1][bi, h],
                             qk=_dot_nt(q, k), qc=_dot(q, c_old.astype(BF16))))

        mix = [None] * gb
        for (bi, h), p in zip(chains, part):
            v = v_ref[bi, sl, p["lanes"]]
            sqk = p["qk"] * p["dmat"]
            num = p["inter"] * p["qc"] + _dot(sqk.astype(BF16), v)
            den = (p["inter"] * jnp.sum(p["q"].astype(F32) * p["n_old"], axis=1, keepdims=True)
                   + jnp.sum(sqk, axis=1, keepdims=True))
            h_til = num / jnp.maximum(jnp.abs(den), jnp.exp(-p["mt_c"]))
            hcell = jax.nn.sigmoid(o_ref[bi, sl, p["lanes"]]) * h_til
            hc = hcell - jnp.mean(hcell, axis=-1, keepdims=True)
            hn = hc * lax.rsqrt(jnp.mean(hc * hc, axis=-1, keepdims=True) + EPS) * mhg_ref[:, p["lanes"]]
            if fuse_out:
                part_mix = _dot(hn.astype(BF16), w_ref[p["lanes"], :])
                mix[bi] = part_mix if mix[bi] is None else mix[bi] + part_mix
            else:
                h_ref[bi, sl, p["lanes"]] = hn.astype(h_ref.dtype)
        if fuse_out:
            for bi in range(gb):
                h_ref[bi, sl, :] = r_ref[bi, sl, :] + _rms(mix[bi], g_ref[...])

        for (bi, h), p in zip(chains, part):
            v = v_ref[bi, sl, p["lanes"]]
            m_end = p["mt_c"][L - 1:L, :]
            bc_end = p["bc_c"][L - 1:L, :]
            w_end = jnp.exp(bc_end - p["bc_c"] + p["li_c"] - m_end)
            decay = jnp.exp(bc_end + p["m_prev"] - m_end)
            wk = w_end * p["k"].astype(F32)
            c_out[bi, h] = decay * p["c_old"] + _dot_tn(wk.astype(BF16), v)
            n_out[bi, h] = decay * p["n_old"] + jnp.sum(wk, axis=0, keepdims=True)
            m_out[bi, h] = jnp.broadcast_to(m_end, m_out.shape[2:])


def _mlstm(q, k, v, o, gt, c0, n0, m0, mh_g, resid, w_out, g_post, nh, chunk, rows_per_step):
    b, s, d = q.shape
    dh = d // nh
    fuse_out = chunk >= LANES
    gb = min(MLSTM_SEQS if fuse_out else MLSTM_SEQS_SHORT, b)
    assert rows_per_step % chunk == 0 and s % rows_per_step == 0 and b % gb == 0
    blk = lambda width: pl.BlockSpec((gb, rows_per_step, width), lambda bi, si: (bi, si, 0))
    st = lambda r, w: pl.BlockSpec((gb, nh, r, w), lambda bi, si: (bi, 0, 0, 0))
    vec = pl.BlockSpec((1, d), lambda bi, si: (0, 0))
    in_specs = [blk(d), blk(d), blk(d), blk(d), blk(LANES), st(dh, dh), st(1, dh), st(SUBLANES, LANES), vec]
    args = [q, k, v, o, gt, c0, n0, m0, mh_g]
    if fuse_out:
        in_specs += [blk(d), pl.BlockSpec((d, d), lambda bi, si: (0, 0)), vec]
        args += [resid, w_out, g_post]
    out = pl.pallas_call(
        functools.partial(_mlstm_kernel, chunk=chunk, nchunks=rows_per_step // chunk, nh=nh,
                          single_step=(s == rows_per_step), fuse_out=fuse_out),
        grid=(b // gb, s // rows_per_step),
        in_specs=in_specs,
        out_specs=[blk(d), st(dh, dh), st(1, dh), st(SUBLANES, LANES)],
        out_shape=[jax.ShapeDtypeStruct((b, s, d), F32 if fuse_out else BF16),
                   jax.ShapeDtypeStruct((b, nh, dh, dh), F32),
                   jax.ShapeDtypeStruct((b, nh, 1, dh), F32),
                   jax.ShapeDtypeStruct((b, nh, SUBLANES, LANES), F32)],
        compiler_params=_params("parallel", "arbitrary"),
        name="mlstm",
    )(*args)
    if fuse_out:
        return out
    h = _outproj(out[0].reshape(b * s, d), resid.reshape(b * s, d), w_out, g_post).reshape(b, s, d)
    return (h,) + tuple(out[1:])


def _trunk(x, state_conv, cache_k, cache_v, page_table, state_c, state_n, state_m, wts):
    b, s, d = x.shape
    m = b * s
    depth = wts["norms"].shape[0]
    cw = wts["conv_w0"].shape[2]
    dh = wts["lam0"].shape[2]
    aw = (wts["w_in0"].shape[2] - 3 * cw) // 3
    nh_a = aw // (2 * dh)
    nh_m = state_c.shape[2]
    dh_m = d // nh_m
    h = x.reshape(m, d)
    ks, vs, convs, cs, ns, ms = [], [], [], [], [], []
    for layer in range(depth):
        j = layer // 2
        g = wts["norms"][layer]
        if layer % 2 == 0:
            lam_init = _lambda_init(layer)
            prompt = cache_k is None
            proj = _inproj0(h, g[0:1], wts["w_in0"][j], cw, aw, dh ** -0.5 * LOG2_E, prompt)
            gb, cx, k, v = proj[:4]
            sg = wts["subln_g0"][j].reshape(1, 2 * dh)
            if prompt:
                kb, qt, vt = proj[4:]
                attn = _attn_prompt(qt, kb.reshape(b, s, aw), vt, wts["lam0"][j], sg, b, s, nh_a, dh,
                                    lam_init).reshape(m, aw)
            else:
                attn = _attn_decode(proj[4], k, v, cache_k, cache_v, j, page_table, wts["lam0"][j], sg,
                                    b, s, nh_a, dh, lam_init)
            h = _outproj_conv_ffn(gb, cx, attn, h, wts["w_out0"][j], g[1:2], wts["conv_w0"][j],
                                  state_conv[j], s, g[2:3], wts["w_gu"], wts["w_down"], layer, g[3:4])
            ks.append(k.reshape(b, s, nh_a, 2 * dh))
            vs.append(v.reshape(b, s, nh_a, 2 * dh))
            convs.append(cx.reshape(b, s, cw)[:, s - 2:, :])
        else:
            q, k, v, o, gt = _inproj1(h, g[0:1], wts["w_in1"][j], wts["w_in1_gates"][j], wts["b_if1"][j],
                                      nh_m, dh_m ** -0.5)
            chunk = MLSTM_CHUNK if s % MLSTM_CHUNK == 0 else s
            sp = s
            r3 = lambda a: a.reshape(b, s, a.shape[1])
            q, k, v, o, gt, resid = r3(q), r3(k), r3(v), r3(o), r3(gt), r3(h)
            if chunk % SUBLANES != 0:
                sp = -(-s // SUBLANES) * SUBLANES
                chunk = sp
                pad = lambda a: jnp.pad(a, ((0, 0), (0, sp - s), (0, 0)))
                lane = lax.broadcasted_iota(jnp.int32, (b, sp - s, LANES), 2)
                gt_pad = jnp.where(lane < nh_m, -jnp.inf, 0.0).astype(F32)
                q, k, v, o, resid = pad(q), pad(k), pad(v), pad(o), pad(resid)
                gt = jnp.concatenate([gt, gt_pad], axis=1)
            rows = min(MLSTM_ROWS, sp)
            m0 = jnp.broadcast_to(state_m[j][:, :, None, None], (b, nh_m, SUBLANES, LANES))
            h3, c_new, n_new, m_new = _mlstm(q, k, v, o, gt, state_c[j], state_n[j][:, :, None, :], m0,
                                             wts["mh_g1"][j].reshape(1, d), resid, wts["w_out1"][j], g[1:2],
                                             nh_m, chunk, rows)
            h = h3[:, :s].reshape(m, d)
            cs.append(c_new)
            ns.append(n_new[:, :, 0, :])
            ms.append(m_new[:, :, 0, 0])
            h = _ffn(h, g[2:3], wts["w_gu"], wts["w_down"], layer, g[3:4])
    return (h.reshape(b, s, d), jnp.stack(ks), jnp.stack(vs), jnp.stack(convs),
            jnp.stack(cs), jnp.stack(ns), jnp.stack(ms))


def kernel(x_prompt, x_sample, cache_k, cache_v, state_conv, state_C, state_n, state_m, page_table,
           norms, w_in0, conv_w0, lam0, subln_g0, w_out0, w_in1, b_if1, mh_g1, w_out1, w_gu, w_down):
    d = x_prompt.shape[-1]
    nh_m = state_C.shape[2]
    n_odd = w_in1.shape[0]
    w_gates = jnp.pad(w_in1[:, :, 4 * d:], ((0, 0), (0, 0), (0, LANES - 2 * nh_m))).astype(BF16)
    b_pad = jnp.pad(b_if1.astype(F32), ((0, 0), (0, LANES - 2 * nh_m))).reshape(n_odd, 1, LANES)
    wts = dict(
        norms=norms.astype(F32), w_in0=w_in0.astype(BF16), conv_w0=conv_w0, lam0=lam0.astype(F32),
        subln_g0=subln_g0.astype(F32), w_out0=w_out0.astype(BF16), w_in1=w_in1[:, :, :4 * d].astype(BF16),
        w_in1_gates=w_gates, b_if1=b_pad, mh_g1=mh_g1.astype(F32), w_out1=w_out1.astype(BF16),
        w_gu=w_gu.astype(BF16), w_down=w_down.astype(BF16))
    bp = x_prompt.shape[0]
    n_even = state_conv.shape[0]
    conv0 = jnp.zeros((n_even, bp) + state_conv.shape[2:], x_prompt.dtype)
    c0 = jnp.zeros((n_odd, bp) + state_C.shape[2:], F32)
    n0 = jnp.zeros((n_odd, bp) + state_n.shape[2:], F32)
    m0 = jnp.zeros((n_odd, bp) + state_m.shape[2:], F32)
    y_p, k_p, v_p, conv_p, c_p, n_p, m_p = _trunk(x_prompt, conv0, None, None, None, c0, n0, m0, wts)
    y_s, k_s, v_s, conv_s, c_s, n_s, m_s = _trunk(x_sample, state_conv, cache_k, cache_v, page_table,
                                                  state_C.astype(F32), state_n.astype(F32),
                                                  state_m.astype(F32), wts)
    return (y_p, y_s, k_p, v_p, conv_p, c_p, n_p, m_p, k_s, v_s, conv_s, c_s, n_s, m_s)
```

```python
import functools
import math

import jax
import jax.numpy as jnp
from jax import lax
from jax.experimental import pallas as pl
from jax.experimental.pallas import tpu as pltpu

EPS = 1e-6
LOG2_E = math.log2(math.e)
F32 = jnp.float32
BF16 = jnp.bfloat16

V7X_VMEM_BYTES = 64 * 1024 * 1024
VMEM_LIMIT_BYTES = V7X_VMEM_BYTES - 8 * 1024 * 1024
VMEM_LIMIT_LARGE_BYTES = V7X_VMEM_BYTES - 4 * 1024 * 1024
LANES = 128
SUBLANES = 8

ROW_TILE = 512
FFN_ROW_TILE = 512
FFN_COL_TILE = 256
ATTN_Q_TILE = 256
ATTN_K_TILE = 1024
DECODE_PAGES = 16
DECODE_SLOTS = 3
DECODE_COLS = 128
DECODE_NEW_ROWS = 16
DECODE_SCORE_PAGES = 4
MLSTM_CHUNK = 512
MLSTM_ROWS = 512
MLSTM_SEQS = 1
MLSTM_SEQS_SHORT = 2


def _params(*semantics, vmem_limit_bytes=VMEM_LIMIT_BYTES):
    return pltpu.CompilerParams(dimension_semantics=semantics, vmem_limit_bytes=vmem_limit_bytes)


def _dot(a, b):
    return jnp.dot(a, b, preferred_element_type=F32)


def _dot_nt(a, b):
    return lax.dot_general(a, b, (((1,), (1,)), ((), ())), preferred_element_type=F32)


def _dot_tn(a, b):
    return lax.dot_general(a, b, (((0,), (0,)), ((), ())), preferred_element_type=F32)


def _rms(x, g):
    return x * lax.rsqrt(jnp.mean(x * x, axis=-1, keepdims=True) + EPS) * g


def _lambda_init(layer):
    return 0.8 - 0.6 * math.exp(-0.3 * layer)


def _lam_value(lam_ref, lam_init):
    lv = lam_ref[...]
    a = jnp.sum(lv[0:1] * lv[1:2], axis=-1, keepdims=True)
    b = jnp.sum(lv[2:3] * lv[3:4], axis=-1, keepdims=True)
    return jnp.exp(a) - jnp.exp(b) + lam_init


def _row_tile(m, want):
    t = min(want, m)
    assert m % t == 0, (m, t)
    return t


def _inproj0_kernel(x_ref, g_ref, w_ref, *rest, cw, aw, qscale, transposed):
    xn = _rms(x_ref[...], g_ref[...]).astype(BF16)

    def proj(lo, width):
        return _dot(xn, w_ref[:, lo:lo + width])

    q = proj(3 * cw, aw) * qscale
    k = proj(3 * cw + aw, aw)
    v = proj(3 * cw + 2 * aw, aw)
    if transposed:
        gb_ref, cx_ref, k_ref, v_ref, kb_ref, qt_ref, vt_ref = rest
        qt_ref[...] = jnp.transpose(q).astype(BF16)
        vt_ref[...] = jnp.transpose(v).astype(BF16)
    else:
        gb_ref, cx_ref, k_ref, v_ref, q_ref = rest
        q_ref[...] = q.astype(BF16)
    gb_ref[...] = proj(0, cw)
    cx_ref[...] = proj(cw, cw) * proj(2 * cw, cw)
    tm = k.shape[0]
    nh = aw // LANES
    for h in range(nh):
        k_ref[pl.ds(h, tm, stride=nh), :] = k[:, h * LANES:(h + 1) * LANES]
        v_ref[pl.ds(h, tm, stride=nh), :] = v[:, h * LANES:(h + 1) * LANES]
    if transposed:
        kb_ref[...] = k.astype(BF16)


def _inproj0(x, g, w, cw, aw, qscale, transposed):
    m, d = x.shape
    tm = _row_tile(m, ROW_TILE)
    n = w.shape[1]
    row = lambda width: pl.BlockSpec((tm, width), lambda i: (i, 0))
    const = lambda shape: pl.BlockSpec(shape, lambda i: (0,) * len(shape))
    nh = aw // LANES
    kv_spec = pl.BlockSpec((tm * nh, LANES), lambda i: (i, 0))
    kv_shape = jax.ShapeDtypeStruct((m * nh, LANES), F32)
    out_specs = [row(cw), row(cw), kv_spec, kv_spec, row(aw)]
    out_shape = [jax.ShapeDtypeStruct((m, cw), F32), jax.ShapeDtypeStruct((m, cw), F32), kv_shape, kv_shape,
                 jax.ShapeDtypeStruct((m, aw), BF16)]
    in_specs = [row(d), const((1, d)), const((d, n))]
    args = [x, g, w]
    if transposed:
        out_specs += [pl.BlockSpec((aw, tm), lambda i: (0, i))] * 2
        out_shape += [jax.ShapeDtypeStruct((aw, m), BF16)] * 2
    return pl.pallas_call(
        functools.partial(_inproj0_kernel, cw=cw, aw=aw, qscale=qscale, transposed=transposed),
        grid=(m // tm,),
        in_specs=in_specs,
        out_specs=out_specs,
        out_shape=out_shape,
        compiler_params=_params("parallel"),
        name="inproj0",
    )(*args)


def _attn_prompt_kernel(lam_ref, sg_ref, qt_ref, k_ref, vt_ref, o_ref, *, tq, tk, dh, lam_init,
                        after_loop=None):
    nch = tk // tq
    g = pl.program_id(2)
    qt = qt_ref[...]
    sub = lax.broadcasted_iota(jnp.int32, (2 * dh, tq), 0)
    zero = jnp.zeros((2 * dh, tq), qt.dtype)
    qqs = []
    for c in range(nch):
        qc = qt[:, c * tq:(c + 1) * tq]
        qqs.append(jnp.concatenate([jnp.where(sub < dh, qc, zero), jnp.where(sub >= dh, qc, zero)], axis=1))
    cols = 2 * tq

    def run(blocks, carry):
        scores, vts, nks = [], [], []
        for j, masked in blocks:
            off = pl.multiple_of(j * tk, tk)
            k = k_ref[pl.ds(off, tk), :]
            vts.append(vt_ref[:, pl.ds(off, tk)])
            nks.append([(c + 1) * tq if masked else tk for c in range(nch)])
            scores.append([_dot(k[:nks[-1][c]], qqs[c]) for c in range(nch)])
        for (j, masked), sc, vt, nk in zip(blocks, scores, vts, nks):
            probs, stats = [], []
            for c in range(nch):
                m, l, _ = carry[c]
                s = sc[c]
                if masked:
                    r = lax.broadcasted_iota(jnp.int32, (nk[c], cols), 0)
                    cc = lax.broadcasted_iota(jnp.int32, (nk[c], cols), 1)
                    s = jnp.where(r <= c * tq + jnp.where(cc >= tq, cc - tq, cc), s, -jnp.inf)
                m_new = jnp.maximum(m, jnp.max(s, axis=0, keepdims=True))
                alpha = jnp.exp2(m - m_new)
                p = jnp.exp2(s - m_new)
                stats.append((m_new, alpha, alpha * l + jnp.sum(p, axis=0, keepdims=True)))
                probs.append(p.astype(BF16))
            out = []
            for c in range(nch):
                m_new, alpha, l = stats[c]
                acc = alpha * carry[c][2] + _dot(vt[:, :nk[c]], probs[c])
                out.append((m_new, l, acc))
            carry = tuple(out)
        return carry

    init = tuple((jnp.full((1, cols), -jnp.inf, F32), jnp.zeros((1, cols), F32),
                  jnp.zeros((2 * dh, cols), F32)) for _ in range(nch))
    carry = lax.fori_loop(0, g // 2, lambda jj, c: run([(2 * jj, False), (2 * jj + 1, False)], c), init)
    if after_loop is not None:
        after_loop()
    carry = lax.cond(g % 2 == 1,
                     lambda c: run([(g - 1, False), (g, True)], c),
                     lambda c: run([(g, True)], c), carry)

    lam = _lam_value(lam_ref, lam_init)
    for c in range(nch):
        _, l, acc = carry[c]
        o = acc / l
        a = o[:, :tq] - lam * o[:, tq:]
        y = a * lax.rsqrt(jnp.mean(a * a, axis=0, keepdims=True) + EPS) * sg_ref[...] * (1.0 - lam_init)
        o_ref[c * tq:(c + 1) * tq, :] = jnp.transpose(y).astype(o_ref.dtype)


def _attn_prompt(qt, kb, vt, lam, subln_g, b, s, nh, dh, lam_init):
    tk = _row_tile(s, ATTN_K_TILE)
    tq = _row_tile(tk, ATTN_Q_TILE)
    hw = 2 * dh
    assert hw == LANES
    nq = s // tk
    return pl.pallas_call(
        functools.partial(_attn_prompt_kernel, tq=tq, tk=tk, dh=dh, lam_init=lam_init),
        grid=(b, nh, nq),
        in_specs=[pl.BlockSpec(lam.shape, lambda bi, h, qi: (0, 0)),
                  pl.BlockSpec((hw, 1), lambda bi, h, qi: (0, 0)),
                  pl.BlockSpec((hw, tk), lambda bi, h, qi: (h, bi * nq + qi)),
                  pl.BlockSpec((None, s, hw), lambda bi, h, qi: (bi, 0, h)),
                  pl.BlockSpec((hw, s), lambda bi, h, qi: (h, bi))],
        out_specs=pl.BlockSpec((None, tk, hw), lambda bi, h, qi: (bi, qi, h)),
        out_shape=jax.ShapeDtypeStruct((b, s, nh * hw), BF16),
        compiler_params=_params("parallel", "parallel", "arbitrary"),
        name="attn_prompt",
    )(lam, subln_g.reshape(hw, 1), qt, kb, vt)


def _decode_step(n, n_steps, total, q, new_kv, store_out, pt_ref, lam_ref, sg_ref, ck_ref, cv_ref,
                 kbuf, vbuf, sem, m_ref, l_ref, acc_ref, *, pages, layer_idx, nh, nq, lam_init):
    step = n % n_steps
    n_slots = kbuf.shape[0]
    ahead = n_slots - 1
    slot = n % n_slots
    cols = q.shape[1]
    hw = kbuf.shape[3]
    ptok = kbuf.shape[2] // nh

    def page_reads(t, r, sl):
        pg = pt_ref[t // n_steps, (t % n_steps) * pages + r]
        return (pltpu.make_async_copy(ck_ref.at[layer_idx, pg], kbuf.at[sl, r], sem.at[0, sl]),
                pltpu.make_async_copy(cv_ref.at[layer_idx, pg], vbuf.at[sl, r], sem.at[1, sl]))

    @pl.when(n == 0)
    def _():
        for t in range(ahead):
            for r in range(pages):
                for cp in page_reads(t, r, t):
                    cp.start()

    fetch = jnp.where(n + ahead < total, n + ahead, n)
    fetch_slot = (n + ahead) % n_slots

    for r in range(pages):
        for cp in page_reads(n, r, slot):
            cp.wait()

    rows = nh * 2 * nq
    eye = (lax.broadcasted_iota(jnp.int32, (rows, cols), 0)
           == lax.broadcasted_iota(jnp.int32, (rows, cols), 1))

    def as_column(row):
        return jnp.sum(jnp.where(eye, row, 0.0), axis=1, keepdims=True)

    def tokens(buf, r):
        return jnp.concatenate([buf[slot, r, pl.ds(h, ptok, stride=nh), :] for h in range(nh)],
                               axis=1).astype(BF16)

    def update(scores, values):
        m_old = m_ref[...]
        m_new = m_old
        for s in scores:
            m_new = jnp.maximum(m_new, jnp.max(s, axis=0, keepdims=True))
        alpha = jnp.exp2(m_old - m_new)
        l = alpha * l_ref[...]
        acc = as_column(alpha) * acc_ref[...]
        values = iter(values)
        for s in scores:
            p = jnp.exp2(s - m_new)
            l = l + jnp.sum(p, axis=0, keepdims=True)
            if s.shape[0] % LANES:
                acc = acc + _dot_tn(p.astype(BF16), next(values))[:rows]
                continue
            pt = jnp.transpose(p)[:rows].astype(BF16)
            for lo in range(0, s.shape[0], ptok):
                acc = acc + _dot(pt[:, lo:lo + ptok], next(values))
        m_ref[...] = m_new
        l_ref[...] = l
        acc_ref[...] = acc

    @pl.when(step == 0)
    def _():
        m_ref[...] = jnp.full(m_ref.shape, -jnp.inf, F32)
        l_ref[...] = jnp.zeros(l_ref.shape, F32)
        acc_ref[...] = jnp.zeros(acc_ref.shape, F32)

    scores = []
    for r0 in range(0, pages, DECODE_SCORE_PAGES):
        group = range(r0, min(r0 + DECODE_SCORE_PAGES, pages))
        for r in group:
            for cp in page_reads(fetch, r, fetch_slot):
                cp.start()
        scores.append(_dot(jnp.concatenate([tokens(kbuf, r) for r in group], axis=0), q))
    update(scores, [tokens(vbuf, r) for r in range(pages)])

    @pl.when(n == total - 1)
    def _():
        for d in range(1, ahead + 1):
            for r in range(pages):
                for cp in page_reads(n, r, (n + d) % n_slots):
                    cp.wait()

    @pl.when(step == n_steps - 1)
    def _():
        kn, vn = new_kv()
        tok = lax.broadcasted_iota(jnp.int32, (kn.shape[0], cols), 0)
        qi = lax.broadcasted_iota(jnp.int32, (kn.shape[0], cols), 1) % nq
        s = jnp.where((tok <= qi) & (tok < nq), _dot(kn.astype(BF16), q), -jnp.inf)
        update([s], [vn.astype(BF16)])
        o = acc_ref[...] / as_column(l_ref[...])
        lam = _lam_value(lam_ref, lam_init)
        for h in range(nh):
            oh = o[h * 2 * nq:(h + 1) * 2 * nq, h * hw:(h + 1) * hw]
            a = oh - lam * pltpu.roll(oh, nq, 0)
            y = _rms(a, sg_ref[...]) * (1.0 - lam_init)
            store_out(h, y[:nq])


def _attn_decode_kernel(pt_ref, lam_ref, sg_ref, q_ref, kn_ref, vn_ref, ck_ref, cv_ref, o_ref,
                        kbuf, vbuf, sem, m_ref, l_ref, acc_ref, **static):
    n_steps = pl.num_programs(1)
    hw = kbuf.shape[3]

    def store_out(h, tile):
        o_ref[:, h * hw:(h + 1) * hw] = tile.astype(o_ref.dtype)

    _decode_step(pl.program_id(0) * n_steps + pl.program_id(1), n_steps, pl.num_programs(0) * n_steps,
                 q_ref[...], lambda: (kn_ref[...], vn_ref[...]), store_out,
                 pt_ref, lam_ref, sg_ref, ck_ref, cv_ref, kbuf, vbuf, sem, m_ref, l_ref, acc_ref, **static)


def _decode_operands(qb, k_new, v_new, cache_k, cache_v, nb, nq, nh, dh):
    hw = 2 * dh
    rows = nh * 2 * nq
    assert 2 * nq == SUBLANES and hw == LANES and rows <= DECODE_COLS and nq <= DECODE_NEW_ROWS
    page = cache_k.shape[2]
    qh = jnp.tile(qb.reshape(nb, nq, nh, hw).transpose(0, 2, 1, 3), (1, 1, 2, 1))
    r = lax.broadcasted_iota(jnp.int32, (2 * nq, hw), 0)
    c = lax.broadcasted_iota(jnp.int32, (2 * nq, hw), 1)
    qh = jnp.where((r // nq) == (c // dh), qh, jnp.zeros_like(qh))
    qcols = qh.transpose(0, 1, 3, 2)
    hrow = lax.broadcasted_iota(jnp.int32, (nh, 1, nh, 1), 0)
    hcol = lax.broadcasted_iota(jnp.int32, (nh, 1, nh, 1), 2)
    qblk = jnp.where(hrow == hcol, qcols[:, :, :, None, :], jnp.zeros_like(qcols[:, :, :, None, :]))
    qblk = jnp.pad(qblk.reshape(nb, nh * hw, rows), ((0, 0), (0, 0), (0, DECODE_COLS - rows)))
    pad_new = lambda a: jnp.pad(a.reshape(nb, nq, nh * hw), ((0, 0), (0, DECODE_NEW_ROWS - nq), (0, 0)))
    view = lambda cache: cache.reshape(cache.shape[0], cache.shape[1], page * nh, hw)
    return qblk, pad_new(k_new), pad_new(v_new), view(cache_k), view(cache_v)


def _decode_scratch(pages, prow, hw, nh, rows, dtype):
    page_buf = pltpu.VMEM((DECODE_SLOTS, pages, prow, hw), dtype)
    return [page_buf, page_buf, pltpu.SemaphoreType.DMA((2, DECODE_SLOTS)),
            pltpu.VMEM((1, DECODE_COLS), F32), pltpu.VMEM((1, DECODE_COLS), F32),
            pltpu.VMEM((rows, nh * hw), F32)]


def _attn_decode(qb, k_new, v_new, cache_k, cache_v, layer_idx, page_table, lam, subln_g,
                 nb, nq, nh, dh, lam_init):
    hw = 2 * dh
    n_pages = page_table.shape[1]
    page = cache_k.shape[2]
    pages = min(DECODE_PAGES, n_pages)
    assert n_pages % pages == 0
    rows = nh * 2 * nq
    qbd, kn, vn, ck, cv = _decode_operands(qb, k_new, v_new, cache_k, cache_v, nb, nq, nh, dh)

    const = lambda shape: pl.BlockSpec(shape, lambda b, p, pt: (0,) * len(shape))
    per_b = lambda shape: pl.BlockSpec((None,) + shape, lambda b, p, pt: (b,) + (0,) * len(shape))
    in_hbm = pl.BlockSpec(memory_space=pl.ANY)
    assert nb * (n_pages // pages) >= DECODE_SLOTS
    grid_spec = pltpu.PrefetchScalarGridSpec(
        num_scalar_prefetch=1,
        grid=(nb, n_pages // pages),
        in_specs=[const(lam.shape), const((1, hw)), per_b(qbd.shape[1:]),
                  per_b(kn.shape[1:]), per_b(vn.shape[1:]), in_hbm, in_hbm],
        out_specs=per_b((nq, nh * hw)),
        scratch_shapes=_decode_scratch(pages, page * nh, hw, nh, rows, cache_k.dtype),
    )
    out = pl.pallas_call(
        functools.partial(_attn_decode_kernel, pages=pages, layer_idx=layer_idx, nh=nh, nq=nq,
                          lam_init=lam_init),
        grid_spec=grid_spec,
        out_shape=jax.ShapeDtypeStruct((nb, nq, nh * hw), F32),
        compiler_params=_params("arbitrary", "arbitrary"),
        name="attn_decode",
    )(page_table, lam, subln_g, qbd, kn, vn, ck, cv)
    return out.reshape(nb * nq, nh * hw)


def _attn_prompt_decode_kernel(pt_ref, lam_ref, sgc_ref, sgr_ref, qt_ref, k_ref, vt_ref, qd_ref, kn_ref, vn_ref,
                               ck_ref, cv_ref, o_ref, od_ref, kbuf, vbuf, sem, m_ref, l_ref, acc_ref,
                               *, prompt_static, decode_static, per_step, n_steps):
    grid_step = ((pl.program_id(0) * pl.num_programs(1) + pl.program_id(1)) * pl.num_programs(2)
                 + pl.program_id(2))
    total = pl.num_programs(0) * pl.num_programs(1) * pl.num_programs(2) * per_step
    hw = kbuf.shape[3]
    def decode(i):
        n = grid_step * per_step + i
        seq = n // n_steps

        def store_out(h, tile):
            od_ref[seq, :, h * hw:(h + 1) * hw] = tile.astype(od_ref.dtype)

        _decode_step(n, n_steps, total, qd_ref[seq], lambda: (kn_ref[seq], vn_ref[seq]), store_out,
                     pt_ref, lam_ref, sgr_ref, ck_ref, cv_ref, kbuf, vbuf, sem, m_ref, l_ref, acc_ref,
                     **decode_static)

    first = per_step // 2
    for i in range(first):
        decode(i)

    def second_half():
        for i in range(first, per_step):
            decode(i)

    _attn_prompt_kernel(lam_ref, sgc_ref, qt_ref, k_ref, vt_ref, o_ref, after_loop=second_half, **prompt_static)


def _attn_prompt_decode(qt, kb, vt, b, s, qb, k_new, v_new, cache_k, cache_v, layer_idx, page_table,
                        lam, subln_g, nb, nq, nh, dh, lam_init):
    hw = 2 * dh
    tk = _row_tile(s, ATTN_K_TILE)
    tq = _row_tile(tk, ATTN_Q_TILE)
    ng = s // tk
    n_pages = page_table.shape[1]
    page = cache_k.shape[2]
    pages = min(DECODE_PAGES, n_pages)
    n_steps = n_pages // pages
    grid = (b, nh, ng)
    decode_total = nb * n_steps
    if n_pages % pages or decode_total % (b * nh * ng) or decode_total < DECODE_SLOTS:
        return None
    per_step = decode_total // (b * nh * ng)
    rows = nh * 2 * nq
    qbd, kn, vn, ck, cv = _decode_operands(qb, k_new, v_new, cache_k, cache_v, nb, nq, nh, dh)
    const = lambda shape: pl.BlockSpec(shape, lambda bi, h, g, pt: (0,) * len(shape))
    once = lambda shape: pl.BlockSpec(shape, lambda bi, h, g, pt: (0,) * len(shape),
                                      pipeline_mode=pl.Buffered(1))
    in_hbm = pl.BlockSpec(memory_space=pl.ANY)
    grid_spec = pltpu.PrefetchScalarGridSpec(
        num_scalar_prefetch=1,
        grid=grid,
        in_specs=[const(lam.shape), const((hw, 1)), const((1, hw)),
                  pl.BlockSpec((hw, tk), lambda bi, h, g, pt: (h, bi * ng + g)),
                  pl.BlockSpec((None, s, hw), lambda bi, h, g, pt: (bi, 0, h)),
                  pl.BlockSpec((hw, s), lambda bi, h, g, pt: (h, bi)),
                  once(qbd.shape), once(kn.shape), once(vn.shape), in_hbm, in_hbm],
        out_specs=[pl.BlockSpec((None, tk, hw), lambda bi, h, g, pt: (bi, g, h)),
                   const((nb, nq, nh * hw))],
        scratch_shapes=_decode_scratch(pages, page * nh, hw, nh, rows, cache_k.dtype),
    )
    attn, attn_d = pl.pallas_call(
        functools.partial(
            _attn_prompt_decode_kernel, per_step=per_step, n_steps=n_steps,
            prompt_static=dict(tq=tq, tk=tk, dh=dh, lam_init=lam_init),
            decode_static=dict(pages=pages, layer_idx=layer_idx, nh=nh, nq=nq, lam_init=lam_init)),
        grid_spec=grid_spec,
        out_shape=[jax.ShapeDtypeStruct((b, s, nh * hw), BF16),
                   jax.ShapeDtypeStruct((nb, nq, nh * hw), F32)],
        compiler_params=_params("arbitrary", "arbitrary", "arbitrary", vmem_limit_bytes=VMEM_LIMIT_LARGE_BYTES),
        name="attn_prompt_decode",
    )(page_table, lam, subln_g.reshape(hw, 1), subln_g.reshape(1, hw), qt, kb, vt, qbd, kn, vn, ck, cv)
    return attn, attn_d.reshape(nb * nq, nh * hw)


def _conv_taps(cx, prev1, prev2, cw_ref):
    return cw_ref[0:1, :] * prev2 + cw_ref[1:2, :] * prev1 + cw_ref[2:3, :] * cx


def _ffn_rows(x, g_in_ref, wgu_ref, wd_ref, g_out_ref, act_ref, tf):
    xn = _rms(x, g_in_ref[...]).astype(BF16)
    dff = wd_ref.shape[0]
    for lo in range(0, dff, tf):
        gate = _dot(xn, wgu_ref[:, lo:lo + tf])
        up = _dot(xn, wgu_ref[:, dff + lo:dff + lo + tf])
        act_ref[:, lo:lo + tf] = (gate * jax.nn.sigmoid(gate) * up).astype(BF16)
    return x + _rms(_dot(act_ref[...], wd_ref[...]), g_out_ref[...])


def _outproj_conv_kernel(gb_ref, cx_ref, at_ref, r_ref, w_ref, g_ref, cw_ref,
                         g_in_ref, wgu_ref, wd_ref, g_out_ref, *rest, seq, carry_mode, tf):
    cx = cx_ref[...]
    tm, cwid = cx.shape
    row = lax.broadcasted_iota(jnp.int32, (tm, 1), 0)
    roll1 = pltpu.roll(cx, 1, 0)
    roll2 = pltpu.roll(cx, 2, 0)
    if carry_mode:
        st_ref, o_ref, act_ref, carry = rest

        @pl.when((pl.program_id(0) * tm) % seq == 0)
        def _():
            carry[0:2, :] = st_ref[...]

        c0 = carry[0:1, :]
        c1 = carry[1:2, :]
        prev1 = jnp.where(row >= 1, roll1, c1)
        prev2 = jnp.where(row >= 2, roll2, jnp.where(row == 1, c1, c0))
        carry[0:2, :] = cx[tm - 2:tm, :]
    else:
        e1_ref, e2_ref, o_ref, act_ref = rest
        t = row % seq
        prev1 = jnp.where(t >= 1, roll1, e1_ref[...])
        prev2 = jnp.where(t >= 2, roll2, e2_ref[...])
    yconv = (gb_ref[...] * _conv_taps(cx, prev1, prev2, cw_ref)).astype(BF16)
    mix = _dot(yconv, w_ref[0:cwid, :]) + _dot(at_ref[...].astype(BF16), w_ref[cwid:, :])
    h = r_ref[...] + _rms(mix, g_ref[...])
    o_ref[...] = _ffn_rows(h, g_in_ref, wgu_ref, wd_ref, g_out_ref, act_ref, tf)


def _outproj_conv_ffn(gb, cx, attn, resid, w, g, conv_w, state, seq, g_in, w_gu, w_down, layer, g_out):
    m, d = resid.shape
    cwid = gb.shape[1]
    dff = w_down.shape[1]
    tm = _row_tile(m, FFN_ROW_TILE)
    tf = _row_tile(dff, FFN_COL_TILE)
    carry_mode = seq % tm == 0
    row = lambda width: pl.BlockSpec((tm, width), lambda i: (i, 0))
    const = lambda shape: pl.BlockSpec(shape, lambda i: (0,) * len(shape), pipeline_mode=pl.Buffered(1))
    resident = lambda shape: pl.BlockSpec((None,) + shape, lambda i: (layer, 0, 0),
                                          pipeline_mode=pl.Buffered(1))
    in_specs = [row(cwid), row(cwid), row(attn.shape[1]), row(d), const(w.shape), const((1, d)),
                const(conv_w.shape), const((1, d)), resident((d, 2 * dff)), resident((dff, d)), const((1, d))]
    scratch = [pltpu.VMEM((tm, dff), BF16)]
    if carry_mode:
        extra = [state]
        in_specs += [pl.BlockSpec((None, 2, cwid), lambda i: ((i * tm) // seq, 0, 0))]
        scratch += [pltpu.VMEM((SUBLANES, cwid), F32)]
    else:
        assert tm % seq == 0 and seq >= 2
        nb = m // seq
        zeros = jnp.zeros((nb, seq - 1, cwid), F32)
        e1 = jnp.concatenate([state[:, 1:2], zeros], axis=1).reshape(m, cwid)
        e2 = jnp.concatenate([state[:, 0:1], state[:, 1:2], zeros[:, 1:]], axis=1).reshape(m, cwid)
        extra = [e1, e2]
        in_specs += [row(cwid), row(cwid)]
    return pl.pallas_call(
        functools.partial(_outproj_conv_kernel, seq=seq, carry_mode=carry_mode, tf=tf),
        grid=(m // tm,),
        in_specs=in_specs,
        out_specs=row(d),
        out_shape=jax.ShapeDtypeStruct((m, d), F32),
        scratch_shapes=scratch,
        compiler_params=_params("arbitrary"),
        name="outproj_conv_ffn",
    )(gb, cx, attn, resid, w, g, conv_w, g_in, w_gu, w_down, g_out, *extra)


def _outproj_kernel(x_ref, r_ref, w_ref, g_ref, o_ref):
    o_ref[...] = r_ref[...] + _rms(_dot(x_ref[...], w_ref[...]), g_ref[...])


def _outproj(x, resid, w, g):
    m, d = resid.shape
    tm = _row_tile(m, ROW_TILE)
    row = lambda width: pl.BlockSpec((tm, width), lambda i: (i, 0))
    return pl.pallas_call(
        _outproj_kernel,
        grid=(m // tm,),
        in_specs=[row(x.shape[1]), row(d), pl.BlockSpec(w.shape, lambda i: (0, 0)),
                  pl.BlockSpec((1, d), lambda i: (0, 0))],
        out_specs=row(d),
        out_shape=jax.ShapeDtypeStruct((m, d), F32),
        compiler_params=_params("parallel"),
        name="outproj",
    )(x, resid, w, g)


def _ffn_kernel(x_ref, g_in_ref, wgu_ref, wd_ref, g_out_ref, o_ref, act_ref, *, tf):
    o_ref[...] = _ffn_rows(x_ref[...], g_in_ref, wgu_ref, wd_ref, g_out_ref, act_ref, tf)


def _ffn(x, g_in, w_gu, w_down, layer, g_out):
    m, d = x.shape
    dff = w_down.shape[1]
    tm = _row_tile(m, FFN_ROW_TILE)
    tf = _row_tile(dff, FFN_COL_TILE)
    row = pl.BlockSpec((tm, d), lambda i: (i, 0))
    vec = pl.BlockSpec((1, d), lambda i: (0, 0))
    resident = lambda shape: pl.BlockSpec((None,) + shape, lambda i: (layer, 0, 0),
                                          pipeline_mode=pl.Buffered(1))
    return pl.pallas_call(
        functools.partial(_ffn_kernel, tf=tf),
        grid=(m // tm,),
        in_specs=[row, vec, resident((d, 2 * dff)), resident((dff, d)), vec],
        out_specs=row,
        out_shape=jax.ShapeDtypeStruct((m, d), F32),
        scratch_shapes=[pltpu.VMEM((tm, dff), BF16)],
        compiler_params=_params("parallel"),
        name="ffn",
    )(x, g_in, w_gu, w_down, g_out)


def _inproj1_kernel(x_ref, g_ref, w_ref, wg_ref, b_ref, q_ref, k_ref, v_ref, o_ref, gt_ref,
                    *, d, nh, kscale):
    xn = _rms(x_ref[...], g_ref[...]).astype(BF16)
    q_ref[...] = _dot(xn, w_ref[:, 0:d]).astype(BF16)
    k_ref[...] = (_dot(xn, w_ref[:, d:2 * d]) * kscale).astype(BF16)
    v_ref[...] = _dot(xn, w_ref[:, 2 * d:3 * d]).astype(BF16)
    o_ref[...] = _dot(xn, w_ref[:, 3 * d:4 * d])
    gates = _dot(xn, wg_ref[...]) + b_ref[...]
    lane = lax.broadcasted_iota(jnp.int32, gates.shape, 1)
    gt_ref[...] = jnp.where(lane < nh, gates, jax.nn.log_sigmoid(gates))


def _inproj1(x, g, w, wg, b, nh, kscale):
    m, d = x.shape
    tm = _row_tile(m, ROW_TILE)
    row = lambda width: pl.BlockSpec((tm, width), lambda i: (i, 0))
    const = lambda shape: pl.BlockSpec(shape, lambda i: (0,) * len(shape))
    shapes = [(d, BF16), (d, BF16), (d, BF16), (d, F32), (LANES, F32)]
    return pl.pallas_call(
        functools.partial(_inproj1_kernel, d=d, nh=nh, kscale=kscale),
        grid=(m // tm,),
        in_specs=[row(d), const((1, d)), const(w.shape), const(wg.shape), const((1, LANES))],
        out_specs=[row(wd) for wd, _ in shapes],
        out_shape=[jax.ShapeDtypeStruct((m, wd), dt) for wd, dt in shapes],
        compiler_params=_params("parallel"),
        name="inproj1",
    )(x, g, w, wg, b)


def _gate_vectors(gt, nh, tri, eye, row, col):
    L = gt.shape[0]
    if L % LANES == 0:
        lane = lax.broadcasted_iota(jnp.int32, gt.shape, 1)
        lf = jnp.where((lane >= nh) & (lane < 2 * nh), gt, 0.0)
        ones = jnp.where(tri, 1.0, 0.0).astype(BF16)
        hi = lf.astype(BF16)
        rest = lf - hi.astype(F32)
        mid = rest.astype(BF16)
        lo = (rest - mid.astype(F32)).astype(BF16)
        bc = _dot(ones, hi) + _dot(ones, mid) + _dot(ones, lo)
        gt_t = jnp.transpose(gt)
        bc_t = jnp.transpose(bc)
        li_c = [gt[:, h:h + 1] for h in range(nh)]
        li_r = [gt_t[h:h + 1, :] for h in range(nh)]
        bc_c = [bc[:, nh + h:nh + h + 1] for h in range(nh)]
        bc_r = [bc_t[nh + h:nh + h + 1, :] for h in range(nh)]
        return li_c, li_r, bc_c, bc_r
    li_c, li_r, bc_c, bc_r = [], [], [], []
    for h in range(nh):
        li = gt[:, h:h + 1]
        lf = gt[:, nh + h:nh + h + 1]
        lf_r = jnp.sum(jnp.where(eye, lf, 0.0), axis=0, keepdims=True)
        li_c.append(li)
        li_r.append(jnp.sum(jnp.where(eye, li, 0.0), axis=0, keepdims=True))
        bc_c.append(jnp.sum(jnp.where(tri, lf_r, 0.0), axis=1, keepdims=True))
        bc_r.append(jnp.sum(jnp.where(row <= col, lf, 0.0), axis=0, keepdims=True))
    return li_c, li_r, bc_c, bc_r


def _mlstm_kernel(q_ref, k_ref, v_ref, o_ref, gt_ref, c0_ref, n0_ref, m0_ref, mhg_ref, *rest,
                  chunk, nchunks, nh, single_step, fuse_out):
    if fuse_out:
        r_ref, w_ref, g_ref, h_ref, c_out, n_out, m_out = rest
    else:
        h_ref, c_out, n_out, m_out = rest
    gb = q_ref.shape[0]
    dh = q_ref.shape[2] // nh
    chains = [(bi, h) for bi in range(gb) for h in range(nh)]

    if not single_step:
        @pl.when(pl.program_id(1) == 0)
        def _():
            c_out[...] = c0_ref[...]
            n_out[...] = n0_ref[...]
            m_out[...] = m0_ref[...]

    L = chunk
    row = lax.broadcasted_iota(jnp.int32, (L, L), 0)
    col = lax.broadcasted_iota(jnp.int32, (L, L), 1)
    tri = col <= row
    eye = col == row

    for c in range(nchunks):
        sl = pl.ds(c * L, L)
        state_in = (c0_ref, n0_ref, m0_ref) if (single_step and c == 0) else (c_out, n_out, m_out)
        part = []
        gates = {}
        for bi, h in chains:
            lanes = slice(h * dh, (h + 1) * dh)
            if bi not in gates:
                gates[bi] = _gate_vectors(gt_ref[bi, sl, :], nh, tri, eye, row, col)
            li_c, li_r, bc_c, bc_r = [a[h] for a in gates[bi]]
            x_r = li_r - bc_r
            cm_c = jnp.max(jnp.where(tri, x_r, -jnp.inf), axis=1, keepdims=True)
            m_prev = state_in[2][bi, h, 0:1, 0:1]
            mt_c = bc_c + jnp.maximum(m_prev, cm_c)
            dmat = jnp.exp(jnp.where(tri, (bc_c - mt_c) + x_r, -jnp.inf))
            inter = jnp.exp(bc_c + m_prev - mt_c)
            q = q_ref[bi, sl, lanes]
            k = k_ref[bi, sl, lanes]
            c_old = state_in[0][bi, h]
            part.append(dict(lanes=lanes, li_c=li_c, bc_c=bc_c, mt_c=mt_c, m_prev=m_prev, dmat=dmat,
                             inter=inter, q=q, k=k, c_old=c_old, n_old=state_in[1][bi, h],
                             qk=_dot_nt(q, k), qc=_dot(q, c_old.astype(BF16))))

        mix = [None] * gb
        for (bi, h), p in zip(chains, part):
            v = v_ref[bi, sl, p["lanes"]]
            sqk = p["qk"] * p["dmat"]
            num = p["inter"] * p["qc"] + _dot(sqk.astype(BF16), v)
            den = (p["inter"] * jnp.sum(p["q"].astype(F32) * p["n_old"], axis=1, keepdims=True)
                   + jnp.sum(sqk, axis=1, keepdims=True))
            h_til = num / jnp.maximum(jnp.abs(den), jnp.exp(-p["mt_c"]))
            hcell = jax.nn.sigmoid(o_ref[bi, sl, p["lanes"]]) * h_til
            hc = hcell - jnp.mean(hcell, axis=-1, keepdims=True)
            hn = hc * lax.rsqrt(jnp.mean(hc * hc, axis=-1, keepdims=True) + EPS) * mhg_ref[:, p["lanes"]]
            if fuse_out:
                part_mix = _dot(hn.astype(BF16), w_ref[p["lanes"], :])
                mix[bi] = part_mix if mix[bi] is None else mix[bi] + part_mix
            else:
                h_ref[bi, sl, p["lanes"]] = hn.astype(h_ref.dtype)
        if fuse_out:
            for bi in range(gb):
                h_ref[bi, sl, :] = r_ref[bi, sl, :] + _rms(mix[bi], g_ref[...])

        for (bi, h), p in zip(chains, part):
            v = v_ref[bi, sl, p["lanes"]]
            m_end = p["mt_c"][L - 1:L, :]
            bc_end = p["bc_c"][L - 1:L, :]
            w_end = jnp.exp(bc_end - p["bc_c"] + p["li_c"] - m_end)
            decay = jnp.exp(bc_end + p["m_prev"] - m_end)
            wk = w_end * p["k"].astype(F32)
            c_out[bi, h] = decay * p["c_old"] + _dot_tn(wk.astype(BF16), v)
            n_out[bi, h] = decay * p["n_old"] + jnp.sum(wk, axis=0, keepdims=True)
            m_out[bi, h] = jnp.broadcast_to(m_end, m_out.shape[2:])


def _mlstm(q, k, v, o, gt, c0, n0, m0, mh_g, resid, w_out, g_post, nh, chunk, rows_per_step):
    b, s, d = q.shape
    dh = d // nh
    fuse_out = chunk >= LANES
    gb = min(MLSTM_SEQS if fuse_out else MLSTM_SEQS_SHORT, b)
    assert rows_per_step % chunk == 0 and s % rows_per_step == 0 and b % gb == 0
    blk = lambda width: pl.BlockSpec((gb, rows_per_step, width), lambda bi, si: (bi, si, 0))
    st = lambda r, w: pl.BlockSpec((gb, nh, r, w), lambda bi, si: (bi, 0, 0, 0))
    vec = pl.BlockSpec((1, d), lambda bi, si: (0, 0))
    in_specs = [blk(d), blk(d), blk(d), blk(d), blk(LANES), st(dh, dh), st(1, dh), st(SUBLANES, LANES), vec]
    args = [q, k, v, o, gt, c0, n0, m0, mh_g]
    if fuse_out:
        in_specs += [blk(d), pl.BlockSpec((d, d), lambda bi, si: (0, 0)), vec]
        args += [resid, w_out, g_post]
    out = pl.pallas_call(
        functools.partial(_mlstm_kernel, chunk=chunk, nchunks=rows_per_step // chunk, nh=nh,
                          single_step=(s == rows_per_step), fuse_out=fuse_out),
        grid=(b // gb, s // rows_per_step),
        in_specs=in_specs,
        out_specs=[blk(d), st(dh, dh), st(1, dh), st(SUBLANES, LANES)],
        out_shape=[jax.ShapeDtypeStruct((b, s, d), F32 if fuse_out else BF16),
                   jax.ShapeDtypeStruct((b, nh, dh, dh), F32),
                   jax.ShapeDtypeStruct((b, nh, 1, dh), F32),
                   jax.ShapeDtypeStruct((b, nh, SUBLANES, LANES), F32)],
        compiler_params=_params("parallel", "arbitrary"),
        name="mlstm",
    )(*args)
    if fuse_out:
        return out
    h = _outproj(out[0].reshape(b * s, d), resid.reshape(b * s, d), w_out, g_post).reshape(b, s, d)
    return (h,) + tuple(out[1:])


def _attention(requests):
    prompts = [r for r in requests if "qt" in r]
    cached = [r for r in requests if "qb" in r]
    out = {}
    if len(prompts) == 1 and len(cached) == 1:
        p, c = prompts[0], cached[0]
        both = _attn_prompt_decode(p["qt"], p["kb"], p["vt"], p["b"], p["s"], c["qb"], c["k_new"], c["v_new"],
                                   c["cache_k"], c["cache_v"], c["layer_idx"], c["page_table"], c["lam"],
                                   c["subln_g"], c["nb"], c["nq"], c["nh"], c["dh"], c["lam_init"])
        if both is not None:
            out[id(p)] = both[0].reshape(p["b"] * p["s"], -1)
            out[id(c)] = both[1]
    for r in requests:
        if id(r) in out:
            continue
        if "qt" in r:
            out[id(r)] = _attn_prompt(r["qt"], r["kb"], r["vt"], r["lam"], r["subln_g"], r["b"], r["s"],
                                      r["nh"], r["dh"], r["lam_init"]).reshape(r["b"] * r["s"], -1)
        else:
            out[id(r)] = _attn_decode(r["qb"], r["k_new"], r["v_new"], r["cache_k"], r["cache_v"],
                                      r["layer_idx"], r["page_table"], r["lam"], r["subln_g"], r["nb"],
                                      r["nq"], r["nh"], r["dh"], r["lam_init"])
    return [out[id(r)] for r in requests]


def _run_trunks(trunks):
    results = [None] * len(trunks)
    requests = [None] * len(trunks)

    def advance(i, value):
        try:
            requests[i] = trunks[i].send(value)
        except StopIteration as done:
            requests[i] = None
            results[i] = done.value

    for i in range(len(trunks)):
        advance(i, None)
    while any(r is not None for r in requests):
        live = [i for i, r in enumerate(requests) if r is not None]
        for i, attn in zip(live, _attention([requests[i] for i in live])):
            advance(i, attn)
    return results


def _trunk(x, state_conv, cache_k, cache_v, page_table, state_c, state_n, state_m, wts):
    b, s, d = x.shape
    m = b * s
    depth = wts["norms"].shape[0]
    cw = wts["conv_w0"].shape[2]
    dh = wts["lam0"].shape[2]
    aw = (wts["w_in0"].shape[2] - 3 * cw) // 3
    nh_a = aw // (2 * dh)
    nh_m = state_c.shape[2]
    dh_m = d // nh_m
    h = x.reshape(m, d)
    ks, vs, convs, cs, ns, ms = [], [], [], [], [], []
    for layer in range(depth):
        j = layer // 2
        g = wts["norms"][layer]
        if layer % 2 == 0:
            lam_init = _lambda_init(layer)
            prompt = cache_k is None
            proj = _inproj0(h, g[0:1], wts["w_in0"][j], cw, aw, dh ** -0.5 * LOG2_E, prompt)
            gb, cx, k, v = proj[:4]
            sg = wts["subln_g0"][j].reshape(1, 2 * dh)
            common = dict(layer_idx=j, lam=wts["lam0"][j], subln_g=sg, nh=nh_a, dh=dh, lam_init=lam_init)
            if prompt:
                kb, qt, vt = proj[4:]
                attn = yield dict(common, qt=qt, kb=kb.reshape(b, s, aw), vt=vt, b=b, s=s)
            else:
                attn = yield dict(common, qb=proj[4], k_new=k, v_new=v, cache_k=cache_k, cache_v=cache_v,
                                  page_table=page_table, nb=b, nq=s)
            h = _outproj_conv_ffn(gb, cx, attn, h, wts["w_out0"][j], g[1:2], wts["conv_w0"][j],
                                  state_conv[j], s, g[2:3], wts["w_gu"], wts["w_down"], layer, g[3:4])
            ks.append(k.reshape(b, s, nh_a, 2 * dh))
            vs.append(v.reshape(b, s, nh_a, 2 * dh))
            convs.append(cx.reshape(b, s, cw)[:, s - 2:, :])
        else:
            q, k, v, o, gt = _inproj1(h, g[0:1], wts["w_in1"][j], wts["w_in1_gates"][j], wts["b_if1"][j],
                                      nh_m, dh_m ** -0.5)
            chunk = MLSTM_CHUNK if s % MLSTM_CHUNK == 0 else s
            sp = s
            r3 = lambda a: a.reshape(b, s, a.shape[1])
            q, k, v, o, gt, resid = r3(q), r3(k), r3(v), r3(o), r3(gt), r3(h)
            if chunk % SUBLANES != 0:
                sp = -(-s // SUBLANES) * SUBLANES
                chunk = sp
                pad = lambda a: jnp.pad(a, ((0, 0), (0, sp - s), (0, 0)))
                lane = lax.broadcasted_iota(jnp.int32, (b, sp - s, LANES), 2)
                gt_pad = jnp.where(lane < nh_m, -jnp.inf, 0.0).astype(F32)
                q, k, v, o, resid = pad(q), pad(k), pad(v), pad(o), pad(resid)
                gt = jnp.concatenate([gt, gt_pad], axis=1)
            rows = min(MLSTM_ROWS, sp)
            m0 = jnp.broadcast_to(state_m[j][:, :, None, None], (b, nh_m, SUBLANES, LANES))
            h3, c_new, n_new, m_new = _mlstm(q, k, v, o, gt, state_c[j], state_n[j][:, :, None, :], m0,
                                             wts["mh_g1"][j].reshape(1, d), resid, wts["w_out1"][j], g[1:2],
                                             nh_m, chunk, rows)
            h = h3[:, :s].reshape(m, d)
            cs.append(c_new)
            ns.append(n_new[:, :, 0, :])
            ms.append(m_new[:, :, 0, 0])
            h = _ffn(h, g[2:3], wts["w_gu"], wts["w_down"], layer, g[3:4])
    return (h.reshape(b, s, d), jnp.stack(ks), jnp.stack(vs), jnp.stack(convs),
            jnp.stack(cs), jnp.stack(ns), jnp.stack(ms))


def kernel(x_prompt, x_sample, cache_k, cache_v, state_conv, state_C, state_n, state_m, page_table,
           norms, w_in0, conv_w0, lam0, subln_g0, w_out0, w_in1, b_if1, mh_g1, w_out1, w_gu, w_down):
    d = x_prompt.shape[-1]
    nh_m = state_C.shape[2]
    n_odd = w_in1.shape[0]
    w_gates = jnp.pad(w_in1[:, :, 4 * d:], ((0, 0), (0, 0), (0, LANES - 2 * nh_m))).astype(BF16)
    b_pad = jnp.pad(b_if1.astype(F32), ((0, 0), (0, LANES - 2 * nh_m))).reshape(n_odd, 1, LANES)
    wts = dict(
        norms=norms.astype(F32), w_in0=w_in0.astype(BF16), conv_w0=conv_w0, lam0=lam0.astype(F32),
        subln_g0=subln_g0.astype(F32), w_out0=w_out0.astype(BF16), w_in1=w_in1[:, :, :4 * d].astype(BF16),
        w_in1_gates=w_gates, b_if1=b_pad, mh_g1=mh_g1.astype(F32), w_out1=w_out1.astype(BF16),
        w_gu=w_gu.astype(BF16), w_down=w_down.astype(BF16))
    bp = x_prompt.shape[0]
    n_even = state_conv.shape[0]
    conv0 = jnp.zeros((n_even, bp) + state_conv.shape[2:], x_prompt.dtype)
    c0 = jnp.zeros((n_odd, bp) + state_C.shape[2:], F32)
    n0 = jnp.zeros((n_odd, bp) + state_n.shape[2:], F32)
    m0 = jnp.zeros((n_odd, bp) + state_m.shape[2:], F32)
    (y_p, k_p, v_p, conv_p, c_p, n_p, m_p), (y_s, k_s, v_s, conv_s, c_s, n_s, m_s) = _run_trunks([
        _trunk(x_prompt, conv0, None, None, None, c0, n0, m0, wts),
        _trunk(x_sample, state_conv, cache_k, cache_v, page_table, state_C.astype(F32),
               state_n.astype(F32), state_m.astype(F32), wts)])
    return (y_p, y_s, k_p, v_p, conv_p, c_p, n_p, m_p, k_s, v_s, conv_s, c_s, n_s, m_s)
```

```python
import functools
import math

import jax
import jax.numpy as jnp
from jax import lax
from jax.experimental import pallas as pl
from jax.experimental.pallas import tpu as pltpu

EPS = 1e-6
LOG2_E = math.log2(math.e)
F32 = jnp.float32
BF16 = jnp.bfloat16

V7X_VMEM_BYTES = 64 * 1024 * 1024
VMEM_LIMIT_BYTES = V7X_VMEM_BYTES - 8 * 1024 * 1024
VMEM_LIMIT_LARGE_BYTES = V7X_VMEM_BYTES - 4 * 1024 * 1024
LANES = 128
SUBLANES = 8

ROW_TILE = 512
FFN_ROW_TILE = 512
FFN_COL_TILE = 256
ATTN_Q_TILE = 256
ATTN_K_TILE = 1024
DECODE_PAGES = 16
DECODE_SLOTS = 3
DECODE_COLS = 128
DECODE_NEW_ROWS = 16
DECODE_SCORE_PAGES = 4
MLSTM_CHUNK = 512
MLSTM_ROWS = 512
MLSTM_SEQS = 1
MLSTM_SEQS_SHORT = 2


def _params(*semantics, vmem_limit_bytes=VMEM_LIMIT_BYTES):
    return pltpu.CompilerParams(dimension_semantics=semantics, vmem_limit_bytes=vmem_limit_bytes)


def _dot(a, b):
    return jnp.dot(a, b, preferred_element_type=F32)


def _dot_nt(a, b):
    return lax.dot_general(a, b, (((1,), (1,)), ((), ())), preferred_element_type=F32)


def _dot_tn(a, b):
    return lax.dot_general(a, b, (((0,), (0,)), ((), ())), preferred_element_type=F32)


def _rms(x, g):
    return x * lax.rsqrt(jnp.mean(x * x, axis=-1, keepdims=True) + EPS) * g


def _lambda_init(layer):
    return 0.8 - 0.6 * math.exp(-0.3 * layer)


def _lam_value(lam_ref, lam_init):
    lv = lam_ref[...]
    a = jnp.sum(lv[0:1] * lv[1:2], axis=-1, keepdims=True)
    b = jnp.sum(lv[2:3] * lv[3:4], axis=-1, keepdims=True)
    return jnp.exp(a) - jnp.exp(b) + lam_init


def _row_tile(m, want):
    t = min(want, m)
    assert m % t == 0, (m, t)
    return t


def _inproj0_kernel(x_ref, g_ref, w_ref, *rest, cw, aw, qscale, transposed):
    xn = _rms(x_ref[...], g_ref[...]).astype(BF16)

    def proj(lo, width):
        return _dot(xn, w_ref[:, lo:lo + width].astype(BF16))

    q = proj(3 * cw, aw) * qscale
    k = proj(3 * cw + aw, aw)
    v = proj(3 * cw + 2 * aw, aw)
    if transposed:
        gb_ref, cx_ref, k_ref, v_ref, kb_ref, qt_ref, vt_ref = rest
        qt_ref[...] = jnp.transpose(q).astype(BF16)
        vt_ref[...] = jnp.transpose(v).astype(BF16)
    else:
        gb_ref, cx_ref, k_ref, v_ref, q_ref = rest
        q_ref[...] = q.astype(BF16)
    gb_ref[...] = proj(0, cw)
    cx_ref[...] = proj(cw, cw) * proj(2 * cw, cw)
    tm = k.shape[0]
    nh = aw // LANES
    for h in range(nh):
        k_ref[pl.ds(h, tm, stride=nh), :] = k[:, h * LANES:(h + 1) * LANES]
        v_ref[pl.ds(h, tm, stride=nh), :] = v[:, h * LANES:(h + 1) * LANES]
    if transposed:
        kb_ref[...] = k.astype(BF16)


def _inproj0(x, g, w, cw, aw, qscale, transposed):
    m, d = x.shape
    tm = _row_tile(m, ROW_TILE)
    n = w.shape[1]
    row = lambda width: pl.BlockSpec((tm, width), lambda i: (i, 0))
    const = lambda shape: pl.BlockSpec(shape, lambda i: (0,) * len(shape))
    nh = aw // LANES
    kv_spec = pl.BlockSpec((tm * nh, LANES), lambda i: (i, 0))
    kv_shape = jax.ShapeDtypeStruct((m * nh, LANES), F32)
    out_specs = [row(cw), row(cw), kv_spec, kv_spec, row(aw)]
    out_shape = [jax.ShapeDtypeStruct((m, cw), F32), jax.ShapeDtypeStruct((m, cw), F32), kv_shape, kv_shape,
                 jax.ShapeDtypeStruct((m, aw), BF16)]
    in_specs = [row(d), const((1, d)), pl.BlockSpec((d, n), lambda i: (0, 0), pipeline_mode=pl.Buffered(1))]
    args = [x, g, w]
    if transposed:
        out_specs += [pl.BlockSpec((aw, tm), lambda i: (0, i))] * 2
        out_shape += [jax.ShapeDtypeStruct((aw, m), BF16)] * 2
    return pl.pallas_call(
        functools.partial(_inproj0_kernel, cw=cw, aw=aw, qscale=qscale, transposed=transposed),
        grid=(m // tm,),
        in_specs=in_specs,
        out_specs=out_specs,
        out_shape=out_shape,
        compiler_params=_params("parallel"),
        name="inproj0",
    )(*args)


def _attn_prompt_kernel(lam_ref, sg_ref, qt_ref, k_ref, vt_ref, o_ref, *, tq, tk, dh, lam_init,
                        after_loop=None):
    nch = tk // tq
    g = pl.program_id(2)
    qt = qt_ref[...]
    sub = lax.broadcasted_iota(jnp.int32, (2 * dh, tq), 0)
    zero = jnp.zeros((2 * dh, tq), qt.dtype)
    qqs = []
    for c in range(nch):
        qc = qt[:, c * tq:(c + 1) * tq]
        qqs.append(jnp.concatenate([jnp.where(sub < dh, qc, zero), jnp.where(sub >= dh, qc, zero)], axis=1))
    cols = 2 * tq

    def run(blocks, carry):
        scores, vts, nks = [], [], []
        for j, masked in blocks:
            off = pl.multiple_of(j * tk, tk)
            k = k_ref[pl.ds(off, tk), :]
            vts.append(vt_ref[:, pl.ds(off, tk)])
            nks.append([(c + 1) * tq if masked else tk for c in range(nch)])
            scores.append([_dot(k[:nks[-1][c]], qqs[c]) for c in range(nch)])
        for (j, masked), sc, vt, nk in zip(blocks, scores, vts, nks):
            probs, stats = [], []
            for c in range(nch):
                m, l, _ = carry[c]
                s = sc[c]
                if masked:
                    r = lax.broadcasted_iota(jnp.int32, (nk[c], cols), 0)
                    cc = lax.broadcasted_iota(jnp.int32, (nk[c], cols), 1)
                    s = jnp.where(r <= c * tq + jnp.where(cc >= tq, cc - tq, cc), s, -jnp.inf)
                m_new = jnp.maximum(m, jnp.max(s, axis=0, keepdims=True))
                alpha = jnp.exp2(m - m_new)
                p = jnp.exp2(s - m_new)
                stats.append((m_new, alpha, alpha * l + jnp.sum(p, axis=0, keepdims=True)))
                probs.append(p.astype(BF16))
            out = []
            for c in range(nch):
                m_new, alpha, l = stats[c]
                acc = alpha * carry[c][2] + _dot(vt[:, :nk[c]], probs[c])
                out.append((m_new, l, acc))
            carry = tuple(out)
        return carry

    init = tuple((jnp.full((1, cols), -jnp.inf, F32), jnp.zeros((1, cols), F32),
                  jnp.zeros((2 * dh, cols), F32)) for _ in range(nch))
    carry = lax.fori_loop(0, g // 2, lambda jj, c: run([(2 * jj, False), (2 * jj + 1, False)], c), init)
    if after_loop is not None:
        after_loop()
    carry = lax.cond(g % 2 == 1,
                     lambda c: run([(g - 1, False), (g, True)], c),
                     lambda c: run([(g, True)], c), carry)

    lam = _lam_value(lam_ref, lam_init)
    for c in range(nch):
        _, l, acc = carry[c]
        o = acc / l
        a = o[:, :tq] - lam * o[:, tq:]
        y = a * lax.rsqrt(jnp.mean(a * a, axis=0, keepdims=True) + EPS) * sg_ref[...] * (1.0 - lam_init)
        o_ref[c * tq:(c + 1) * tq, :] = jnp.transpose(y).astype(o_ref.dtype)


def _attn_prompt(qt, kb, vt, lam, subln_g, b, s, nh, dh, lam_init):
    tk = _row_tile(s, ATTN_K_TILE)
    tq = _row_tile(tk, ATTN_Q_TILE)
    hw = 2 * dh
    assert hw == LANES
    nq = s // tk
    return pl.pallas_call(
        functools.partial(_attn_prompt_kernel, tq=tq, tk=tk, dh=dh, lam_init=lam_init),
        grid=(b, nh, nq),
        in_specs=[pl.BlockSpec(lam.shape, lambda bi, h, qi: (0, 0)),
                  pl.BlockSpec((hw, 1), lambda bi, h, qi: (0, 0)),
                  pl.BlockSpec((hw, tk), lambda bi, h, qi: (h, bi * nq + qi)),
                  pl.BlockSpec((None, s, hw), lambda bi, h, qi: (bi, 0, h)),
                  pl.BlockSpec((hw, s), lambda bi, h, qi: (h, bi))],
        out_specs=pl.BlockSpec((None, tk, hw), lambda bi, h, qi: (bi, qi, h)),
        out_shape=jax.ShapeDtypeStruct((b, s, nh * hw), BF16),
        compiler_params=_params("parallel", "parallel", "arbitrary"),
        name="attn_prompt",
    )(lam, subln_g.reshape(hw, 1), qt, kb, vt)


def _decode_step(n, n_steps, total, q, new_kv, store_out, pt_ref, lam_ref, sg_ref, ck_ref, cv_ref,
                 kbuf, vbuf, sem, m_ref, l_ref, acc_ref, *, pages, layer_idx, nh, nq, lam_init):
    step = n % n_steps
    n_slots = kbuf.shape[0]
    ahead = n_slots - 1
    slot = n % n_slots
    cols = q.shape[1]
    hw = kbuf.shape[3]
    ptok = kbuf.shape[2] // nh

    def page_reads(t, r, sl):
        pg = pt_ref[t // n_steps, (t % n_steps) * pages + r]
        return (pltpu.make_async_copy(ck_ref.at[layer_idx, pg], kbuf.at[sl, r], sem.at[0, sl]),
                pltpu.make_async_copy(cv_ref.at[layer_idx, pg], vbuf.at[sl, r], sem.at[1, sl]))

    @pl.when(n == 0)
    def _():
        for t in range(ahead):
            for r in range(pages):
                for cp in page_reads(t, r, t):
                    cp.start()

    fetch = jnp.where(n + ahead < total, n + ahead, n)
    fetch_slot = (n + ahead) % n_slots

    for r in range(pages):
        for cp in page_reads(n, r, slot):
            cp.wait()

    rows = nh * 2 * nq
    eye = (lax.broadcasted_iota(jnp.int32, (rows, cols), 0)
           == lax.broadcasted_iota(jnp.int32, (rows, cols), 1))

    def as_column(row):
        return jnp.sum(jnp.where(eye, row, 0.0), axis=1, keepdims=True)

    def tokens(buf, r):
        return jnp.concatenate([buf[slot, r, pl.ds(h, ptok, stride=nh), :] for h in range(nh)],
                               axis=1).astype(BF16)

    def update(scores, values):
        m_old = m_ref[...]
        m_new = m_old
        for s in scores:
            m_new = jnp.maximum(m_new, jnp.max(s, axis=0, keepdims=True))
        alpha = jnp.exp2(m_old - m_new)
        l = alpha * l_ref[...]
        acc = as_column(alpha) * acc_ref[...]
        values = iter(values)
        for s in scores:
            p = jnp.exp2(s - m_new)
            l = l + jnp.sum(p, axis=0, keepdims=True)
            if s.shape[0] % LANES:
                acc = acc + _dot_tn(p.astype(BF16), next(values))[:rows]
                continue
            pt = jnp.transpose(p)[:rows].astype(BF16)
            for lo in range(0, s.shape[0], ptok):
                acc = acc + _dot(pt[:, lo:lo + ptok], next(values))
        m_ref[...] = m_new
        l_ref[...] = l
        acc_ref[...] = acc

    @pl.when(step == 0)
    def _():
        m_ref[...] = jnp.full(m_ref.shape, -jnp.inf, F32)
        l_ref[...] = jnp.zeros(l_ref.shape, F32)
        acc_ref[...] = jnp.zeros(acc_ref.shape, F32)

    scores = []
    for r0 in range(0, pages, DECODE_SCORE_PAGES):
        group = range(r0, min(r0 + DECODE_SCORE_PAGES, pages))
        for r in group:
            for cp in page_reads(fetch, r, fetch_slot):
                cp.start()
        scores.append(_dot(jnp.concatenate([tokens(kbuf, r) for r in group], axis=0), q))
    update(scores, [tokens(vbuf, r) for r in range(pages)])

    @pl.when(n == total - 1)
    def _():
        for d in range(1, ahead + 1):
            for r in range(pages):
                for cp in page_reads(n, r, (n + d) % n_slots):
                    cp.wait()

    @pl.when(step == n_steps - 1)
    def _():
        kn, vn = new_kv()
        tok = lax.broadcasted_iota(jnp.int32, (kn.shape[0], cols), 0)
        qi = lax.broadcasted_iota(jnp.int32, (kn.shape[0], cols), 1) % nq
        s = jnp.where((tok <= qi) & (tok < nq), _dot(kn.astype(BF16), q), -jnp.inf)
        update([s], [vn.astype(BF16)])
        o = acc_ref[...] / as_column(l_ref[...])
        lam = _lam_value(lam_ref, lam_init)
        for h in range(nh):
            oh = o[h * 2 * nq:(h + 1) * 2 * nq, h * hw:(h + 1) * hw]
            a = oh - lam * pltpu.roll(oh, nq, 0)
            y = _rms(a, sg_ref[...]) * (1.0 - lam_init)
            store_out(h, y[:nq])


def _attn_decode_kernel(pt_ref, lam_ref, sg_ref, q_ref, kn_ref, vn_ref, ck_ref, cv_ref, o_ref,
                        kbuf, vbuf, sem, m_ref, l_ref, acc_ref, **static):
    n_steps = pl.num_programs(1)
    hw = kbuf.shape[3]

    def store_out(h, tile):
        o_ref[:, h * hw:(h + 1) * hw] = tile.astype(o_ref.dtype)

    _decode_step(pl.program_id(0) * n_steps + pl.program_id(1), n_steps, pl.num_programs(0) * n_steps,
                 q_ref[...], lambda: (kn_ref[...], vn_ref[...]), store_out,
                 pt_ref, lam_ref, sg_ref, ck_ref, cv_ref, kbuf, vbuf, sem, m_ref, l_ref, acc_ref, **static)


def _decode_operands(qb, k_new, v_new, cache_k, cache_v, nb, nq, nh, dh):
    hw = 2 * dh
    rows = nh * 2 * nq
    assert 2 * nq == SUBLANES and hw == LANES and rows <= DECODE_COLS and nq <= DECODE_NEW_ROWS
    page = cache_k.shape[2]
    qh = jnp.tile(qb.reshape(nb, nq, nh, hw).transpose(0, 2, 1, 3), (1, 1, 2, 1))
    r = lax.broadcasted_iota(jnp.int32, (2 * nq, hw), 0)
    c = lax.broadcasted_iota(jnp.int32, (2 * nq, hw), 1)
    qh = jnp.where((r // nq) == (c // dh), qh, jnp.zeros_like(qh))
    qcols = qh.transpose(0, 1, 3, 2)
    hrow = lax.broadcasted_iota(jnp.int32, (nh, 1, nh, 1), 0)
    hcol = lax.broadcasted_iota(jnp.int32, (nh, 1, nh, 1), 2)
    qblk = jnp.where(hrow == hcol, qcols[:, :, :, None, :], jnp.zeros_like(qcols[:, :, :, None, :]))
    qblk = jnp.pad(qblk.reshape(nb, nh * hw, rows), ((0, 0), (0, 0), (0, DECODE_COLS - rows)))
    pad_new = lambda a: jnp.pad(a.reshape(nb, nq, nh * hw), ((0, 0), (0, DECODE_NEW_ROWS - nq), (0, 0)))
    view = lambda cache: cache.reshape(cache.shape[0], cache.shape[1], page * nh, hw)
    return qblk, pad_new(k_new), pad_new(v_new), view(cache_k), view(cache_v)


def _decode_scratch(pages, prow, hw, nh, rows, dtype):
    page_buf = pltpu.VMEM((DECODE_SLOTS, pages, prow, hw), dtype)
    return [page_buf, page_buf, pltpu.SemaphoreType.DMA((2, DECODE_SLOTS)),
            pltpu.VMEM((1, DECODE_COLS), F32), pltpu.VMEM((1, DECODE_COLS), F32),
            pltpu.VMEM((rows, nh * hw), F32)]


def _attn_decode(qb, k_new, v_new, cache_k, cache_v, layer_idx, page_table, lam, subln_g,
                 nb, nq, nh, dh, lam_init):
    hw = 2 * dh
    n_pages = page_table.shape[1]
    page = cache_k.shape[2]
    pages = min(DECODE_PAGES, n_pages)
    assert n_pages % pages == 0
    rows = nh * 2 * nq
    qbd, kn, vn, ck, cv = _decode_operands(qb, k_new, v_new, cache_k, cache_v, nb, nq, nh, dh)

    const = lambda shape: pl.BlockSpec(shape, lambda b, p, pt: (0,) * len(shape))
    per_b = lambda shape: pl.BlockSpec((None,) + shape, lambda b, p, pt: (b,) + (0,) * len(shape))
    in_hbm = pl.BlockSpec(memory_space=pl.ANY)
    assert nb * (n_pages // pages) >= DECODE_SLOTS
    grid_spec = pltpu.PrefetchScalarGridSpec(
        num_scalar_prefetch=1,
        grid=(nb, n_pages // pages),
        in_specs=[const(lam.shape), const((1, hw)), per_b(qbd.shape[1:]),
                  per_b(kn.shape[1:]), per_b(vn.shape[1:]), in_hbm, in_hbm],
        out_specs=per_b((nq, nh * hw)),
        scratch_shapes=_decode_scratch(pages, page * nh, hw, nh, rows, cache_k.dtype),
    )
    out = pl.pallas_call(
        functools.partial(_attn_decode_kernel, pages=pages, layer_idx=layer_idx, nh=nh, nq=nq,
                          lam_init=lam_init),
        grid_spec=grid_spec,
        out_shape=jax.ShapeDtypeStruct((nb, nq, nh * hw), F32),
        compiler_params=_params("arbitrary", "arbitrary"),
        name="attn_decode",
    )(page_table, lam, subln_g, qbd, kn, vn, ck, cv)
    return out.reshape(nb * nq, nh * hw)


def _attn_prompt_decode_kernel(pt_ref, lam_ref, sgc_ref, sgr_ref, qt_ref, k_ref, vt_ref, qd_ref, kn_ref, vn_ref,
                               ck_ref, cv_ref, o_ref, od_ref, kbuf, vbuf, sem, m_ref, l_ref, acc_ref,
                               *, prompt_static, decode_static, per_step, n_steps):
    grid_step = ((pl.program_id(0) * pl.num_programs(1) + pl.program_id(1)) * pl.num_programs(2)
                 + pl.program_id(2))
    total = pl.num_programs(0) * pl.num_programs(1) * pl.num_programs(2) * per_step
    hw = kbuf.shape[3]
    def decode(i):
        n = grid_step * per_step + i
        seq = n // n_steps

        def store_out(h, tile):
            od_ref[seq, :, h * hw:(h + 1) * hw] = tile.astype(od_ref.dtype)

        _decode_step(n, n_steps, total, qd_ref[seq], lambda: (kn_ref[seq], vn_ref[seq]), store_out,
                     pt_ref, lam_ref, sgr_ref, ck_ref, cv_ref, kbuf, vbuf, sem, m_ref, l_ref, acc_ref,
                     **decode_static)

    first = per_step // 2
    for i in range(first):
        decode(i)

    def second_half():
        for i in range(first, per_step):
            decode(i)

    _attn_prompt_kernel(lam_ref, sgc_ref, qt_ref, k_ref, vt_ref, o_ref, after_loop=second_half, **prompt_static)


def _attn_prompt_decode(qt, kb, vt, b, s, qb, k_new, v_new, cache_k, cache_v, layer_idx, page_table,
                        lam, subln_g, nb, nq, nh, dh, lam_init):
    hw = 2 * dh
    tk = _row_tile(s, ATTN_K_TILE)
    tq = _row_tile(tk, ATTN_Q_TILE)
    ng = s // tk
    n_pages = page_table.shape[1]
    page = cache_k.shape[2]
    pages = min(DECODE_PAGES, n_pages)
    n_steps = n_pages // pages
    grid = (b, nh, ng)
    decode_total = nb * n_steps
    if n_pages % pages or decode_total % (b * nh * ng) or decode_total < DECODE_SLOTS:
        return None
    per_step = decode_total // (b * nh * ng)
    rows = nh * 2 * nq
    qbd, kn, vn, ck, cv = _decode_operands(qb, k_new, v_new, cache_k, cache_v, nb, nq, nh, dh)
    const = lambda shape: pl.BlockSpec(shape, lambda bi, h, g, pt: (0,) * len(shape))
    once = lambda shape: pl.BlockSpec(shape, lambda bi, h, g, pt: (0,) * len(shape),
                                      pipeline_mode=pl.Buffered(1))
    in_hbm = pl.BlockSpec(memory_space=pl.ANY)
    grid_spec = pltpu.PrefetchScalarGridSpec(
        num_scalar_prefetch=1,
        grid=grid,
        in_specs=[const(lam.shape), const((hw, 1)), const((1, hw)),
                  pl.BlockSpec((hw, tk), lambda bi, h, g, pt: (h, bi * ng + g)),
                  pl.BlockSpec((None, s, hw), lambda bi, h, g, pt: (bi, 0, h)),
                  pl.BlockSpec((hw, s), lambda bi, h, g, pt: (h, bi)),
                  once(qbd.shape), once(kn.shape), once(vn.shape), in_hbm, in_hbm],
        out_specs=[pl.BlockSpec((None, tk, hw), lambda bi, h, g, pt: (bi, g, h)),
                   const((nb, nq, nh * hw))],
        scratch_shapes=_decode_scratch(pages, page * nh, hw, nh, rows, cache_k.dtype),
    )
    attn, attn_d = pl.pallas_call(
        functools.partial(
            _attn_prompt_decode_kernel, per_step=per_step, n_steps=n_steps,
            prompt_static=dict(tq=tq, tk=tk, dh=dh, lam_init=lam_init),
            decode_static=dict(pages=pages, layer_idx=layer_idx, nh=nh, nq=nq, lam_init=lam_init)),
        grid_spec=grid_spec,
        out_shape=[jax.ShapeDtypeStruct((b, s, nh * hw), BF16),
                   jax.ShapeDtypeStruct((nb, nq, nh * hw), F32)],
        compiler_params=_params("arbitrary", "arbitrary", "arbitrary", vmem_limit_bytes=VMEM_LIMIT_LARGE_BYTES),
        name="attn_prompt_decode",
    )(page_table, lam, subln_g.reshape(hw, 1), subln_g.reshape(1, hw), qt, kb, vt, qbd, kn, vn, ck, cv)
    return attn, attn_d.reshape(nb * nq, nh * hw)


def _conv_taps(cx, prev1, prev2, cw_ref):
    return cw_ref[0:1, :] * prev2 + cw_ref[1:2, :] * prev1 + cw_ref[2:3, :] * cx


def _ffn_rows(x, g_in_ref, wgu_ref, wd_ref, g_out_ref, act_ref, tf):
    xn = _rms(x, g_in_ref[...]).astype(BF16)
    dff = wd_ref.shape[0]
    for lo in range(0, dff, tf):
        gate = _dot(xn, wgu_ref[:, lo:lo + tf])
        up = _dot(xn, wgu_ref[:, dff + lo:dff + lo + tf])
        act_ref[:, lo:lo + tf] = (gate * jax.nn.sigmoid(gate) * up).astype(BF16)
    return x + _rms(_dot(act_ref[...], wd_ref[...]), g_out_ref[...])


def _outproj_conv_kernel(gb_ref, cx_ref, at_ref, r_ref, w_ref, g_ref, cw_ref,
                         g_in_ref, wgu_ref, wd_ref, g_out_ref, *rest, seq, carry_mode, tf):
    cx = cx_ref[...]
    tm, cwid = cx.shape
    row = lax.broadcasted_iota(jnp.int32, (tm, 1), 0)
    roll1 = pltpu.roll(cx, 1, 0)
    roll2 = pltpu.roll(cx, 2, 0)
    if carry_mode:
        st_ref, o_ref, act_ref, carry = rest

        @pl.when((pl.program_id(0) * tm) % seq == 0)
        def _():
            carry[0:2, :] = st_ref[...]

        c0 = carry[0:1, :]
        c1 = carry[1:2, :]
        prev1 = jnp.where(row >= 1, roll1, c1)
        prev2 = jnp.where(row >= 2, roll2, jnp.where(row == 1, c1, c0))
        carry[0:2, :] = cx[tm - 2:tm, :]
    else:
        e1_ref, e2_ref, o_ref, act_ref = rest
        t = row % seq
        prev1 = jnp.where(t >= 1, roll1, e1_ref[...])
        prev2 = jnp.where(t >= 2, roll2, e2_ref[...])
    yconv = (gb_ref[...] * _conv_taps(cx, prev1, prev2, cw_ref)).astype(BF16)
    mix = _dot(yconv, w_ref[0:cwid, :]) + _dot(at_ref[...].astype(BF16), w_ref[cwid:, :])
    h = r_ref[...] + _rms(mix, g_ref[...])
    o_ref[...] = _ffn_rows(h, g_in_ref, wgu_ref, wd_ref, g_out_ref, act_ref, tf)


def _outproj_conv_ffn(gb, cx, attn, resid, w, g, conv_w, state, seq, g_in, w_gu, w_down, layer, g_out):
    m, d = resid.shape
    cwid = gb.shape[1]
    dff = w_down.shape[1]
    tm = _row_tile(m, FFN_ROW_TILE)
    tf = _row_tile(dff, FFN_COL_TILE)
    carry_mode = seq % tm == 0
    row = lambda width: pl.BlockSpec((tm, width), lambda i: (i, 0))
    const = lambda shape: pl.BlockSpec(shape, lambda i: (0,) * len(shape), pipeline_mode=pl.Buffered(1))
    resident = lambda shape: pl.BlockSpec((None,) + shape, lambda i: (layer, 0, 0),
                                          pipeline_mode=pl.Buffered(1))
    in_specs = [row(cwid), row(cwid), row(attn.shape[1]), row(d), const(w.shape), const((1, d)),
                const(conv_w.shape), const((1, d)), resident((d, 2 * dff)), resident((dff, d)), const((1, d))]
    scratch = [pltpu.VMEM((tm, dff), BF16)]
    if carry_mode:
        extra = [state]
        in_specs += [pl.BlockSpec((None, 2, cwid), lambda i: ((i * tm) // seq, 0, 0))]
        scratch += [pltpu.VMEM((SUBLANES, cwid), F32)]
    else:
        assert tm % seq == 0 and seq >= 2
        nb = m // seq
        zeros = jnp.zeros((nb, seq - 1, cwid), F32)
        e1 = jnp.concatenate([state[:, 1:2], zeros], axis=1).reshape(m, cwid)
        e2 = jnp.concatenate([state[:, 0:1], state[:, 1:2], zeros[:, 1:]], axis=1).reshape(m, cwid)
        extra = [e1, e2]
        in_specs += [row(cwid), row(cwid)]
    return pl.pallas_call(
        functools.partial(_outproj_conv_kernel, seq=seq, carry_mode=carry_mode, tf=tf),
        grid=(m // tm,),
        in_specs=in_specs,
        out_specs=row(d),
        out_shape=jax.ShapeDtypeStruct((m, d), F32),
        scratch_shapes=scratch,
        compiler_params=_params("arbitrary"),
        name="outproj_conv_ffn",
    )(gb, cx, attn, resid, w, g, conv_w, g_in, w_gu, w_down, g_out, *extra)


def _outproj_kernel(x_ref, r_ref, w_ref, g_ref, o_ref):
    o_ref[...] = r_ref[...] + _rms(_dot(x_ref[...], w_ref[...]), g_ref[...])


def _outproj(x, resid, w, g):
    m, d = resid.shape
    tm = _row_tile(m, ROW_TILE)
    row = lambda width: pl.BlockSpec((tm, width), lambda i: (i, 0))
    return pl.pallas_call(
        _outproj_kernel,
        grid=(m // tm,),
        in_specs=[row(x.shape[1]), row(d), pl.BlockSpec(w.shape, lambda i: (0, 0)),
                  pl.BlockSpec((1, d), lambda i: (0, 0))],
        out_specs=row(d),
        out_shape=jax.ShapeDtypeStruct((m, d), F32),
        compiler_params=_params("parallel"),
        name="outproj",
    )(x, resid, w, g)


def _ffn_kernel(x_ref, g_in_ref, wgu_ref, wd_ref, g_out_ref, o_ref, act_ref, *, tf):
    o_ref[...] = _ffn_rows(x_ref[...], g_in_ref, wgu_ref, wd_ref, g_out_ref, act_ref, tf)


def _ffn(x, g_in, w_gu, w_down, layer, g_out):
    m, d = x.shape
    dff = w_down.shape[1]
    tm = _row_tile(m, FFN_ROW_TILE)
    tf = _row_tile(dff, FFN_COL_TILE)
    row = pl.BlockSpec((tm, d), lambda i: (i, 0))
    vec = pl.BlockSpec((1, d), lambda i: (0, 0))
    resident = lambda shape: pl.BlockSpec((None,) + shape, lambda i: (layer, 0, 0),
                                          pipeline_mode=pl.Buffered(1))
    return pl.pallas_call(
        functools.partial(_ffn_kernel, tf=tf),
        grid=(m // tm,),
        in_specs=[row, vec, resident((d, 2 * dff)), resident((dff, d)), vec],
        out_specs=row,
        out_shape=jax.ShapeDtypeStruct((m, d), F32),
        scratch_shapes=[pltpu.VMEM((tm, dff), BF16)],
        compiler_params=_params("parallel"),
        name="ffn",
    )(x, g_in, w_gu, w_down, g_out)


def _inproj1_kernel(x_ref, g_ref, w_ref, wg_ref, b_ref, q_ref, k_ref, v_ref, o_ref, gt_ref,
                    *, d, nh, kscale):
    xn = _rms(x_ref[...], g_ref[...]).astype(BF16)
    q_ref[...] = _dot(xn, w_ref[:, 0:d].astype(BF16)).astype(BF16)
    k_ref[...] = (_dot(xn, w_ref[:, d:2 * d].astype(BF16)) * kscale).astype(BF16)
    v_ref[...] = _dot(xn, w_ref[:, 2 * d:3 * d].astype(BF16)).astype(BF16)
    o_ref[...] = _dot(xn, w_ref[:, 3 * d:4 * d].astype(BF16))
    gates = _dot(xn, wg_ref[...]) + b_ref[...]
    lane = lax.broadcasted_iota(jnp.int32, gates.shape, 1)
    gt_ref[...] = jnp.where(lane < nh, gates, jax.nn.log_sigmoid(gates))


def _inproj1(x, g, w, wg, b, nh, kscale):
    m, d = x.shape
    tm = _row_tile(m, ROW_TILE)
    row = lambda width: pl.BlockSpec((tm, width), lambda i: (i, 0))
    const = lambda shape: pl.BlockSpec(shape, lambda i: (0,) * len(shape))
    shapes = [(d, BF16), (d, BF16), (d, BF16), (d, F32), (LANES, F32)]
    return pl.pallas_call(
        functools.partial(_inproj1_kernel, d=d, nh=nh, kscale=kscale),
        grid=(m // tm,),
        in_specs=[row(d), const((1, d)),
                  pl.BlockSpec(w.shape, lambda i: (0, 0), pipeline_mode=pl.Buffered(1)),
                  const(wg.shape), const((1, LANES))],
        out_specs=[row(wd) for wd, _ in shapes],
        out_shape=[jax.ShapeDtypeStruct((m, wd), dt) for wd, dt in shapes],
        compiler_params=_params("parallel"),
        name="inproj1",
    )(x, g, w, wg, b)


def _gate_vectors(gt, nh, tri, eye, row, col):
    L = gt.shape[0]
    if L % LANES == 0:
        lane = lax.broadcasted_iota(jnp.int32, gt.shape, 1)
        lf = jnp.where((lane >= nh) & (lane < 2 * nh), gt, 0.0)
        ones = jnp.where(tri, 1.0, 0.0).astype(BF16)
        hi = lf.astype(BF16)
        rest = lf - hi.astype(F32)
        mid = rest.astype(BF16)
        lo = (rest - mid.astype(F32)).astype(BF16)
        bc = _dot(ones, hi) + _dot(ones, mid) + _dot(ones, lo)
        gt_t = jnp.transpose(gt)
        bc_t = jnp.transpose(bc)
        li_c = [gt[:, h:h + 1] for h in range(nh)]
        li_r = [gt_t[h:h + 1, :] for h in range(nh)]
        bc_c = [bc[:, nh + h:nh + h + 1] for h in range(nh)]
        bc_r = [bc_t[nh + h:nh + h + 1, :] for h in range(nh)]
        return li_c, li_r, bc_c, bc_r
    li_c, li_r, bc_c, bc_r = [], [], [], []
    for h in range(nh):
        li = gt[:, h:h + 1]
        lf = gt[:, nh + h:nh + h + 1]
        lf_r = jnp.sum(jnp.where(eye, lf, 0.0), axis=0, keepdims=True)
        li_c.append(li)
        li_r.append(jnp.sum(jnp.where(eye, li, 0.0), axis=0, keepdims=True))
        bc_c.append(jnp.sum(jnp.where(tri, lf_r, 0.0), axis=1, keepdims=True))
        bc_r.append(jnp.sum(jnp.where(row <= col, lf, 0.0), axis=0, keepdims=True))
    return li_c, li_r, bc_c, bc_r


def _mlstm_kernel(q_ref, k_ref, v_ref, o_ref, gt_ref, c0_ref, n0_ref, m0_ref, mhg_ref, *rest,
                  chunk, nchunks, nh, single_step, fuse_out):
    if fuse_out:
        r_ref, w_ref, g_ref, h_ref, c_out, n_out, m_out = rest
    else:
        h_ref, c_out, n_out, m_out = rest
    gb = q_ref.shape[0]
    dh = q_ref.shape[2] // nh
    chains = [(bi, h) for bi in range(gb) for h in range(nh)]

    if not single_step:
        @pl.when(pl.program_id(1) == 0)
        def _():
            c_out[...] = c0_ref[...]
            n_out[...] = n0_ref[...]
            m_out[...] = m0_ref[...]

    L = chunk
    row = lax.broadcasted_iota(jnp.int32, (L, L), 0)
    col = lax.broadcasted_iota(jnp.int32, (L, L), 1)
    tri = col <= row
    eye = col == row

    for c in range(nchunks):
        sl = pl.ds(c * L, L)
        state_in = (c0_ref, n0_ref, m0_ref) if (single_step and c == 0) else (c_out, n_out, m_out)
        part = []
        gates = {}
        for bi, h in chains:
            lanes = slice(h * dh, (h + 1) * dh)
            if bi not in gates:
                gates[bi] = _gate_vectors(gt_ref[bi, sl, :], nh, tri, eye, row, col)
            li_c, li_r, bc_c, bc_r = [a[h] for a in gates[bi]]
            x_r = li_r - bc_r
            cm_c = jnp.max(jnp.where(tri, x_r, -jnp.inf), axis=1, keepdims=True)
            m_prev = state_in[2][bi, h, 0:1, 0:1]
            mt_c = bc_c + jnp.maximum(m_prev, cm_c)
            dmat = jnp.exp(jnp.where(tri, (bc_c - mt_c) + x_r, -jnp.inf))
            inter = jnp.exp(bc_c + m_prev - mt_c)
            q = q_ref[bi, sl, lanes]
            k = k_ref[bi, sl, lanes]
            c_old = state_in[0][bi, h]
            part.append(dict(lanes=lanes, li_c=li_c, bc_c=bc_c, mt_c=mt_c, m_prev=m_prev, dmat=dmat,
                             inter=inter, q=q, k=k, c_old=c_old, n_old=state_in[1][bi, h],
                             qk=_dot_nt(q, k), qc=_dot(q, c_old.astype(BF16))))

        mix = [None] * gb
        for (bi, h), p in zip(chains, part):
            v = v_ref[bi, sl, p["lanes"]]
            sqk = p["qk"] * p["dmat"]
            num = p["inter"] * p["qc"] + _dot(sqk.astype(BF16), v)
            den = (p["inter"] * jnp.sum(p["q"].astype(F32) * p["n_old"], axis=1, keepdims=True)
                   + jnp.sum(sqk, axis=1, keepdims=True))
            h_til = num / jnp.maximum(jnp.abs(den), jnp.exp(-p["mt_c"]))
            hcell = jax.nn.sigmoid(o_ref[bi, sl, p["lanes"]]) * h_til
            hc = hcell - jnp.mean(hcell, axis=-1, keepdims=True)
            hn = hc * lax.rsqrt(jnp.mean(hc * hc, axis=-1, keepdims=True) + EPS) * mhg_ref[:, p["lanes"]]
            if fuse_out:
                part_mix = _dot(hn.astype(BF16), w_ref[p["lanes"], :])
                mix[bi] = part_mix if mix[bi] is None else mix[bi] + part_mix
            else:
                h_ref[bi, sl, p["lanes"]] = hn.astype(h_ref.dtype)
        if fuse_out:
            for bi in range(gb):
                h_ref[bi, sl, :] = r_ref[bi, sl, :] + _rms(mix[bi], g_ref[...])

        for (bi, h), p in zip(chains, part):
            v = v_ref[bi, sl, p["lanes"]]
            m_end = p["mt_c"][L - 1:L, :]
            bc_end = p["bc_c"][L - 1:L, :]
            w_end = jnp.exp(bc_end - p["bc_c"] + p["li_c"] - m_end)
            decay = jnp.exp(bc_end + p["m_prev"] - m_end)
            wk = w_end * p["k"].astype(F32)
            c_out[bi, h] = decay * p["c_old"] + _dot_tn(wk.astype(BF16), v)
            n_out[bi, h] = decay * p["n_old"] + jnp.sum(wk, axis=0, keepdims=True)
            m_out[bi, h] = jnp.broadcast_to(m_end, m_out.shape[2:])


def _mlstm(q, k, v, o, gt, c0, n0, m0, mh_g, resid, w_out, g_post, nh, chunk, rows_per_step):
    b, s, d = q.shape
    dh = d // nh
    fuse_out = chunk >= LANES
    gb = min(MLSTM_SEQS if fuse_out else MLSTM_SEQS_SHORT, b)
    assert rows_per_step % chunk == 0 and s % rows_per_step == 0 and b % gb == 0
    blk = lambda width: pl.BlockSpec((gb, rows_per_step, width), lambda bi, si: (bi, si, 0))
    st = lambda r, w: pl.BlockSpec((gb, nh, r, w), lambda bi, si: (bi, 0, 0, 0))
    vec = pl.BlockSpec((1, d), lambda bi, si: (0, 0))
    in_specs = [blk(d), blk(d), blk(d), blk(d), blk(LANES), st(dh, dh), st(1, dh), st(SUBLANES, LANES), vec]
    args = [q, k, v, o, gt, c0, n0, m0, mh_g]
    if fuse_out:
        in_specs += [blk(d), pl.BlockSpec((d, d), lambda bi, si: (0, 0)), vec]
        args += [resid, w_out, g_post]
    out = pl.pallas_call(
        functools.partial(_mlstm_kernel, chunk=chunk, nchunks=rows_per_step // chunk, nh=nh,
                          single_step=(s == rows_per_step), fuse_out=fuse_out),
        grid=(b // gb, s // rows_per_step),
        in_specs=in_specs,
        out_specs=[blk(d), st(dh, dh), st(1, dh), st(SUBLANES, LANES)],
        out_shape=[jax.ShapeDtypeStruct((b, s, d), F32 if fuse_out else BF16),
                   jax.ShapeDtypeStruct((b, nh, dh, dh), F32),
                   jax.ShapeDtypeStruct((b, nh, 1, dh), F32),
                   jax.ShapeDtypeStruct((b, nh, SUBLANES, LANES), F32)],
        compiler_params=_params("parallel", "arbitrary"),
        name="mlstm",
    )(*args)
    if fuse_out:
        return out
    h = _outproj(out[0].reshape(b * s, d), resid.reshape(b * s, d), w_out, g_post).reshape(b, s, d)
    return (h,) + tuple(out[1:])


def _attention(requests):
    prompts = [r for r in requests if "qt" in r]
    cached = [r for r in requests if "qb" in r]
    out = {}
    if len(prompts) == 1 and len(cached) == 1:
        p, c = prompts[0], cached[0]
        both = _attn_prompt_decode(p["qt"], p["kb"], p["vt"], p["b"], p["s"], c["qb"], c["k_new"], c["v_new"],
                                   c["cache_k"], c["cache_v"], c["layer_idx"], c["page_table"], c["lam"],
                                   c["subln_g"], c["nb"], c["nq"], c["nh"], c["dh"], c["lam_init"])
        if both is not None:
            out[id(p)] = both[0].reshape(p["b"] * p["s"], -1)
            out[id(c)] = both[1]
    for r in requests:
        if id(r) in out:
            continue
        if "qt" in r:
            out[id(r)] = _attn_prompt(r["qt"], r["kb"], r["vt"], r["lam"], r["subln_g"], r["b"], r["s"],
                                      r["nh"], r["dh"], r["lam_init"]).reshape(r["b"] * r["s"], -1)
        else:
            out[id(r)] = _attn_decode(r["qb"], r["k_new"], r["v_new"], r["cache_k"], r["cache_v"],
                                      r["layer_idx"], r["page_table"], r["lam"], r["subln_g"], r["nb"],
                                      r["nq"], r["nh"], r["dh"], r["lam_init"])
    return [out[id(r)] for r in requests]


def _run_trunks(trunks):
    results = [None] * len(trunks)
    requests = [None] * len(trunks)

    def advance(i, value):
        try:
            requests[i] = trunks[i].send(value)
        except StopIteration as done:
            requests[i] = None
            results[i] = done.value

    for i in range(len(trunks)):
        advance(i, None)
    while any(r is not None for r in requests):
        live = [i for i, r in enumerate(requests) if r is not None]
        for i, attn in zip(live, _attention([requests[i] for i in live])):
            advance(i, attn)
    return results


def _trunk(x, state_conv, cache_k, cache_v, page_table, state_c, state_n, state_m, wts):
    b, s, d = x.shape
    m = b * s
    depth = wts["norms"].shape[0]
    cw = wts["conv_w0"].shape[2]
    dh = wts["lam0"].shape[2]
    aw = (wts["w_in0"].shape[2] - 3 * cw) // 3
    nh_a = aw // (2 * dh)
    nh_m = state_c.shape[2]
    dh_m = d // nh_m
    h = x.reshape(m, d)
    ks, vs, convs, cs, ns, ms = [], [], [], [], [], []
    for layer in range(depth):
        j = layer // 2
        g = wts["norms"][layer]
        if layer % 2 == 0:
            lam_init = _lambda_init(layer)
            prompt = cache_k is None
            proj = _inproj0(h, g[0:1], wts["w_in0"][j], cw, aw, dh ** -0.5 * LOG2_E, prompt)
            gb, cx, k, v = proj[:4]
            sg = wts["subln_g0"][j].reshape(1, 2 * dh)
            common = dict(layer_idx=j, lam=wts["lam0"][j], subln_g=sg, nh=nh_a, dh=dh, lam_init=lam_init)
            if prompt:
                kb, qt, vt = proj[4:]
                attn = yield dict(common, qt=qt, kb=kb.reshape(b, s, aw), vt=vt, b=b, s=s)
            else:
                attn = yield dict(common, qb=proj[4], k_new=k, v_new=v, cache_k=cache_k, cache_v=cache_v,
                                  page_table=page_table, nb=b, nq=s)
            h = _outproj_conv_ffn(gb, cx, attn, h, wts["w_out0"][j], g[1:2], wts["conv_w0"][j],
                                  state_conv[j], s, g[2:3], wts["w_gu"], wts["w_down"], layer, g[3:4])
            ks.append(k.reshape(b, s, nh_a, 2 * dh))
            vs.append(v.reshape(b, s, nh_a, 2 * dh))
            convs.append(cx.reshape(b, s, cw)[:, s - 2:, :])
        else:
            q, k, v, o, gt = _inproj1(h, g[0:1], wts["w_in1"][j], wts["w_in1_gates"][j], wts["b_if1"][j],
                                      nh_m, dh_m ** -0.5)
            chunk = MLSTM_CHUNK if s % MLSTM_CHUNK == 0 else s
            sp = s
            r3 = lambda a: a.reshape(b, s, a.shape[1])
            q, k, v, o, gt, resid = r3(q), r3(k), r3(v), r3(o), r3(gt), r3(h)
            if chunk % SUBLANES != 0:
                sp = -(-s // SUBLANES) * SUBLANES
                chunk = sp
                pad = lambda a: jnp.pad(a, ((0, 0), (0, sp - s), (0, 0)))
                lane = lax.broadcasted_iota(jnp.int32, (b, sp - s, LANES), 2)
                gt_pad = jnp.where(lane < nh_m, -jnp.inf, 0.0).astype(F32)
                q, k, v, o, resid = pad(q), pad(k), pad(v), pad(o), pad(resid)
                gt = jnp.concatenate([gt, gt_pad], axis=1)
            rows = min(MLSTM_ROWS, sp)
            m0 = jnp.broadcast_to(state_m[j][:, :, None, None], (b, nh_m, SUBLANES, LANES))
            h3, c_new, n_new, m_new = _mlstm(q, k, v, o, gt, state_c[j], state_n[j][:, :, None, :], m0,
                                             wts["mh_g1"][j].reshape(1, d), resid, wts["w_out1"][j], g[1:2],
                                             nh_m, chunk, rows)
            h = h3[:, :s].reshape(m, d)
            cs.append(c_new)
            ns.append(n_new[:, :, 0, :])
            ms.append(m_new[:, :, 0, 0])
            h = _ffn(h, g[2:3], wts["w_gu"], wts["w_down"], layer, g[3:4])
    return (h.reshape(b, s, d), jnp.stack(ks), jnp.stack(vs), jnp.stack(convs),
            jnp.stack(cs), jnp.stack(ns), jnp.stack(ms))


def kernel(x_prompt, x_sample, cache_k, cache_v, state_conv, state_C, state_n, state_m, page_table,
           norms, w_in0, conv_w0, lam0, subln_g0, w_out0, w_in1, b_if1, mh_g1, w_out1, w_gu, w_down):
    d = x_prompt.shape[-1]
    nh_m = state_C.shape[2]
    n_odd = w_in1.shape[0]
    w_gates = jnp.pad(w_in1[:, :, 4 * d:], ((0, 0), (0, 0), (0, LANES - 2 * nh_m))).astype(BF16)
    b_pad = jnp.pad(b_if1.astype(F32), ((0, 0), (0, LANES - 2 * nh_m))).reshape(n_odd, 1, LANES)
    wts = dict(
        norms=norms.astype(F32), w_in0=w_in0, conv_w0=conv_w0, lam0=lam0.astype(F32),
        subln_g0=subln_g0.astype(F32), w_out0=w_out0.astype(BF16), w_in1=w_in1,
        w_in1_gates=w_gates, b_if1=b_pad, mh_g1=mh_g1.astype(F32), w_out1=w_out1.astype(BF16),
        w_gu=w_gu.astype(BF16), w_down=w_down.astype(BF16))
    bp = x_prompt.shape[0]
    n_even = state_conv.shape[0]
    conv0 = jnp.zeros((n_even, bp) + state_conv.shape[2:], x_prompt.dtype)
    c0 = jnp.zeros((n_odd, bp) + state_C.shape[2:], F32)
    n0 = jnp.zeros((n_odd, bp) + state_n.shape[2:], F32)
    m0 = jnp.zeros((n_odd, bp) + state_m.shape[2:], F32)
    (y_p, k_p, v_p, conv_p, c_p, n_p, m_p), (y_s, k_s, v_s, conv_s, c_s, n_s, m_s) = _run_trunks([
        _trunk(x_prompt, conv0, None, None, None, c0, n0, m0, wts),
        _trunk(x_sample, state_conv, cache_k, cache_v, page_table, state_C.astype(F32),
               state_n.astype(F32), state_m.astype(F32), wts)])
    return (y_p, y_s, k_p, v_p, conv_p, c_p, n_p, m_p, k_s, v_s, conv_s, c_s, n_s, m_s)
```
